```python
import jax, jax.numpy as jnp
from jax import lax
import numpy as np

D_MODEL = 2048
BATCH = 8
SEQ = 2048
DEPTH = 1

HEAD_DIM = 128
D_MIX = D_MODEL
N_HEADS_DIL = (D_MIX // 2) // HEAD_DIM
D_DIL = N_HEADS_DIL * HEAD_DIM
DIL_PATTERNS = ((128, 1), (512, 4), (2048, 16))
BLOCK = 128
V_HEAD = 128
N_HEADS_MLA = (D_MIX - D_DIL) // V_HEAD
D_MLA = N_HEADS_MLA * V_HEAD
Q_LORA = 512
KV_LORA = 512
QK_NOPE = 128
QK_ROPE = 64
ROPE_THETA = 10000.0
D_IN = 3 * D_DIL + Q_LORA + KV_LORA + QK_ROPE
IN_SPLITS = [D_DIL, 2 * D_DIL, 3 * D_DIL, 3 * D_DIL + Q_LORA, 3 * D_DIL + Q_LORA + KV_LORA]
PEER_HEADS = 8
PEER_NKEYS = 128
PEER_N = PEER_NKEYS * PEER_NKEYS
PEER_QDIM = 256
PEER_TOPK = 16
PEER_CHUNK = 128
PEER_V_STD = 0.5

EPS = 1e-6
NEG = -1e30
f32 = jnp.float32

kernel_name = "hymba_dilated_mla_peer_block"


def rmsnorm(x, g):
    xf = x.astype(f32)
    y = xf * lax.rsqrt(jnp.mean(xf * xf, axis=-1, keepdims=True) + EPS)
    return (y * g.astype(f32)).astype(x.dtype)


def alibi_slopes(n):
    return 2.0 ** (-8.0 * jnp.arange(1, n + 1, dtype=f32) / n)


def rope(x, pos):
    half = x.shape[-1] // 2
    freqs = ROPE_THETA ** (-jnp.arange(half, dtype=f32) / half)
    ang = pos.astype(f32)[:, None] * freqs[None, :]
    cos, sin = jnp.cos(ang), jnp.sin(ang)
    x1, x2 = x[..., :half].astype(f32), x[..., half:].astype(f32)
    return jnp.concatenate([x1 * cos - x2 * sin, x1 * sin + x2 * cos], axis=-1).astype(x.dtype)


def dilated_window_attn(q, k, v, slopes, window, dilation):
    b, h, s, hd = q.shape
    steps = window // dilation
    L = s // dilation

    def split(t):
        return t.reshape(b, h, L, dilation, hd).transpose(0, 1, 3, 2, 4)

    nb = -(-L // BLOCK)
    pad = nb * BLOCK - L
    padcfg = ((0, 0), (0, 0), (0, 0), (0, pad), (0, 0))
    qs = jnp.pad(split(q), padcfg).reshape(b, h, dilation, nb, BLOCK, hd)
    ks = jnp.pad(split(k), padcfg).reshape(b, h, dilation, nb, BLOCK, hd)
    vs = jnp.pad(split(v), padcfg).reshape(b, h, dilation, nb, BLOCK, hd)

    def with_prev(t):
        prev = jnp.pad(t, ((0, 0), (0, 0), (0, 0), (1, 0), (0, 0), (0, 0)))[:, :, :, :-1]
        return jnp.concatenate([prev, t], axis=4)

    kb, vb = with_prev(ks), with_prev(vs)
    sc = jnp.einsum('bhrnqd,bhrnkd->bhrnqk', qs, kb).astype(f32) * (hd ** -0.5)
    qi = jnp.arange(BLOCK)[:, None]
    kj = jnp.arange(2 * BLOCK)[None, :]
    delta = qi + BLOCK - kj
    key_idx = jnp.arange(nb)[:, None, None] * BLOCK - BLOCK + kj
    valid = (delta >= 0) & (delta <= steps) & (key_idx >= 0)
    bias = -slopes[:, None, None, None, None] * (delta * dilation).astype(f32)
    sc = jnp.where(valid, sc + bias, NEG)
    m = jnp.max(sc, axis=-1, keepdims=True)
    p = jnp.exp(sc - m)
    l = jnp.sum(p, axis=-1, keepdims=True)
    o = jnp.einsum('bhrnqk,bhrnkd->bhrnqd', (p / l).astype(v.dtype), vb)
    lse = (m + jnp.log(l))[..., 0]

    def merge(t):
        t = t.reshape(b, h, dilation, nb * BLOCK, *t.shape[5:])[:, :, :, :L]
        t = jnp.moveaxis(t, 2, 3)
        return t.reshape(b, h, s, *t.shape[4:])

    return merge(o), merge(lse)


def dilated_mixture_attention(q, k, v, slopes):
    res = [dilated_window_attn(q, k, v, slopes, w, d) for (w, d) in DIL_PATTERNS]
    outs = jnp.stack([r[0] for r in res], 0).astype(f32)
    lses = jnp.stack([r[1] for r in res], 0)
    wts = jax.nn.softmax(lses, axis=0)
    return jnp.einsum('pbhs,pbhsd->bhsd', wts, outs).astype(q.dtype)


def causal_block_attention(q, k, v, scale):
    b, h, s, dq = q.shape
    nblk = s // BLOCK
    qb = q.reshape(b, h, nblk, BLOCK, dq).transpose(2, 0, 1, 3, 4)
    kpos = jnp.arange(s)

    def one(args):
        qi, i = args
        sc = jnp.einsum('bhqd,bhkd->bhqk', qi, k).astype(f32) * scale
        qpos = i * BLOCK + jnp.arange(BLOCK)
        sc = jnp.where(kpos[None, :] <= qpos[:, None], sc, NEG)
        p = jax.nn.softmax(sc, axis=-1)
        return jnp.einsum('bhqk,bhkd->bhqd', p.astype(v.dtype), v)

    o = lax.map(one, (qb, jnp.arange(nblk)))
    return o.transpose(1, 2, 0, 3, 4).reshape(b, h, s, v.shape[-1])


def mla_attention(c_q, c_kv, k_rope, q_a_norm, kv_a_norm, w_uq, w_ukv, pos):
    b, s, _ = c_q.shape
    H = N_HEADS_MLA
    q = (rmsnorm(c_q, q_a_norm) @ w_uq).reshape(b, s, H, QK_NOPE + QK_ROPE).transpose(0, 2, 1, 3)
    kv = (rmsnorm(c_kv, kv_a_norm) @ w_ukv).reshape(b, s, H, QK_NOPE + V_HEAD).transpose(0, 2, 1, 3)
    q_nope, q_pe = q[..., :QK_NOPE], q[..., QK_NOPE:]
    k_nope, v = kv[..., :QK_NOPE], kv[..., QK_NOPE:]
    q_pe = rope(q_pe, pos)
    k_pe = jnp.broadcast_to(rope(k_rope, pos)[:, None], (b, H, s, QK_ROPE))
    qf = jnp.concatenate([q_nope, q_pe], axis=-1)
    kf = jnp.concatenate([k_nope, k_pe], axis=-1)
    o = causal_block_attention(qf, kf, v, (QK_NOPE + QK_ROPE) ** -0.5)
    return o.transpose(0, 2, 1, 3).reshape(b, s, D_MLA)


def peer_ffn(xn, w_pq, sub_keys, peer_u, peer_v):
    b, s, d = xn.shape
    T = b * s
    t = xn.reshape(T, d)
    q = (t @ w_pq).reshape(T, PEER_HEADS, 2, PEER_QDIM // 2)
    sc = jnp.einsum('thpc,hpnc->thpn', q, sub_keys).astype(f32)
    top_s, top_i = lax.top_k(sc, PEER_TOPK)
    cand_s = (top_s[:, :, 0, :, None] + top_s[:, :, 1, None, :]).reshape(T, PEER_HEADS, PEER_TOPK * PEER_TOPK)
    cand_i = (top_i[:, :, 0, :, None] * PEER_NKEYS + top_i[:, :, 1, None, :]).reshape(T, PEER_HEADS, PEER_TOPK * PEER_TOPK)
    best_s, best_p = lax.top_k(cand_s, PEER_TOPK)
    idx = jnp.take_along_axis(cand_i, best_p, axis=-1).reshape(T, PEER_HEADS * PEER_TOPK)
    gate = jax.nn.softmax(best_s, axis=-1).reshape(T, PEER_HEADS * PEER_TOPK)
    nchunk = T // PEER_CHUNK

    def expert_chunk(args):
        xc, ic, gc = args
        u = jnp.take(peer_u, ic, axis=0)
        a = jax.nn.gelu(jnp.einsum('cd,ced->ce', xc, u).astype(f32), approximate=False) * gc
        vsel = jnp.take(peer_v, ic, axis=0)
        return jnp.einsum('ce,ced->cd', a.astype(vsel.dtype), vsel)

    y = lax.map(expert_chunk, (t.reshape(nchunk, PEER_CHUNK, d),
                               idx.reshape(nchunk, PEER_CHUNK, -1),
                               gate.reshape(nchunk, PEER_CHUNK, -1)))
    return y.reshape(b, s, d)


def setup_inputs(seed: int = 0) -> dict:
    key = jax.random.key(seed)
    ks = jax.random.split(key, 20)
    nrm = lambda k, shape, std: jax.random.normal(k, shape, f32) * std
    gain = lambda k, shape: 1.0 + 0.02 * jax.random.normal(k, shape, f32)
    return {
        "x": nrm(ks[0], (BATCH, SEQ, D_MODEL), 1.0),
        "ln1_g": gain(ks[1], (DEPTH, D_MODEL)),
        "w_in": nrm(ks[2], (DEPTH, D_MODEL, D_IN), D_MODEL ** -0.5),
        "q_a_norm": gain(ks[3], (DEPTH, Q_LORA)),
        "kv_a_norm": gain(ks[4], (DEPTH, KV_LORA)),
        "w_uq": nrm(ks[5], (DEPTH, Q_LORA, N_HEADS_MLA * (QK_NOPE + QK_ROPE)), Q_LORA ** -0.5),
        "w_ukv": nrm(ks[6], (DEPTH, KV_LORA, N_HEADS_MLA * (QK_NOPE + V_HEAD)), KV_LORA ** -0.5),
        "out_norm_dil": gain(ks[7], (DEPTH, D_DIL)),
        "out_norm_mla": gain(ks[8], (DEPTH, D_MLA)),
        "w_o": nrm(ks[9], (DEPTH, D_MIX, D_MODEL), D_MIX ** -0.5),
        "ln2_g": gain(ks[10], (DEPTH, D_MODEL)),
        "peer_wq": nrm(ks[11], (DEPTH, D_MODEL, PEER_HEADS * PEER_QDIM), D_MODEL ** -0.5),
        "peer_sub_keys": nrm(ks[12], (DEPTH, PEER_HEADS, 2, PEER_NKEYS, PEER_QDIM // 2), (PEER_QDIM // 2) ** -0.5),
        "peer_u": nrm(ks[13], (DEPTH, PEER_N, D_MODEL), D_MODEL ** -0.5),
        "peer_v": nrm(ks[14], (DEPTH, PEER_N, D_MODEL), PEER_V_STD),
        "lnf_g": gain(ks[15], (D_MODEL,)),
    }


def reference(x, ln1_g, w_in, q_a_norm, kv_a_norm, w_uq, w_ukv, out_norm_dil, out_norm_mla, w_o,
              ln2_g, peer_wq, peer_sub_keys, peer_u, peer_v, lnf_g):
    b, s, _ = x.shape
    pos = jnp.arange(s)
    slopes = alibi_slopes(N_HEADS_DIL)
    h = x
    for l in range(DEPTH):
        xn = rmsnorm(h, ln1_g[l])
        z = xn @ w_in[l]
        q_a, k_a, v_a, c_q, c_kv, k_r = jnp.split(z, IN_SPLITS, axis=-1)
        to_heads = lambda t: t.reshape(b, s, N_HEADS_DIL, HEAD_DIM).transpose(0, 2, 1, 3)
        o_a = dilated_mixture_attention(to_heads(q_a), to_heads(k_a), to_heads(v_a), slopes)
        o_a = rmsnorm(o_a.transpose(0, 2, 1, 3).reshape(b, s, D_DIL), out_norm_dil[l])
        o_b = rmsnorm(mla_attention(c_q, c_kv, k_r, q_a_norm[l], kv_a_norm[l], w_uq[l], w_ukv[l], pos),
                      out_norm_mla[l])
        h = h + jnp.concatenate([o_a, o_b], axis=-1) @ w_o[l]
        h = h + peer_ffn(rmsnorm(h, ln2_g[l]), peer_wq[l], peer_sub_keys[l], peer_u[l], peer_v[l])
    return rmsnorm(h, lnf_g)
```

```python
import functools
import math

import jax
import jax.numpy as jnp
from jax import lax
from jax.experimental import pallas as pl
from jax.experimental.pallas import tpu as pltpu

F32 = jnp.float32
BF16 = jnp.bfloat16

EPS = 1e-6
NEG = -1e30
HEAD_DIM = 128
BLOCK = 128
DIL_PATTERNS = ((128, 1), (512, 4), (2048, 16))
N_HEADS = 8
QK_NOPE = 128
QK_ROPE = 64
ROPE_THETA = 10000.0
PEER_HEADS = 8
PEER_NKEYS = 128
PEER_TOPK = 16
PEER_E = PEER_HEADS * PEER_TOPK

LANES = 128
SUBLANES = 8
VMEM_CAP = 60000 * 1024


def _vmem_limit(nbytes):
    return int(min(VMEM_CAP, max(16 * 1024 * 1024, nbytes * 3 // 2)))


def _rms(x, g):
    return x * lax.rsqrt(jnp.mean(x * x, axis=-1, keepdims=True) + EPS) * g


def _norm_matmul_body(x_ref, g_ref, w_ref, *rest, emit_xn, slabs):
    if emit_xn:
        o_ref, xn_out_ref, xn_ref = rest
    else:
        o_ref, xn_ref = rest
    j = pl.program_id(1)

    @pl.when(j == 0)
    def _():
        xn = _rms(x_ref[...].astype(F32), g_ref[...])
        xn_ref[...] = xn.astype(BF16)
        if emit_xn:
            xn_out_ref[...] = xn

    res = jnp.dot(xn_ref[...], w_ref[...], preferred_element_type=F32)
    if slabs:
        for s in range(slabs):
            o_ref[s] = res[:, s * LANES:(s + 1) * LANES]
    else:
        o_ref[...] = res


def _norm_matmul(x, g, w, *, xcol=0, kdim=None, tm=512, tn=512, emit_xn=False, slab_out=False):
    T = x.shape[0]
    kdim = kdim or x.shape[1]
    N = w.shape[1]
    tm = min(tm, T)
    tn = min(tn, N)
    assert T % tm == 0 and N % tn == 0 and w.shape[0] == kdim
    slabs = tn // LANES if slab_out else 0
    if slab_out:
        out_shape = [jax.ShapeDtypeStruct((N // LANES, T, LANES), F32)]
        out_specs = [pl.BlockSpec((slabs, tm, LANES), lambda i, j: (j, i, 0))]
    else:
        out_shape = [jax.ShapeDtypeStruct((T, N), F32)]
        out_specs = [pl.BlockSpec((tm, tn), lambda i, j: (i, j))]
    if emit_xn:
        out_shape.append(jax.ShapeDtypeStruct((T, kdim), F32))
        out_specs.append(pl.BlockSpec((tm, kdim), lambda i, j: (i, 0)))
    est = 2 * (tm * kdim * 4 + kdim * tn * 2 + tm * tn * 4) + tm * kdim * 2
    if emit_xn:
        est += 2 * tm * kdim * 4
    outs = pl.pallas_call(
        functools.partial(_norm_matmul_body, emit_xn=emit_xn, slabs=slabs),
        grid=(T // tm, N // tn),
        in_specs=[
            pl.BlockSpec((tm, kdim), lambda i, j: (i, xcol)),
            pl.BlockSpec((1, kdim), lambda i, j: (0, 0)),
            pl.BlockSpec((kdim, tn), lambda i, j: (0, j)),
        ],
        out_specs=out_specs,
        out_shape=out_shape,
        scratch_shapes=[pltpu.VMEM((tm, kdim), BF16)],
        compiler_params=pltpu.CompilerParams(
            dimension_semantics=("arbitrary", "arbitrary"),
            vmem_limit_bytes=_vmem_limit(est)),
        name="norm_matmul",
    )(x, g.reshape(1, kdim).astype(F32), w)
    return outs if emit_xn else outs[0]


def _dilated_body(slopes_ref, q_ref, k_ref, v_ref, o_ref, o_scr, l_scr, *, seq, patterns, scale):
    h = pl.program_id(1)
    slope = slopes_ref[h]
    qi = lax.broadcasted_iota(jnp.int32, (BLOCK, 2 * BLOCK), 0)
    kj = lax.broadcasted_iota(jnp.int32, (BLOCK, 2 * BLOCK), 1)
    delta = qi + BLOCK - kj
    nt = (((1,), (1,)), ((), ()))

    for p, (window, d) in enumerate(patterns):
        steps = window // d
        nb = seq // d // BLOCK
        in_window = (delta >= 0) & (delta <= steps)
        bias = -slope * (delta * d).astype(F32)

        def block(t, carry, d=d, nb=nb, in_window=in_window, bias=bias, p=p):
            r = t // nb
            n = t % nb
            start = n * (BLOCK * d) + r
            pstart = jnp.maximum(n - 1, 0) * (BLOCK * d) + r
            rows = pl.ds(start, BLOCK, stride=d) if d > 1 else pl.ds(start, BLOCK)
            prow = pl.ds(pstart, BLOCK, stride=d) if d > 1 else pl.ds(pstart, BLOCK)
            q = q_ref[0, rows, :].astype(BF16)
            kk = jnp.concatenate([k_ref[0, prow, :], k_ref[0, rows, :]], axis=0).astype(BF16)
            vv = jnp.concatenate([v_ref[0, prow, :], v_ref[0, rows, :]], axis=0).astype(BF16)
            s = lax.dot_general(q, kk, nt, preferred_element_type=F32) * scale
            valid = in_window & ((kj >= BLOCK) | (n > 0))
            s = jnp.where(valid, s + bias, NEG)
            m = jnp.max(s, axis=-1, keepdims=True)
            e = jnp.exp(s - m)
            l = jnp.sum(e, axis=-1, keepdims=True)
            o = jnp.dot((e / l).astype(BF16), vv, preferred_element_type=F32)
            o_scr[p, rows, :] = o
            l_scr[p, rows, :] = jnp.broadcast_to(m + jnp.log(l), (BLOCK, HEAD_DIM))
            return carry

        lax.fori_loop(0, d * nb, block, 0)

    def mix(c, carry):
        rows = pl.ds(pl.multiple_of(c * BLOCK, BLOCK), BLOCK)
        ls = [l_scr[p, rows, :] for p in range(len(patterns))]
        m = functools.reduce(jnp.maximum, ls)
        es = [jnp.exp(l - m) for l in ls]
        den = functools.reduce(jnp.add, es)
        num = functools.reduce(jnp.add, [e * o_scr[p, rows, :] for p, e in enumerate(es)])
        o_ref[0, rows, :] = num / den
        return carry

    lax.fori_loop(0, seq // BLOCK, mix, 0)


def _dilated_attention(z3, slopes, *, n_heads, patterns=DIL_PATTERNS):
    B, S, _ = z3.shape
    for _, d in patterns:
        assert S % (d * BLOCK) == 0
    blk = (1, S, HEAD_DIM)
    est = 2 * 4 * S * HEAD_DIM * 4 + 2 * len(patterns) * S * HEAD_DIM * 4
    return pl.pallas_call(
        functools.partial(_dilated_body, seq=S, patterns=patterns, scale=HEAD_DIM ** -0.5),
        grid=(B, n_heads),
        in_specs=[
            pl.BlockSpec(memory_space=pltpu.SMEM),
            pl.BlockSpec(blk, lambda b, h: (b, 0, h)),
            pl.BlockSpec(blk, lambda b, h: (b, 0, n_heads + h)),
            pl.BlockSpec(blk, lambda b, h: (b, 0, 2 * n_heads + h)),
        ],
        out_specs=pl.BlockSpec(blk, lambda b, h: (b, 0, h)),
        out_shape=jax.ShapeDtypeStruct((B, S, n_heads * HEAD_DIM), F32),
        scratch_shapes=[pltpu.VMEM((len(patterns), S, HEAD_DIM), F32),
                        pltpu.VMEM((len(patterns), S, HEAD_DIM), F32)],
        compiler_params=pltpu.CompilerParams(
            dimension_semantics=("arbitrary", "arbitrary"),
            vmem_limit_bytes=_vmem_limit(est)),
        name="dilated_attention",
    )(slopes, z3, z3, z3)


def _rope(x, cos, sin_signed):
    lane = lax.broadcasted_iota(jnp.int32, x.shape, 1)
    half = QK_ROPE // 2
    swapped = jnp.where(lane < half, pltpu.roll(x, LANES - half, 1), pltpu.roll(x, half, 1))
    return x * cos + swapped * sin_signed


def _mla_body(qn_ref, qp_ref, kn_ref, v_ref, kr_ref, cosq_ref, sinq_ref, cosk_ref, sink_ref,
              o_ref, kcat_scr, v_scr, *, tq, tk, scale):
    i = pl.program_id(2)

    @pl.when(i == 0)
    def _():
        kcat_scr[:, :LANES] = kn_ref[0].astype(BF16)
        kcat_scr[:, LANES:] = _rope(kr_ref[0], cosk_ref[...], sink_ref[...]).astype(BF16)
        v_scr[...] = v_ref[0].astype(BF16)

    qpe = _rope(qp_ref[0], cosq_ref[...], sinq_ref[...])
    q = jnp.concatenate([qn_ref[0], qpe], axis=1).astype(BF16)
    qpos = i * tq + lax.broadcasted_iota(jnp.int32, (tq, tk), 0)
    kcol = lax.broadcasted_iota(jnp.int32, (tq, tk), 1)
    nt = (((1,), (1,)), ((), ()))

    def kv_block(j, carry):
        m, l, acc = carry
        rows = pl.ds(pl.multiple_of(j * tk, tk), tk)
        s = lax.dot_general(q, kcat_scr[rows, :], nt, preferred_element_type=F32) * scale
        s = jnp.where(j * tk + kcol <= qpos, s, NEG)
        m_new = jnp.maximum(m, jnp.max(s, axis=-1, keepdims=True))
        alpha = jnp.exp(m - m_new)
        e = jnp.exp(s - m_new)
        l = alpha * l + jnp.sum(e, axis=-1, keepdims=True)
        acc = alpha * acc + jnp.dot(e.astype(BF16), v_scr[rows, :], preferred_element_type=F32)
        return m_new, l, acc

    n_kv = (i * tq + tq + tk - 1) // tk
    init = (jnp.full((tq, 1), NEG, F32), jnp.zeros((tq, 1), F32), jnp.zeros((tq, HEAD_DIM), F32))
    _, l, acc = lax.fori_loop(0, n_kv, kv_block, init)
    o_ref[0] = acc / l


def _mla_attention(q3, kv3, z3, kr_col, cos, sin_signed, *, n_heads, tq=128, tk=256):
    B, S, _ = q3.shape
    tq, tk = min(tq, S), min(tk, S)
    assert S % tq == 0 and S % tk == 0
    est = 2 * (2 * tq + 3 * S + 2 * tq + 2 * S + tq) * LANES * 4 + S * 3 * LANES * 2
    return pl.pallas_call(
        functools.partial(_mla_body, tq=tq, tk=tk, scale=(QK_NOPE + QK_ROPE) ** -0.5),
        grid=(B, n_heads, S // tq),
        in_specs=[
            pl.BlockSpec((1, tq, LANES), lambda b, h, i: (b, i, h)),
            pl.BlockSpec((1, tq, LANES), lambda b, h, i: (b, i, n_heads + h)),
            pl.BlockSpec((1, S, LANES), lambda b, h, i: (b, 0, h)),
            pl.BlockSpec((1, S, LANES), lambda b, h, i: (b, 0, n_heads + h)),
            pl.BlockSpec((1, S, LANES), lambda b, h, i: (b, 0, kr_col)),
            pl.BlockSpec((tq, LANES), lambda b, h, i: (i, 0)),
            pl.BlockSpec((tq, LANES), lambda b, h, i: (i, 0)),
            pl.BlockSpec((S, LANES), lambda b, h, i: (0, 0)),
            pl.BlockSpec((S, LANES), lambda b, h, i: (0, 0)),
        ],
        out_specs=pl.BlockSpec((1, tq, LANES), lambda b, h, i: (b, i, h)),
        out_shape=jax.ShapeDtypeStruct((B, S, n_heads * HEAD_DIM), F32),
        scratch_shapes=[pltpu.VMEM((S, 2 * LANES), BF16), pltpu.VMEM((S, LANES), BF16)],
        compiler_params=pltpu.CompilerParams(
            dimension_semantics=("arbitrary", "arbitrary", "arbitrary"),
            vmem_limit_bytes=_vmem_limit(est)),
        name="mla_attention",
    )(q3, q3, kv3, kv3, z3, cos, sin_signed, cos, sin_signed)


def _outproj_body(oa_ref, ob_ref, ga_ref, gb_ref, w_ref, x_ref, o_ref, xn_ref, *, da):
    j = pl.program_id(1)

    @pl.when(j == 0)
    def _():
        xn_ref[:, :da] = _rms(oa_ref[...], ga_ref[...]).astype(BF16)
        xn_ref[:, da:] = _rms(ob_ref[...], gb_ref[...]).astype(BF16)

    o_ref[...] = x_ref[...] + jnp.dot(xn_ref[...], w_ref[...], preferred_element_type=F32)


def _outproj(oa, ob, ga, gb, w, x, *, tm=512, tn=512):
    T, da = oa.shape
    db = ob.shape[1]
    N = w.shape[1]
    tm, tn = min(tm, T), min(tn, N)
    assert T % tm == 0 and N % tn == 0 and w.shape[0] == da + db
    est = 2 * (tm * (da + db) * 4 + (da + db) * tn * 2 + 2 * tm * tn * 4) + tm * (da + db) * 2
    return pl.pallas_call(
        functools.partial(_outproj_body, da=da),
        grid=(T // tm, N // tn),
        in_specs=[
            pl.BlockSpec((tm, da), lambda i, j: (i, 0)),
            pl.BlockSpec((tm, db), lambda i, j: (i, 0)),
            pl.BlockSpec((1, da), lambda i, j: (0, 0)),
            pl.BlockSpec((1, db), lambda i, j: (0, 0)),
            pl.BlockSpec((da + db, tn), lambda i, j: (0, j)),
            pl.BlockSpec((tm, tn), lambda i, j: (i, j)),
        ],
        out_specs=pl.BlockSpec((tm, tn), lambda i, j: (i, j)),
        out_shape=jax.ShapeDtypeStruct((T, N), F32),
        scratch_shapes=[pltpu.VMEM((tm, da + db), BF16)],
        compiler_params=pltpu.CompilerParams(
            dimension_semantics=("arbitrary", "arbitrary"),
            vmem_limit_bytes=_vmem_limit(est)),
        name="out_projection",
    )(oa, ob, ga.reshape(1, da), gb.reshape(1, db), w, x)


def _topk_body(q_ref, keys_ref, idx_ref, gate_ref, sv_ref, si_ref, cs_ref, ci_ref, bs_ref, be_ref,
               *, tt, nkeys, topk):
    lowest = float(jnp.finfo(jnp.float32).min)
    iota_n = lax.broadcasted_iota(jnp.int32, (nkeys, tt), 0)
    nt = (((1,), (1,)), ((), ()))

    for p in range(2):
        s = lax.dot_general(keys_ref[p], q_ref[p].astype(BF16), nt, preferred_element_type=F32)

        def pick(k, s, p=p):
            m = jnp.max(s, axis=0, keepdims=True)
            ix = jnp.min(jnp.where(s == m, iota_n, nkeys), axis=0, keepdims=True)
            sv_ref[p, pl.ds(k, 1), :] = m
            si_ref[p, pl.ds(k, 1), :] = ix
            return jnp.where(iota_n == ix, lowest, s)

        lax.fori_loop(0, topk, pick, s)

    for a in range(topk):
        cs_ref[a * topk:(a + 1) * topk, :] = sv_ref[0, a:a + 1, :] + sv_ref[1]
        ci_ref[a * topk:(a + 1) * topk, :] = si_ref[0, a:a + 1, :] * nkeys + si_ref[1]

    ncand = topk * topk
    iota_c = lax.broadcasted_iota(jnp.int32, (ncand, tt), 0)

    def pick2(k, c):
        m = jnp.max(c, axis=0, keepdims=True)
        pos = jnp.min(jnp.where(c == m, iota_c, ncand), axis=0, keepdims=True)
        hit = iota_c == pos
        bs_ref[pl.ds(k, 1), :] = m
        be_ref[pl.ds(k, 1), :] = jnp.max(jnp.where(hit, ci_ref[...], -1), axis=0, keepdims=True)
        return jnp.where(hit, lowest, c)

    lax.fori_loop(0, topk, pick2, cs_ref[...])

    b = bs_ref[...]
    e = jnp.exp(b - jnp.max(b, axis=0, keepdims=True))
    gate_ref[...] = e / jnp.sum(e, axis=0, keepdims=True)
    idx_ref[...] = be_ref[...]


def _peer_topk(q_slabs, keys, *, tt=128):
    hp, T, c = q_slabs.shape
    heads = hp // 2
    nkeys = keys.shape[1]
    tt = min(tt, T)
    assert T % tt == 0
    topk = PEER_TOPK
    return pl.pallas_call(
        functools.partial(_topk_body, tt=tt, nkeys=nkeys, topk=topk),
        grid=(T // tt, heads),
        in_specs=[
            pl.BlockSpec((2, tt, c), lambda i, h: (h, i, 0)),
            pl.BlockSpec((2, nkeys, c), lambda i, h: (h, 0, 0)),
        ],
        out_specs=[pl.BlockSpec((topk, tt), lambda i, h: (h, i)),
                   pl.BlockSpec((topk, tt), lambda i, h: (h, i))],
        out_shape=[jax.ShapeDtypeStruct((heads * topk, T), jnp.int32),
                   jax.ShapeDtypeStruct((heads * topk, T), F32)],
        scratch_shapes=[pltpu.VMEM((2, topk, tt), F32), pltpu.VMEM((2, topk, tt), jnp.int32),
                        pltpu.VMEM((topk * topk, tt), F32), pltpu.VMEM((topk * topk, tt), jnp.int32),
                        pltpu.VMEM((topk, tt), F32), pltpu.VMEM((topk, tt), jnp.int32)],
        compiler_params=pltpu.CompilerParams(dimension_semantics=("arbitrary", "arbitrary")),
        name="peer_topk",
    )(q_slabs, keys)


GROUP = SUBLANES


def _gelu_exact(x):
    return 0.5 * x * (1.0 + lax.erf(x * (2.0 ** -0.5)))


def _peer_ffn_body(idx0_ref, idxn_ref, gate_ref, x_ref, h_ref, gf_ref, tab_ref, o_ref,
                   buf, a_scr, sem, *, ctok, nexp):
    i = pl.program_id(0)
    n = pl.num_programs(0)
    slot = lax.rem(i, 2)
    nslot = 1 - slot

    def start_token(idx_ref, c, dst_slot):
        for e in range(nexp):
            pltpu.make_async_copy(tab_ref.at[idx_ref[c, e]], buf.at[dst_slot, c * nexp + e],
                                  sem.at[dst_slot]).start()

    def wait_slot(s):
        pltpu.make_async_copy(buf.at[s], buf.at[s], sem.at[s]).wait()

    @pl.when(i == 0)
    def _():
        def first(c, carry):
            start_token(idx0_ref, c, 0)
            return carry
        lax.fori_loop(0, ctok, first, 0)

    wait_slot(slot)
    gate_t = jnp.transpose(gate_ref[...])
    sub = lax.broadcasted_iota(jnp.int32, (SUBLANES, LANES), 0)
    hi_mask = jnp.uint32(0xFFFF0000)

    def token(c, carry):
        start_token(idxn_ref, c, nslot)
        x0 = x_ref[c, :SUBLANES, :]
        x1 = x_ref[c, SUBLANES:, :]
        lane_c = lax.broadcasted_iota(jnp.int32, gate_t.shape, 1)
        gcol = jnp.sum(jnp.where(lane_c == c, gate_t, 0.0), axis=1, keepdims=True)
        base = c * nexp
        for g in range(nexp // GROUP):
            r = jnp.zeros((SUBLANES, LANES), F32)
            for k in range(GROUP):
                w = buf[slot, base + g * GROUP + k]
                u0 = pltpu.bitcast(w[:SUBLANES] << 16, F32)
                u1 = pltpu.bitcast(w[SUBLANES:] << 16, F32)
                q = u0 * x0 + u1 * x1
                r = jnp.where(sub == k, jnp.sum(q, axis=0, keepdims=True), r)
            dots = jnp.sum(r, axis=1, keepdims=True)
            act = _gelu_exact(dots) * gcol[g * GROUP:(g + 1) * GROUP]
            a_scr[g * GROUP:(g + 1) * GROUP, :] = jnp.broadcast_to(act, (GROUP, LANES))
        y0 = jnp.zeros((SUBLANES, LANES), F32)
        y1 = jnp.zeros((SUBLANES, LANES), F32)
        for e in range(nexp):
            w = buf[slot, base + e]
            a = jnp.broadcast_to(a_scr[e:e + 1, :], (SUBLANES, LANES))
            y0 = y0 + a * pltpu.bitcast(w[:SUBLANES] & hi_mask, F32)
            y1 = y1 + a * pltpu.bitcast(w[SUBLANES:] & hi_mask, F32)
        z0 = h_ref[c, :SUBLANES, :] + y0
        z1 = h_ref[c, SUBLANES:, :] + y1
        ss = jnp.sum(z0 * z0 + z1 * z1, axis=1, keepdims=True)
        ms = jnp.sum(ss, axis=0, keepdims=True) * (1.0 / (2 * SUBLANES * LANES))
        inv = lax.rsqrt(ms + EPS)
        o_ref[c, :SUBLANES, :] = z0 * inv * gf_ref[:SUBLANES, :]
        o_ref[c, SUBLANES:, :] = z1 * inv * gf_ref[SUBLANES:, :]
        return carry

    lax.fori_loop(0, ctok, token, 0)

    @pl.when(i == n - 1)
    def _():
        wait_slot(nslot)


def _peer_ffn(idx, gate, xn3, h3, gf, table, *, ctok=8):
    T, nexp = idx.shape
    rows = xn3.shape[1]
    assert T % ctok == 0 and rows == 2 * SUBLANES
    n = T // ctok
    last = n - 1
    est = 2 * ctok * nexp * rows * LANES * 4 + 6 * ctok * rows * LANES * 4 + nexp * LANES * 4
    return pl.pallas_call(
        functools.partial(_peer_ffn_body, ctok=ctok, nexp=nexp),
        grid=(n,),
        in_specs=[
            pl.BlockSpec((ctok, nexp), lambda i: (0, 0), memory_space=pltpu.SMEM),
            pl.BlockSpec((ctok, nexp), lambda i: (jnp.minimum(i + 1, last), 0), memory_space=pltpu.SMEM),
            pl.BlockSpec((ctok, nexp), lambda i: (i, 0)),
            pl.BlockSpec((ctok, rows, LANES), lambda i: (i, 0, 0)),
            pl.BlockSpec((ctok, rows, LANES), lambda i: (i, 0, 0)),
            pl.BlockSpec((rows, LANES), lambda i: (0, 0)),
            pl.BlockSpec(memory_space=pl.ANY),
        ],
        out_specs=pl.BlockSpec((ctok, rows, LANES), lambda i: (i, 0, 0)),
        out_shape=jax.ShapeDtypeStruct((T, rows, LANES), F32),
        scratch_shapes=[pltpu.VMEM((2, ctok * nexp, rows, LANES), jnp.uint32),
                        pltpu.VMEM((nexp, LANES), F32),
                        pltpu.SemaphoreType.DMA((2,))],
        compiler_params=pltpu.CompilerParams(
            dimension_semantics=("arbitrary",),
            vmem_limit_bytes=_vmem_limit(est)),
        name="peer_ffn",
    )(idx, idx, gate, xn3, h3, gf, table)


def _pack_expert_table(u, v):
    ub = lax.bitcast_convert_type(u.astype(BF16), jnp.uint16).astype(jnp.uint32)
    vb = lax.bitcast_convert_type(v.astype(BF16), jnp.uint16).astype(jnp.uint32)
    n, d = u.shape
    return (ub | (vb << 16)).reshape(n, d // LANES, LANES)


def _rope_tables(seq):
    half = QK_ROPE // 2
    freqs = ROPE_THETA ** (-jnp.arange(half, dtype=F32) / half)
    ang = jnp.arange(seq, dtype=F32)[:, None] * freqs[None, :]
    cos, sin = jnp.cos(ang), jnp.sin(ang)
    zeros = jnp.zeros((seq, LANES - QK_ROPE), F32)
    return (jnp.concatenate([cos, cos, zeros], axis=1),
            jnp.concatenate([-sin, sin, zeros], axis=1))


def _layer(h, ln1_g, w_in, q_a_norm, kv_a_norm, w_uq, w_ukv, out_norm_dil, out_norm_mla, w_o,
           ln2_g, peer_wq, peer_sub_keys, peer_u, peer_v, final_g):
    B, S, D = h.shape
    T = B * S
    H = N_HEADS
    d_dil = H * HEAD_DIM
    q_lora = q_a_norm.shape[0]
    kv_lora = kv_a_norm.shape[0]
    x2 = h.reshape(T, D)

    d_in = w_in.shape[1]
    d_in_pad = -(-d_in // LANES) * LANES
    w_in_p = jnp.pad(w_in, ((0, 0), (0, d_in_pad - d_in))).astype(BF16)
    z = _norm_matmul(x2, ln1_g, w_in_p, tn=d_in_pad // 3 if d_in_pad % (3 * LANES) == 0 else LANES)
    z3 = z.reshape(B, S, d_in_pad)

    slopes = 2.0 ** (-8.0 * jnp.arange(1, H + 1, dtype=F32) / H)
    o_a = _dilated_attention(z3, slopes, n_heads=H)

    wq = w_uq.reshape(q_lora, H, QK_NOPE + QK_ROPE)
    wq_pe = jnp.pad(wq[:, :, QK_NOPE:], ((0, 0), (0, 0), (0, LANES - QK_ROPE)))
    wq_p = jnp.concatenate([wq[:, :, :QK_NOPE].reshape(q_lora, H * QK_NOPE),
                            wq_pe.reshape(q_lora, H * LANES)], axis=1).astype(BF16)
    wkv = w_ukv.reshape(kv_lora, H, QK_NOPE + HEAD_DIM)
    wkv_p = jnp.concatenate([wkv[:, :, :QK_NOPE].reshape(kv_lora, H * QK_NOPE),
                             wkv[:, :, QK_NOPE:].reshape(kv_lora, H * HEAD_DIM)], axis=1).astype(BF16)
    assert q_lora == kv_lora and (3 * d_dil) % q_lora == 0
    q_mla = _norm_matmul(z, q_a_norm, wq_p, xcol=3 * d_dil // q_lora, kdim=q_lora)
    kv_mla = _norm_matmul(z, kv_a_norm, wkv_p, xcol=3 * d_dil // q_lora + 1, kdim=kv_lora)
    cos, sin_signed = _rope_tables(S)
    o_b = _mla_attention(q_mla.reshape(B, S, -1), kv_mla.reshape(B, S, -1), z3,
                         (3 * d_dil + q_lora + kv_lora) // LANES, cos, sin_signed, n_heads=H)

    h1 = _outproj(o_a.reshape(T, d_dil), o_b.reshape(T, -1), out_norm_dil, out_norm_mla,
                  w_o.astype(BF16), x2)

    q_slabs, xn2 = _norm_matmul(h1, ln2_g, peer_wq.astype(BF16), emit_xn=True, slab_out=True)
    keys = peer_sub_keys.reshape(PEER_HEADS * 2, PEER_NKEYS, -1).astype(BF16)
    idx_t, gate_t = _peer_topk(q_slabs, keys)
    table = _pack_expert_table(peer_u, peer_v)
    rows = D // LANES
    out = _peer_ffn(idx_t.T, gate_t.T, xn2.reshape(T, rows, LANES), h1.reshape(T, rows, LANES),
                    final_g.reshape(rows, LANES), table)
    return out.reshape(B, S, D)


def kernel(x, ln1_g, w_in, q_a_norm, kv_a_norm, w_uq, w_ukv, out_norm_dil, out_norm_mla, w_o,
           ln2_g, peer_wq, peer_sub_keys, peer_u, peer_v, lnf_g):
    assert ln1_g.shape[0] == 1, "single-layer trunk"
    return _layer(x, ln1_g[0], w_in[0], q_a_norm[0], kv_a_norm[0], w_uq[0], w_ukv[0],
                  out_norm_dil[0], out_norm_mla[0], w_o[0], ln2_g[0], peer_wq[0],
                  peer_sub_keys[0], peer_u[0], peer_v[0], lnf_g)
```

```python
import functools
import math

import jax
import jax.numpy as jnp
from jax import lax
from jax.experimental import pallas as pl
from jax.experimental.pallas import tpu as pltpu

F32 = jnp.float32
BF16 = jnp.bfloat16

EPS = 1e-6
NEG = -1e30
HEAD_DIM = 128
BLOCK = 128
DIL_PATTERNS = ((128, 1), (512, 4), (2048, 16))
N_HEADS = 8
QK_NOPE = 128
QK_ROPE = 64
ROPE_THETA = 10000.0
PEER_HEADS = 8
PEER_NKEYS = 128
PEER_TOPK = 16
PEER_E = PEER_HEADS * PEER_TOPK

LANES = 128
SUBLANES = 8
VMEM_CAP = 60000 * 1024


def _vmem_limit(nbytes):
    return int(min(VMEM_CAP, max(16 * 1024 * 1024, nbytes * 3 // 2)))


def _rms(x, g):
    return x * lax.rsqrt(jnp.mean(x * x, axis=-1, keepdims=True) + EPS) * g


def _norm_matmul_body(x_ref, g_ref, w_ref, *rest, emit_xn, slabs):
    if emit_xn:
        o_ref, xn_out_ref, xn_ref = rest
    else:
        o_ref, xn_ref = rest
    j = pl.program_id(1)

    @pl.when(j == 0)
    def _():
        xn = _rms(x_ref[...].astype(F32), g_ref[...])
        xn_ref[...] = xn.astype(BF16)
        if emit_xn:
            xn_out_ref[...] = xn

    res = jnp.dot(xn_ref[...], w_ref[...], preferred_element_type=F32)
    if slabs:
        for s in range(slabs):
            o_ref[s] = res[:, s * LANES:(s + 1) * LANES]
    else:
        o_ref[...] = res


def _norm_matmul(x, g, w, *, xcol=0, kdim=None, tm=512, tn=512, emit_xn=False, slab_out=False):
    T = x.shape[0]
    kdim = kdim or x.shape[1]
    N = w.shape[1]
    tm = min(tm, T)
    tn = min(tn, N)
    assert T % tm == 0 and N % tn == 0 and w.shape[0] == kdim
    slabs = tn // LANES if slab_out else 0
    if slab_out:
        out_shape = [jax.ShapeDtypeStruct((N // LANES, T, LANES), F32)]
        out_specs = [pl.BlockSpec((slabs, tm, LANES), lambda i, j: (j, i, 0))]
    else:
        out_shape = [jax.ShapeDtypeStruct((T, N), F32)]
        out_specs = [pl.BlockSpec((tm, tn), lambda i, j: (i, j))]
    if emit_xn:
        out_shape.append(jax.ShapeDtypeStruct((T, kdim), F32))
        out_specs.append(pl.BlockSpec((tm, kdim), lambda i, j: (i, 0)))
    est = 2 * (tm * kdim * 4 + kdim * tn * 2 + tm * tn * 4) + tm * kdim * 2
    if emit_xn:
        est += 2 * tm * kdim * 4
    outs = pl.pallas_call(
        functools.partial(_norm_matmul_body, emit_xn=emit_xn, slabs=slabs),
        grid=(T // tm, N // tn),
        in_specs=[
            pl.BlockSpec((tm, kdim), lambda i, j: (i, xcol)),
            pl.BlockSpec((1, kdim), lambda i, j: (0, 0)),
            pl.BlockSpec((kdim, tn), lambda i, j: (0, j)),
        ],
        out_specs=out_specs,
        out_shape=out_shape,
        scratch_shapes=[pltpu.VMEM((tm, kdim), BF16)],
        compiler_params=pltpu.CompilerParams(
            dimension_semantics=("arbitrary", "arbitrary"),
            vmem_limit_bytes=_vmem_limit(est)),
        name="norm_matmul",
    )(x, g.reshape(1, kdim).astype(F32), w)
    return outs if emit_xn else outs[0]


def _dilated_body(slopes_ref, q_ref, k_ref, v_ref, o_ref, o_scr, l_scr, *, seq, patterns, scale):
    h = pl.program_id(1)
    slope = slopes_ref[h]
    qi = lax.broadcasted_iota(jnp.int32, (BLOCK, 2 * BLOCK), 0)
    kj = lax.broadcasted_iota(jnp.int32, (BLOCK, 2 * BLOCK), 1)
    delta = qi + BLOCK - kj
    nt = (((1,), (1,)), ((), ()))

    for p, (window, d) in enumerate(patterns):
        steps = window // d
        nb = seq // d // BLOCK
        in_window = (delta >= 0) & (delta <= steps)
        bias = -slope * (delta * d).astype(F32)

        def block(t, carry, d=d, nb=nb, in_window=in_window, bias=bias, p=p):
            r = t // nb
            n = t % nb
            start = n * (BLOCK * d) + r
            pstart = jnp.maximum(n - 1, 0) * (BLOCK * d) + r
            rows = pl.ds(start, BLOCK, stride=d) if d > 1 else pl.ds(start, BLOCK)
            prow = pl.ds(pstart, BLOCK, stride=d) if d > 1 else pl.ds(pstart, BLOCK)
            q = q_ref[0, rows, :].astype(BF16)
            kk = jnp.concatenate([k_ref[0, prow, :], k_ref[0, rows, :]], axis=0).astype(BF16)
            vv = jnp.concatenate([v_ref[0, prow, :], v_ref[0, rows, :]], axis=0).astype(BF16)
            s = lax.dot_general(q, kk, nt, preferred_element_type=F32) * scale
            valid = in_window & ((kj >= BLOCK) | (n > 0))
            s = jnp.where(valid, s + bias, NEG)
            m = jnp.max(s, axis=-1, keepdims=True)
            e = jnp.exp(s - m)
            l = jnp.sum(e, axis=-1, keepdims=True)
            o = jnp.dot((e / l).astype(BF16), vv, preferred_element_type=F32)
            o_scr[p, rows, :] = o
            l_scr[p, rows, :] = jnp.broadcast_to(m + jnp.log(l), (BLOCK, HEAD_DIM))
            return carry

        lax.fori_loop(0, d * nb, block, 0)

    def mix(c, carry):
        rows = pl.ds(pl.multiple_of(c * BLOCK, BLOCK), BLOCK)
        ls = [l_scr[p, rows, :] for p in range(len(patterns))]
        m = functools.reduce(jnp.maximum, ls)
        es = [jnp.exp(l - m) for l in ls]
        den = functools.reduce(jnp.add, es)
        num = functools.reduce(jnp.add, [e * o_scr[p, rows, :] for p, e in enumerate(es)])
        o_ref[0, rows, :] = num / den
        return carry

    lax.fori_loop(0, seq // BLOCK, mix, 0)


def _dilated_attention(z3, slopes, *, n_heads, patterns=DIL_PATTERNS):
    B, S, _ = z3.shape
    for _, d in patterns:
        assert S % (d * BLOCK) == 0
    blk = (1, S, HEAD_DIM)
    est = 2 * 4 * S * HEAD_DIM * 4 + 2 * len(patterns) * S * HEAD_DIM * 4
    return pl.pallas_call(
        functools.partial(_dilated_body, seq=S, patterns=patterns, scale=HEAD_DIM ** -0.5),
        grid=(B, n_heads),
        in_specs=[
            pl.BlockSpec(memory_space=pltpu.SMEM),
            pl.BlockSpec(blk, lambda b, h: (b, 0, h)),
            pl.BlockSpec(blk, lambda b, h: (b, 0, n_heads + h)),
            pl.BlockSpec(blk, lambda b, h: (b, 0, 2 * n_heads + h)),
        ],
        out_specs=pl.BlockSpec(blk, lambda b, h: (b, 0, h)),
        out_shape=jax.ShapeDtypeStruct((B, S, n_heads * HEAD_DIM), F32),
        scratch_shapes=[pltpu.VMEM((len(patterns), S, HEAD_DIM), F32),
                        pltpu.VMEM((len(patterns), S, HEAD_DIM), F32)],
        compiler_params=pltpu.CompilerParams(
            dimension_semantics=("arbitrary", "arbitrary"),
            vmem_limit_bytes=_vmem_limit(est)),
        name="dilated_attention",
    )(slopes, z3, z3, z3)


def _rope(x, cos, sin_signed):
    lane = lax.broadcasted_iota(jnp.int32, x.shape, 1)
    half = QK_ROPE // 2
    swapped = jnp.where(lane < half, pltpu.roll(x, LANES - half, 1), pltpu.roll(x, half, 1))
    return x * cos + swapped * sin_signed


def _mla_body(qn_ref, qp_ref, kn_ref, v_ref, kr_ref, cosq_ref, sinq_ref, cosk_ref, sink_ref,
              o_ref, kcat_scr, v_scr, *, tq, tk, scale):
    i = pl.program_id(2)

    @pl.when(i == 0)
    def _():
        kcat_scr[:, :LANES] = kn_ref[0].astype(BF16)
        kcat_scr[:, LANES:] = _rope(kr_ref[0], cosk_ref[...], sink_ref[...]).astype(BF16)
        v_scr[...] = v_ref[0].astype(BF16)

    qpe = _rope(qp_ref[0], cosq_ref[...], sinq_ref[...])
    q = jnp.concatenate([qn_ref[0], qpe], axis=1).astype(BF16)
    qpos = i * tq + lax.broadcasted_iota(jnp.int32, (tq, tk), 0)
    kcol = lax.broadcasted_iota(jnp.int32, (tq, tk), 1)
    nt = (((1,), (1,)), ((), ()))

    def kv_block(j, carry):
        m, l, acc = carry
        rows = pl.ds(pl.multiple_of(j * tk, tk), tk)
        s = lax.dot_general(q, kcat_scr[rows, :], nt, preferred_element_type=F32) * scale
        s = jnp.where(j * tk + kcol <= qpos, s, NEG)
        m_new = jnp.maximum(m, jnp.max(s, axis=-1, keepdims=True))
        alpha = jnp.exp(m - m_new)
        e = jnp.exp(s - m_new)
        l = alpha * l + jnp.sum(e, axis=-1, keepdims=True)
        acc = alpha * acc + jnp.dot(e.astype(BF16), v_scr[rows, :], preferred_element_type=F32)
        return m_new, l, acc

    n_kv = (i * tq + tq + tk - 1) // tk
    init = (jnp.full((tq, 1), NEG, F32), jnp.zeros((tq, 1), F32), jnp.zeros((tq, HEAD_DIM), F32))
    _, l, acc = lax.fori_loop(0, n_kv, kv_block, init)
    o_ref[0] = acc / l


def _mla_attention(q3, kv3, z3, kr_col, cos, sin_signed, *, n_heads, tq=128, tk=256):
    B, S, _ = q3.shape
    tq, tk = min(tq, S), min(tk, S)
    assert S % tq == 0 and S % tk == 0
    est = 2 * (2 * tq + 3 * S + 2 * tq + 2 * S + tq) * LANES * 4 + S * 3 * LANES * 2
    return pl.pallas_call(
        functools.partial(_mla_body, tq=tq, tk=tk, scale=(QK_NOPE + QK_ROPE) ** -0.5),
        grid=(B, n_heads, S // tq),
        in_specs=[
            pl.BlockSpec((1, tq, LANES), lambda b, h, i: (b, i, h)),
            pl.BlockSpec((1, tq, LANES), lambda b, h, i: (b, i, n_heads + h)),
            pl.BlockSpec((1, S, LANES), lambda b, h, i: (b, 0, h)),
            pl.BlockSpec((1, S, LANES), lambda b, h, i: (b, 0, n_heads + h)),
            pl.BlockSpec((1, S, LANES), lambda b, h, i: (b, 0, kr_col)),
            pl.BlockSpec((tq, LANES), lambda b, h, i: (i, 0)),
            pl.BlockSpec((tq, LANES), lambda b, h, i: (i, 0)),
            pl.BlockSpec((S, LANES), lambda b, h, i: (0, 0)),
            pl.BlockSpec((S, LANES), lambda b, h, i: (0, 0)),
        ],
        out_specs=pl.BlockSpec((1, tq, LANES), lambda b, h, i: (b, i, h)),
        out_shape=jax.ShapeDtypeStruct((B, S, n_heads * HEAD_DIM), F32),
        scratch_shapes=[pltpu.VMEM((S, 2 * LANES), BF16), pltpu.VMEM((S, LANES), BF16)],
        compiler_params=pltpu.CompilerParams(
            dimension_semantics=("arbitrary", "arbitrary", "arbitrary"),
            vmem_limit_bytes=_vmem_limit(est)),
        name="mla_attention",
    )(q3, q3, kv3, kv3, z3, cos, sin_signed, cos, sin_signed)


def _outproj_body(oa_ref, ob_ref, ga_ref, gb_ref, w_ref, x_ref, o_ref, xn_ref, *, da):
    j = pl.program_id(1)

    @pl.when(j == 0)
    def _():
        xn_ref[:, :da] = _rms(oa_ref[...], ga_ref[...]).astype(BF16)
        xn_ref[:, da:] = _rms(ob_ref[...], gb_ref[...]).astype(BF16)

    o_ref[...] = x_ref[...] + jnp.dot(xn_ref[...], w_ref[...], preferred_element_type=F32)


def _outproj(oa, ob, ga, gb, w, x, *, tm=512, tn=512):
    T, da = oa.shape
    db = ob.shape[1]
    N = w.shape[1]
    tm, tn = min(tm, T), min(tn, N)
    assert T % tm == 0 and N % tn == 0 and w.shape[0] == da + db
    est = 2 * (tm * (da + db) * 4 + (da + db) * tn * 2 + 2 * tm * tn * 4) + tm * (da + db) * 2
    return pl.pallas_call(
        functools.partial(_outproj_body, da=da),
        grid=(T // tm, N // tn),
        in_specs=[
            pl.BlockSpec((tm, da), lambda i, j: (i, 0)),
            pl.BlockSpec((tm, db), lambda i, j: (i, 0)),
            pl.BlockSpec((1, da), lambda i, j: (0, 0)),
            pl.BlockSpec((1, db), lambda i, j: (0, 0)),
            pl.BlockSpec((da + db, tn), lambda i, j: (0, j)),
            pl.BlockSpec((tm, tn), lambda i, j: (i, j)),
        ],
        out_specs=pl.BlockSpec((tm, tn), lambda i, j: (i, j)),
        out_shape=jax.ShapeDtypeStruct((T, N), F32),
        scratch_shapes=[pltpu.VMEM((tm, da + db), BF16)],
        compiler_params=pltpu.CompilerParams(
            dimension_semantics=("arbitrary", "arbitrary"),
            vmem_limit_bytes=_vmem_limit(est)),
        name="out_projection",
    )(oa, ob, ga.reshape(1, da), gb.reshape(1, db), w, x)


def _topk_body(q_ref, keys_ref, idx_ref, gate_ref, sv_ref, si_ref, cs_ref, ci_ref, bs_ref, be_ref,
               *, tt, nkeys, topk):
    lowest = float(jnp.finfo(jnp.float32).min)
    iota_n = lax.broadcasted_iota(jnp.int32, (nkeys, tt), 0)
    nt = (((1,), (1,)), ((), ()))

    for p in range(2):
        s = lax.dot_general(keys_ref[p], q_ref[p].astype(BF16), nt, preferred_element_type=F32)

        def pick(k, s, p=p):
            m = jnp.max(s, axis=0, keepdims=True)
            ix = jnp.min(jnp.where(s == m, iota_n, nkeys), axis=0, keepdims=True)
            sv_ref[p, pl.ds(k, 1), :] = m
            si_ref[p, pl.ds(k, 1), :] = ix
            return jnp.where(iota_n == ix, lowest, s)

        lax.fori_loop(0, topk, pick, s)

    for a in range(topk):
        cs_ref[a * topk:(a + 1) * topk, :] = sv_ref[0, a:a + 1, :] + sv_ref[1]
        ci_ref[a * topk:(a + 1) * topk, :] = si_ref[0, a:a + 1, :] * nkeys + si_ref[1]

    ncand = topk * topk
    iota_c = lax.broadcasted_iota(jnp.int32, (ncand, tt), 0)

    def pick2(k, c):
        m = jnp.max(c, axis=0, keepdims=True)
        pos = jnp.min(jnp.where(c == m, iota_c, ncand), axis=0, keepdims=True)
        hit = iota_c == pos
        bs_ref[pl.ds(k, 1), :] = m
        be_ref[pl.ds(k, 1), :] = jnp.max(jnp.where(hit, ci_ref[...], -1), axis=0, keepdims=True)
        return jnp.where(hit, lowest, c)

    lax.fori_loop(0, topk, pick2, cs_ref[...])

    b = bs_ref[...]
    e = jnp.exp(b - jnp.max(b, axis=0, keepdims=True))
    gate_ref[...] = e / jnp.sum(e, axis=0, keepdims=True)
    idx_ref[...] = be_ref[...]


def _peer_topk(q_slabs, keys, *, tt=128):
    hp, T, c = q_slabs.shape
    heads = hp // 2
    nkeys = keys.shape[1]
    tt = min(tt, T)
    assert T % tt == 0
    topk = PEER_TOPK
    return pl.pallas_call(
        functools.partial(_topk_body, tt=tt, nkeys=nkeys, topk=topk),
        grid=(T // tt, heads),
        in_specs=[
            pl.BlockSpec((2, tt, c), lambda i, h: (h, i, 0)),
            pl.BlockSpec((2, nkeys, c), lambda i, h: (h, 0, 0)),
        ],
        out_specs=[pl.BlockSpec((topk, tt), lambda i, h: (h, i)),
                   pl.BlockSpec((topk, tt), lambda i, h: (h, i))],
        out_shape=[jax.ShapeDtypeStruct((heads * topk, T), jnp.int32),
                   jax.ShapeDtypeStruct((heads * topk, T), F32)],
        scratch_shapes=[pltpu.VMEM((2, topk, tt), F32), pltpu.VMEM((2, topk, tt), jnp.int32),
                        pltpu.VMEM((topk * topk, tt), F32), pltpu.VMEM((topk * topk, tt), jnp.int32),
                        pltpu.VMEM((topk, tt), F32), pltpu.VMEM((topk, tt), jnp.int32)],
        compiler_params=pltpu.CompilerParams(dimension_semantics=("arbitrary", "arbitrary")),
        name="peer_topk",
    )(q_slabs, keys)


GSIZE = SUBLANES
NGROUP = PEER_E // GSIZE


def _gelu_exact(x):
    return 0.5 * x * (1.0 + lax.erf(x * (2.0 ** -0.5)))


def _sublane_sums(vs, sub):
    lo = sub < 4
    halves = []
    for j in range(4):
        a, b = vs[j], vs[j + 4]
        halves.append(jnp.where(lo, a, b) + pltpu.roll(jnp.where(lo, b, a), 4, 0))
    done = []
    for j, (sa, sb) in enumerate(((6, 7), (6, 1), (2, 7), (2, 1))):
        c = halves[j] + pltpu.roll(halves[j], sa, 0)
        done.append(c + pltpu.roll(c, sb, 0))
    q = sub & 3
    return jnp.where(q == 0, done[0], jnp.where(q == 1, done[1], jnp.where(q == 2, done[2], done[3])))


def _peer_ffn_body(idxc_ref, idxn_ref, gate_ref, x_ref, h_ref, gf_ref, tab_ref, o_ref,
                   buf, sem, *, ctok, nrow):
    i = pl.program_id(0)
    n = pl.num_programs(0)
    sub = lax.broadcasted_iota(jnp.int32, (SUBLANES, LANES), 0)
    lane = lax.broadcasted_iota(jnp.int32, (SUBLANES, LANES), 1)
    hi_mask = jnp.uint32(0xFFFF0000)

    def start_row(idx_ref, row, dst_slot, c, j):
        k, g = divmod(j, NGROUP)
        pltpu.make_async_copy(tab_ref.at[idx_ref[row, j]], buf.at[dst_slot, c * NGROUP + g, :, k, :],
                              sem.at[dst_slot]).start()

    def wait_slot(s):
        pltpu.make_async_copy(buf.at[s], buf.at[s], sem.at[s]).wait()

    @pl.when(i == 0)
    def _():
        def first(c, carry):
            for j in range(PEER_E):
                start_row(idxc_ref, c, 0, c, j)
            return carry
        lax.fori_loop(0, ctok, first, 0)

    for half in range(2):
        wait_slot(half)
        gate_h = gate_ref[half]
        results = []
        for c in range(ctok):
            tok = half * ctok + c
            if half == 0:
                nxt = lambda j, c=c: start_row(idxc_ref, ctok + c, 1, c, j)
            else:
                nxt = lambda j, c=c: start_row(idxn_ref, c, 0, c, j)
            xb = [jnp.broadcast_to(x_ref[tok, s:s + 1, :], (SUBLANES, LANES)) for s in range(nrow)]
            dots = jnp.zeros((SUBLANES, LANES), F32)
            for g in range(NGROUP):
                for j in range(g * 4, g * 4 + 4):
                    nxt(j)
                r = None
                for s in range(nrow):
                    w = buf[half, c * NGROUP + g, s]
                    p = pltpu.bitcast(w << 16, F32) * xb[s]
                    r = p if r is None else r + p
                d = jnp.sum(r, axis=1, keepdims=True)
                dots = jnp.where(lane == c * NGROUP + g, d, dots)
            act = _gelu_exact(dots) * gate_h
            yacc = [None] * nrow
            for g in range(NGROUP):
                for j in range(PEER_E // 2 + g * 4, PEER_E // 2 + g * 4 + 4):
                    nxt(j)
                col = c * NGROUP + g
                a = jnp.broadcast_to(act[:, col:col + 1], (SUBLANES, LANES))
                for s in range(nrow):
                    w = buf[half, c * NGROUP + g, s]
                    t = a * pltpu.bitcast(w & hi_mask, F32)
                    yacc[s] = t if yacc[s] is None else yacc[s] + t
            y0 = _sublane_sums(yacc[:SUBLANES], sub)
            y1 = _sublane_sums(yacc[SUBLANES:], sub)
            z0 = h_ref[tok, :SUBLANES, :] + y0
            z1 = h_ref[tok, SUBLANES:, :] + y1
            ss = jnp.sum(z0 * z0 + z1 * z1, axis=1, keepdims=True)
            ms = jnp.sum(ss, axis=0, keepdims=True) * (1.0 / (nrow * LANES))
            inv = lax.rsqrt(ms + EPS)
            results.append((tok, z0 * inv * gf_ref[:SUBLANES, :], z1 * inv * gf_ref[SUBLANES:, :]))
        for tok, r0, r1 in results:
            o_ref[tok, :SUBLANES, :] = r0
            o_ref[tok, SUBLANES:, :] = r1

    @pl.when(i == n - 1)
    def _():
        wait_slot(0)


def _peer_ffn(idx, gate_h, xn3, h3, gf, table, *, ctok=8):
    T, nexp = idx.shape
    nrow = xn3.shape[1]
    step = 2 * ctok
    assert nexp == PEER_E and ctok == SUBLANES and T % step == 0 and nrow == 2 * SUBLANES
    n = T // step
    last = n - 1
    est = 2 * ctok * nexp * nrow * LANES * 4 + 6 * step * nrow * LANES * 4
    return pl.pallas_call(
        functools.partial(_peer_ffn_body, ctok=ctok, nrow=nrow),
        grid=(n,),
        in_specs=[
            pl.BlockSpec((step, nexp), lambda i: (i, 0), memory_space=pltpu.SMEM),
            pl.BlockSpec((step, nexp), lambda i: (jnp.minimum(i + 1, last), 0), memory_space=pltpu.SMEM),
            pl.BlockSpec((2, SUBLANES, LANES), lambda i: (i, 0, 0)),
            pl.BlockSpec((step, nrow, LANES), lambda i: (i, 0, 0)),
            pl.BlockSpec((step, nrow, LANES), lambda i: (i, 0, 0)),
            pl.BlockSpec((nrow, LANES), lambda i: (0, 0)),
            pl.BlockSpec(memory_space=pl.ANY),
        ],
        out_specs=pl.BlockSpec((step, nrow, LANES), lambda i: (i, 0, 0)),
        out_shape=jax.ShapeDtypeStruct((T, nrow, LANES), F32),
        scratch_shapes=[pltpu.VMEM((2, ctok * NGROUP, nrow, GSIZE, LANES), jnp.uint32),
                        pltpu.SemaphoreType.DMA((2,))],
        compiler_params=pltpu.CompilerParams(
            dimension_semantics=("arbitrary",),
            vmem_limit_bytes=_vmem_limit(est)),
        name="peer_ffn",
    )(idx, idx, gate_h, xn3, h3, gf, table)


def _pack_expert_table(u, v):
    ub = lax.bitcast_convert_type(u.astype(BF16), jnp.uint16).astype(jnp.uint32)
    vb = lax.bitcast_convert_type(v.astype(BF16), jnp.uint16).astype(jnp.uint32)
    n, d = u.shape
    return (ub | (vb << 16)).reshape(n, d // LANES, LANES)


def _rope_tables(seq):
    half = QK_ROPE // 2
    freqs = ROPE_THETA ** (-jnp.arange(half, dtype=F32) / half)
    ang = jnp.arange(seq, dtype=F32)[:, None] * freqs[None, :]
    cos, sin = jnp.cos(ang), jnp.sin(ang)
    zeros = jnp.zeros((seq, LANES - QK_ROPE), F32)
    return (jnp.concatenate([cos, cos, zeros], axis=1),
            jnp.concatenate([-sin, sin, zeros], axis=1))


def _layer(h, ln1_g, w_in, q_a_norm, kv_a_norm, w_uq, w_ukv, out_norm_dil, out_norm_mla, w_o,
           ln2_g, peer_wq, peer_sub_keys, peer_u, peer_v, final_g):
    B, S, D = h.shape
    T = B * S
    H = N_HEADS
    d_dil = H * HEAD_DIM
    q_lora = q_a_norm.shape[0]
    kv_lora = kv_a_norm.shape[0]
    x2 = h.reshape(T, D)

    d_in = w_in.shape[1]
    d_in_pad = -(-d_in // LANES) * LANES
    w_in_p = jnp.pad(w_in, ((0, 0), (0, d_in_pad - d_in))).astype(BF16)
    z = _norm_matmul(x2, ln1_g, w_in_p, tn=d_in_pad // 3 if d_in_pad % (3 * LANES) == 0 else LANES)
    z3 = z.reshape(B, S, d_in_pad)

    slopes = 2.0 ** (-8.0 * jnp.arange(1, H + 1, dtype=F32) / H)
    o_a = _dilated_attention(z3, slopes, n_heads=H)

    wq = w_uq.reshape(q_lora, H, QK_NOPE + QK_ROPE)
    wq_pe = jnp.pad(wq[:, :, QK_NOPE:], ((0, 0), (0, 0), (0, LANES - QK_ROPE)))
    wq_p = jnp.concatenate([wq[:, :, :QK_NOPE].reshape(q_lora, H * QK_NOPE),
                            wq_pe.reshape(q_lora, H * LANES)], axis=1).astype(BF16)
    wkv = w_ukv.reshape(kv_lora, H, QK_NOPE + HEAD_DIM)
    wkv_p = jnp.concatenate([wkv[:, :, :QK_NOPE].reshape(kv_lora, H * QK_NOPE),
                             wkv[:, :, QK_NOPE:].reshape(kv_lora, H * HEAD_DIM)], axis=1).astype(BF16)
    assert q_lora == kv_lora and (3 * d_dil) % q_lora == 0
    q_mla = _norm_matmul(z, q_a_norm, wq_p, xcol=3 * d_dil // q_lora, kdim=q_lora)
    kv_mla = _norm_matmul(z, kv_a_norm, wkv_p, xcol=3 * d_dil // q_lora + 1, kdim=kv_lora)
    cos, sin_signed = _rope_tables(S)
    o_b = _mla_attention(q_mla.reshape(B, S, -1), kv_mla.reshape(B, S, -1), z3,
                         (3 * d_dil + q_lora + kv_lora) // LANES, cos, sin_signed, n_heads=H)

    h1 = _outproj(o_a.reshape(T, d_dil), o_b.reshape(T, -1), out_norm_dil, out_norm_mla,
                  w_o.astype(BF16), x2)

    q_slabs, xn2 = _norm_matmul(h1, ln2_g, peer_wq.astype(BF16), emit_xn=True, slab_out=True)
    keys = peer_sub_keys.reshape(PEER_HEADS * 2, PEER_NKEYS, -1).astype(BF16)
    idx_t, gate_t = _peer_topk(q_slabs, keys)
    table = _pack_expert_table(peer_u, peer_v)
    rows = D // LANES
    gate_h = gate_t.reshape(GSIZE, NGROUP, T // GSIZE, GSIZE).transpose(2, 0, 3, 1).reshape(T // GSIZE, GSIZE, LANES)
    out = _peer_ffn(idx_t.T, gate_h, xn2.reshape(T, rows, LANES), h1.reshape(T, rows, LANES),
                    final_g.reshape(rows, LANES), table)
    return out.reshape(B, S, D)


def kernel(x, ln1_g, w_in, q_a_norm, kv_a_norm, w_uq, w_ukv, out_norm_dil, out_norm_mla, w_o,
           ln2_g, peer_wq, peer_sub_keys, peer_u, peer_v, lnf_g):
    assert ln1_g.shape[0] == 1, "single-layer trunk"
    return _layer(x, ln1_g[0], w_in[0], q_a_norm[0], kv_a_norm[0], w_uq[0], w_ukv[0],
                  out_norm_dil[0], out_norm_mla[0], w_o[0], ln2_g[0], peer_wq[0],
                  peer_sub_keys[0], peer_u[0], peer_v[0], lnf_g)
```

```python
import functools
import math

import jax
import jax.numpy as jnp
from jax import lax
from jax.experimental import pallas as pl
from jax.experimental.pallas import tpu as pltpu

F32 = jnp.float32
BF16 = jnp.bfloat16

EPS = 1e-6
NEG = -1e30
HEAD_DIM = 128
BLOCK = 128
DIL_PATTERNS = ((128, 1), (512, 4), (2048, 16))
DIL_UNROLL = 8
N_HEADS = 8
QK_NOPE = 128
QK_ROPE = 64
ROPE_THETA = 10000.0
PEER_HEADS = 8
PEER_NKEYS = 128
PEER_TOPK = 16
PEER_E = PEER_HEADS * PEER_TOPK

LANES = 128
SUBLANES = 8
VMEM_CAP = 60000 * 1024


def _vmem_limit(nbytes):
    return int(min(VMEM_CAP, max(16 * 1024 * 1024, nbytes * 3 // 2)))


def _rms(x, g):
    return x * lax.rsqrt(jnp.mean(x * x, axis=-1, keepdims=True) + EPS) * g


def _norm_matmul_body(x_ref, g_ref, w_ref, *rest, emit_xn, slabs):
    if emit_xn:
        o_ref, xn_out_ref, xn_ref = rest
    else:
        o_ref, xn_ref = rest
    j = pl.program_id(1)

    @pl.when(j == 0)
    def _():
        xn = _rms(x_ref[...].astype(F32), g_ref[...])
        xn_ref[...] = xn.astype(BF16)
        if emit_xn:
            xn_out_ref[...] = xn

    res = jnp.dot(xn_ref[...], w_ref[...], preferred_element_type=F32)
    if slabs:
        for s in range(slabs):
            o_ref[s] = res[:, s * LANES:(s + 1) * LANES]
    else:
        o_ref[...] = res


def _norm_matmul(x, g, w, *, xcol=0, kdim=None, tm=512, tn=512, emit_xn=False, slab_out=False):
    T = x.shape[0]
    kdim = kdim or x.shape[1]
    N = w.shape[1]
    tm = min(tm, T)
    tn = min(tn, N)
    assert T % tm == 0 and N % tn == 0 and w.shape[0] == kdim
    slabs = tn // LANES if slab_out else 0
    if slab_out:
        out_shape = [jax.ShapeDtypeStruct((N // LANES, T, LANES), F32)]
        out_specs = [pl.BlockSpec((slabs, tm, LANES), lambda i, j: (j, i, 0))]
    else:
        out_shape = [jax.ShapeDtypeStruct((T, N), F32)]
        out_specs = [pl.BlockSpec((tm, tn), lambda i, j: (i, j))]
    if emit_xn:
        out_shape.append(jax.ShapeDtypeStruct((T, kdim), F32))
        out_specs.append(pl.BlockSpec((tm, kdim), lambda i, j: (i, 0)))
    est = 2 * (tm * kdim * 4 + kdim * tn * 2 + tm * tn * 4) + tm * kdim * 2
    if emit_xn:
        est += 2 * tm * kdim * 4
    outs = pl.pallas_call(
        functools.partial(_norm_matmul_body, emit_xn=emit_xn, slabs=slabs),
        grid=(T // tm, N // tn),
        in_specs=[
            pl.BlockSpec((tm, kdim), lambda i, j: (i, xcol)),
            pl.BlockSpec((1, kdim), lambda i, j: (0, 0)),
            pl.BlockSpec((kdim, tn), lambda i, j: (0, j)),
        ],
        out_specs=out_specs,
        out_shape=out_shape,
        scratch_shapes=[pltpu.VMEM((tm, kdim), BF16)],
        compiler_params=pltpu.CompilerParams(
            dimension_semantics=("arbitrary", "arbitrary"),
            vmem_limit_bytes=_vmem_limit(est)),
        name="norm_matmul",
    )(x, g.reshape(1, kdim).astype(F32), w)
    return outs if emit_xn else outs[0]


def _dilated_body(slopes_ref, q_ref, k_ref, v_ref, o_ref, o_scr, l_scr, *, seq, patterns, scale):
    h = pl.program_id(1)
    slope = slopes_ref[h]
    qi = lax.broadcasted_iota(jnp.int32, (BLOCK, 2 * BLOCK), 0)
    kj = lax.broadcasted_iota(jnp.int32, (BLOCK, 2 * BLOCK), 1)
    delta = qi + BLOCK - kj
    nt = (((1,), (1,)), ((), ()))

    for p, (window, d) in enumerate(patterns):
        steps = window // d
        nb = seq // d // BLOCK
        in_window = (delta >= 0) & (delta <= steps)
        bias = -slope * (delta * d).astype(F32)

        def block(t, carry, d=d, nb=nb, in_window=in_window, bias=bias, p=p):
            r = t // nb
            n = t % nb
            start = n * (BLOCK * d) + r
            pstart = jnp.maximum(n - 1, 0) * (BLOCK * d) + r
            rows = pl.ds(start, BLOCK, stride=d) if d > 1 else pl.ds(start, BLOCK)
            prow = pl.ds(pstart, BLOCK, stride=d) if d > 1 else pl.ds(pstart, BLOCK)
            q = q_ref[0, rows, :].astype(BF16)
            kk = jnp.concatenate([k_ref[0, prow, :], k_ref[0, rows, :]], axis=0).astype(BF16)
            vv = jnp.concatenate([v_ref[0, prow, :], v_ref[0, rows, :]], axis=0).astype(BF16)
            s = lax.dot_general(q, kk, nt, preferred_element_type=F32) * scale
            valid = in_window & ((kj >= BLOCK) | (n > 0))
            s = jnp.where(valid, s + bias, NEG)
            m = jnp.max(s, axis=-1, keepdims=True)
            e = jnp.exp(s - m)
            l = jnp.sum(e, axis=-1, keepdims=True)
            o = jnp.dot((e / l).astype(BF16), vv, preferred_element_type=F32)
            o_scr[p, rows, :] = o
            l_scr[p, rows, :] = jnp.broadcast_to(m + jnp.log(l), (BLOCK, HEAD_DIM))
            return carry

        lax.fori_loop(0, d * nb, block, 0, unroll=DIL_UNROLL)

    def mix(c, carry):
        rows = pl.ds(pl.multiple_of(c * BLOCK, BLOCK), BLOCK)
        ls = [l_scr[p, rows, :] for p in range(len(patterns))]
        m = functools.reduce(jnp.maximum, ls)
        es = [jnp.exp(l - m) for l in ls]
        den = functools.reduce(jnp.add, es)
        num = functools.reduce(jnp.add, [e * o_scr[p, rows, :] for p, e in enumerate(es)])
        o_ref[0, rows, :] = num / den
        return carry

    lax.fori_loop(0, seq // BLOCK, mix, 0)


def _dilated_attention(z3, slopes, *, n_heads, patterns=DIL_PATTERNS):
    B, S, _ = z3.shape
    for _, d in patterns:
        assert S % (d * BLOCK) == 0
    blk = (1, S, HEAD_DIM)
    est = 2 * 4 * S * HEAD_DIM * 4 + 2 * len(patterns) * S * HEAD_DIM * 4
    return pl.pallas_call(
        functools.partial(_dilated_body, seq=S, patterns=patterns, scale=HEAD_DIM ** -0.5),
        grid=(B, n_heads),
        in_specs=[
            pl.BlockSpec(memory_space=pltpu.SMEM),
            pl.BlockSpec(blk, lambda b, h: (b, 0, h)),
            pl.BlockSpec(blk, lambda b, h: (b, 0, n_heads + h)),
            pl.BlockSpec(blk, lambda b, h: (b, 0, 2 * n_heads + h)),
        ],
        out_specs=pl.BlockSpec(blk, lambda b, h: (b, 0, h)),
        out_shape=jax.ShapeDtypeStruct((B, S, n_heads * HEAD_DIM), F32),
        scratch_shapes=[pltpu.VMEM((len(patterns), S, HEAD_DIM), F32),
                        pltpu.VMEM((len(patterns), S, HEAD_DIM), F32)],
        compiler_params=pltpu.CompilerParams(
            dimension_semantics=("arbitrary", "arbitrary"),
            vmem_limit_bytes=_vmem_limit(est)),
        name="dilated_attention",
    )(slopes, z3, z3, z3)


def _rope(x, cos, sin_signed):
    lane = lax.broadcasted_iota(jnp.int32, x.shape, 1)
    half = QK_ROPE // 2
    swapped = jnp.where(lane < half, pltpu.roll(x, LANES - half, 1), pltpu.roll(x, half, 1))
    return x * cos + swapped * sin_signed


def _mla_body(qn_ref, qp_ref, kn_ref, v_ref, kr_ref, cos_ref, sin_ref,
              o_ref, qcat_scr, kcat_scr, v_scr, *, seq, tq, scale):
    cos, sin = cos_ref[...], sin_ref[...]
    qcat_scr[:, :LANES] = qn_ref[0].astype(BF16)
    qcat_scr[:, LANES:] = _rope(qp_ref[0], cos, sin).astype(BF16)
    kcat_scr[:, :LANES] = kn_ref[0].astype(BF16)
    kcat_scr[:, LANES:] = _rope(kr_ref[0], cos, sin).astype(BF16)
    v_scr[...] = v_ref[0].astype(BF16)
    nt = (((1,), (1,)), ((), ()))

    for qi in range(seq // tq):
        ext = (qi + 1) * tq
        rows = slice(qi * tq, ext)
        s = lax.dot_general(qcat_scr[rows, :], kcat_scr[:ext, :], nt, preferred_element_type=F32) * scale
        qpos = qi * tq + lax.broadcasted_iota(jnp.int32, (tq, ext), 0)
        kpos = lax.broadcasted_iota(jnp.int32, (tq, ext), 1)
        s = jnp.where(kpos <= qpos, s, NEG)
        e = jnp.exp(s - jnp.max(s, axis=-1, keepdims=True))
        l = jnp.sum(e, axis=-1, keepdims=True)
        o_ref[0, rows, :] = jnp.dot(e.astype(BF16), v_scr[:ext, :], preferred_element_type=F32) / l


def _mla_attention(q3, kv3, z3, kr_col, cos, sin_signed, *, n_heads, tq=512):
    B, S, _ = q3.shape
    tq = min(tq, S)
    assert S % tq == 0
    blk = (1, S, LANES)
    est = 2 * 8 * S * LANES * 4 + 5 * S * LANES * 2 + 3 * tq * S * 4
    return pl.pallas_call(
        functools.partial(_mla_body, seq=S, tq=tq, scale=(QK_NOPE + QK_ROPE) ** -0.5),
        grid=(B, n_heads),
        in_specs=[
            pl.BlockSpec(blk, lambda b, h: (b, 0, h)),
            pl.BlockSpec(blk, lambda b, h: (b, 0, n_heads + h)),
            pl.BlockSpec(blk, lambda b, h: (b, 0, h)),
            pl.BlockSpec(blk, lambda b, h: (b, 0, n_heads + h)),
            pl.BlockSpec(blk, lambda b, h: (b, 0, kr_col)),
            pl.BlockSpec((S, LANES), lambda b, h: (0, 0)),
            pl.BlockSpec((S, LANES), lambda b, h: (0, 0)),
        ],
        out_specs=pl.BlockSpec(blk, lambda b, h: (b, 0, h)),
        out_shape=jax.ShapeDtypeStruct((B, S, n_heads * HEAD_DIM), F32),
        scratch_shapes=[pltpu.VMEM((S, 2 * LANES), BF16), pltpu.VMEM((S, 2 * LANES), BF16),
                        pltpu.VMEM((S, LANES), BF16)],
        compiler_params=pltpu.CompilerParams(
            dimension_semantics=("arbitrary", "arbitrary"),
            vmem_limit_bytes=_vmem_limit(est)),
        name="mla_attention",
    )(q3, q3, kv3, kv3, z3, cos, sin_signed)


def _outproj_body(oa_ref, ob_ref, ga_ref, gb_ref, w_ref, x_ref, o_ref, xn_ref, *, da):
    j = pl.program_id(1)

    @pl.when(j == 0)
    def _():
        xn_ref[:, :da] = _rms(oa_ref[...], ga_ref[...]).astype(BF16)
        xn_ref[:, da:] = _rms(ob_ref[...], gb_ref[...]).astype(BF16)

    o_ref[...] = x_ref[...] + jnp.dot(xn_ref[...], w_ref[...], preferred_element_type=F32)


def _outproj(oa, ob, ga, gb, w, x, *, tm=512, tn=512):
    T, da = oa.shape
    db = ob.shape[1]
    N = w.shape[1]
    tm, tn = min(tm, T), min(tn, N)
    assert T % tm == 0 and N % tn == 0 and w.shape[0] == da + db
    est = 2 * (tm * (da + db) * 4 + (da + db) * tn * 2 + 2 * tm * tn * 4) + tm * (da + db) * 2
    return pl.pallas_call(
        functools.partial(_outproj_body, da=da),
        grid=(T // tm, N // tn),
        in_specs=[
            pl.BlockSpec((tm, da), lambda i, j: (i, 0)),
            pl.BlockSpec((tm, db), lambda i, j: (i, 0)),
            pl.BlockSpec((1, da), lambda i, j: (0, 0)),
            pl.BlockSpec((1, db), lambda i, j: (0, 0)),
            pl.BlockSpec((da + db, tn), lambda i, j: (0, j)),
            pl.BlockSpec((tm, tn), lambda i, j: (i, j)),
        ],
        out_specs=pl.BlockSpec((tm, tn), lambda i, j: (i, j)),
        out_shape=jax.ShapeDtypeStruct((T, N), F32),
        scratch_shapes=[pltpu.VMEM((tm, da + db), BF16)],
        compiler_params=pltpu.CompilerParams(
            dimension_semantics=("arbitrary", "arbitrary"),
            vmem_limit_bytes=_vmem_limit(est)),
        name="out_projection",
    )(oa, ob, ga.reshape(1, da), gb.reshape(1, db), w, x)


TOPK_HEADS = 2


def _candidate_blocks(topk):
    blocks = []
    a = 0
    while a < topk:
        nb = topk // (a + 1)
        if nb >= SUBLANES:
            blocks += [(a, 1, b0, min(SUBLANES, nb - b0)) for b0 in range(0, nb, SUBLANES)]
            a += 1
        elif nb > 1:
            blocks.append((a, 1, 0, nb))
            a += 1
        else:
            na = min(SUBLANES, topk - a)
            blocks.append((a, na, 0, 1))
            a += na
    return blocks


def _topk_body(q_ref, keys_ref, idx_ref, gate_ref, s_ref, sv_ref, si_ref, cs_ref, ci_ref, bs_ref, be_ref,
               *, tt, nkeys, topk, blocks):
    lowest = float(jnp.finfo(jnp.float32).min)
    nchain = 2 * TOPK_HEADS
    iota_n = lax.broadcasted_iota(jnp.int32, (nkeys, tt), 0)
    sub = lax.broadcasted_iota(jnp.int32, (SUBLANES, tt), 0)
    nt = (((1,), (1,)), ((), ()))

    for ch in range(nchain):
        s_ref[ch] = lax.dot_general(keys_ref[ch], q_ref[ch].astype(BF16), nt, preferred_element_type=F32)

    def pick(k, carry):
        for ch in range(nchain):
            s = s_ref[ch]
            m = jnp.max(s, axis=0, keepdims=True)
            ix = jnp.min(jnp.where(s == m, iota_n, nkeys), axis=0, keepdims=True)
            sv_ref[ch, pl.ds(k, 1), :] = m
            si_ref[ch, pl.ds(k, 1), :] = ix
            s_ref[ch] = jnp.where(iota_n == ix, lowest, s)
        return carry

    lax.fori_loop(0, topk, pick, 0)

    pos_blocks = []
    for r, (a0, na, b0, nb) in enumerate(blocks):
        rows = slice(r * SUBLANES, (r + 1) * SUBLANES)
        if na == 1:
            live = sub < nb
            pos_blocks.append(a0 * topk + b0 + sub)
        else:
            live = sub < na
            pos_blocks.append((a0 + sub) * topk)
        for hd in range(TOPK_HEADS):
            if na == 1:
                cs = sv_ref[2 * hd, a0:a0 + 1, :] + sv_ref[2 * hd + 1, b0:b0 + SUBLANES, :]
                ci = si_ref[2 * hd, a0:a0 + 1, :] * nkeys + si_ref[2 * hd + 1, b0:b0 + SUBLANES, :]
            else:
                cs = sv_ref[2 * hd, a0:a0 + SUBLANES, :] + sv_ref[2 * hd + 1, 0:1, :]
                ci = si_ref[2 * hd, a0:a0 + SUBLANES, :] * nkeys + si_ref[2 * hd + 1, 0:1, :]
            cs_ref[hd, rows, :] = jnp.where(live, cs, lowest)
            ci_ref[hd, rows, :] = ci
    pos = jnp.concatenate(pos_blocks, axis=0)
    big = topk * topk

    def pick2(k, carry):
        for hd in range(TOPK_HEADS):
            c = cs_ref[hd]
            m = jnp.max(c, axis=0, keepdims=True)
            first = jnp.min(jnp.where(c == m, pos, big), axis=0, keepdims=True)
            hit = pos == first
            bs_ref[hd, pl.ds(k, 1), :] = m
            be_ref[hd, pl.ds(k, 1), :] = jnp.max(jnp.where(hit, ci_ref[hd], -1), axis=0, keepdims=True)
            cs_ref[hd] = jnp.where(hit, lowest, c)
        return carry

    lax.fori_loop(0, topk, pick2, 0)

    for hd in range(TOPK_HEADS):
        b = bs_ref[hd]
        e = jnp.exp(b - jnp.max(b, axis=0, keepdims=True))
        gate_ref[hd * topk:(hd + 1) * topk, :] = e / jnp.sum(e, axis=0, keepdims=True)
        idx_ref[hd * topk:(hd + 1) * topk, :] = be_ref[hd]


def _peer_topk(q_slabs, keys, *, tt=128):
    hp, T, c = q_slabs.shape
    heads = hp // 2
    nkeys = keys.shape[1]
    tt = min(tt, T)
    topk = PEER_TOPK
    assert T % tt == 0 and heads % TOPK_HEADS == 0 and topk % SUBLANES == 0
    blocks = _candidate_blocks(topk)
    ncand = len(blocks) * SUBLANES
    nchain = 2 * TOPK_HEADS
    rows = TOPK_HEADS * topk
    return pl.pallas_call(
        functools.partial(_topk_body, tt=tt, nkeys=nkeys, topk=topk, blocks=blocks),
        grid=(T // tt, heads // TOPK_HEADS),
        in_specs=[
            pl.BlockSpec((nchain, tt, c), lambda i, h: (h, i, 0)),
            pl.BlockSpec((nchain, nkeys, c), lambda i, h: (h, 0, 0)),
        ],
        out_specs=[pl.BlockSpec((rows, tt), lambda i, h: (h, i)),
                   pl.BlockSpec((rows, tt), lambda i, h: (h, i))],
        out_shape=[jax.ShapeDtypeStruct((heads * topk, T), jnp.int32),
                   jax.ShapeDtypeStruct((heads * topk, T), F32)],
        scratch_shapes=[pltpu.VMEM((nchain, nkeys, tt), F32),
                        pltpu.VMEM((nchain, topk, tt), F32), pltpu.VMEM((nchain, topk, tt), jnp.int32),
                        pltpu.VMEM((TOPK_HEADS, ncand, tt), F32), pltpu.VMEM((TOPK_HEADS, ncand, tt), jnp.int32),
                        pltpu.VMEM((TOPK_HEADS, topk, tt), F32), pltpu.VMEM((TOPK_HEADS, topk, tt), jnp.int32)],
        compiler_params=pltpu.CompilerParams(dimension_semantics=("arbitrary", "arbitrary")),
        name="peer_topk",
    )(q_slabs, keys)


GSIZE = SUBLANES
NGROUP = PEER_E // GSIZE


def _gelu_exact(x):
    return 0.5 * x * (1.0 + lax.erf(x * (2.0 ** -0.5)))


def _sublane_sums(vs, sub):
    lo = sub < 4
    halves = []
    for j in range(4):
        a, b = vs[j], vs[j + 4]
        halves.append(jnp.where(lo, a, b) + pltpu.roll(jnp.where(lo, b, a), 4, 0))
    done = []
    for j, (sa, sb) in enumerate(((6, 7), (6, 1), (2, 7), (2, 1))):
        c = halves[j] + pltpu.roll(halves[j], sa, 0)
        done.append(c + pltpu.roll(c, sb, 0))
    q = sub & 3
    return jnp.where(q == 0, done[0], jnp.where(q == 1, done[1], jnp.where(q == 2, done[2], done[3])))


def _peer_ffn_body(idxc_ref, idxn_ref, gate_ref, x_ref, h_ref, gf_ref, tab_ref, o_ref,
                   buf, sem, *, ctok, nrow):
    i = pl.program_id(0)
    n = pl.num_programs(0)
    sub = lax.broadcasted_iota(jnp.int32, (SUBLANES, LANES), 0)
    lane = lax.broadcasted_iota(jnp.int32, (SUBLANES, LANES), 1)
    hi_mask = jnp.uint32(0xFFFF0000)

    def start_row(idx_ref, row, dst_slot, c, j):
        k, g = divmod(j, NGROUP)
        pltpu.make_async_copy(tab_ref.at[idx_ref[row, j]], buf.at[dst_slot, c * NGROUP + g, :, k, :],
                              sem.at[dst_slot]).start(priority=j % 2)

    def wait_slot(s):
        pltpu.make_async_copy(buf.at[s], buf.at[s], sem.at[s]).wait()

    @pl.when(i == 0)
    def _():
        def first(c, carry):
            for j in range(PEER_E):
                start_row(idxc_ref, c, 0, c, j)
            return carry
        lax.fori_loop(0, ctok, first, 0)

    for half in range(2):
        wait_slot(half)
        gate_h = gate_ref[half]
        results = []
        for c in range(ctok):
            tok = half * ctok + c
            if half == 0:
                nxt = lambda j, c=c: start_row(idxc_ref, ctok + c, 1, c, j)
            else:
                nxt = lambda j, c=c: start_row(idxn_ref, c, 0, c, j)
            xb = [jnp.broadcast_to(x_ref[tok, s:s + 1, :], (SUBLANES, LANES)) for s in range(nrow)]
            dots = jnp.zeros((SUBLANES, LANES), F32)
            for g in range(NGROUP):
                for j in range(g * 4, g * 4 + 4):
                    nxt(j)
                r = None
                for s in range(nrow):
                    w = buf[half, c * NGROUP + g, s]
                    p = pltpu.bitcast(w << 16, F32) * xb[s]
                    r = p if r is None else r + p
                d = jnp.sum(r, axis=1, keepdims=True)
                dots = jnp.where(lane == c * NGROUP + g, d, dots)
            act = _gelu_exact(dots) * gate_h
            yacc = [None] * nrow
            for g in range(NGROUP):
                for j in range(PEER_E // 2 + g * 4, PEER_E // 2 + g * 4 + 4):
                    nxt(j)
                col = c * NGROUP + g
                a = jnp.broadcast_to(act[:, col:col + 1], (SUBLANES, LANES))
                for s in range(nrow):
                    w = buf[half, c * NGROUP + g, s]
                    t = a * pltpu.bitcast(w & hi_mask, F32)
                    yacc[s] = t if yacc[s] is None else yacc[s] + t
            y0 = _sublane_sums(yacc[:SUBLANES], sub)
            y1 = _sublane_sums(yacc[SUBLANES:], sub)
            z0 = h_ref[tok, :SUBLANES, :] + y0
            z1 = h_ref[tok, SUBLANES:, :] + y1
            ss = jnp.sum(z0 * z0 + z1 * z1, axis=1, keepdims=True)
            ms = jnp.sum(ss, axis=0, keepdims=True) * (1.0 / (nrow * LANES))
            inv = lax.rsqrt(ms + EPS)
            results.append((tok, z0 * inv * gf_ref[:SUBLANES, :], z1 * inv * gf_ref[SUBLANES:, :]))
        for tok, r0, r1 in results:
            o_ref[tok, :SUBLANES, :] = r0
            o_ref[tok, SUBLANES:, :] = r1

    @pl.when(i == n - 1)
    def _():
        wait_slot(0)


def _peer_ffn(idx, gate_h, xn3, h3, gf, table, *, ctok=8):
    T, nexp = idx.shape
    nrow = xn3.shape[1]
    step = 2 * ctok
    assert nexp == PEER_E and ctok == SUBLANES and T % step == 0 and nrow == 2 * SUBLANES
    n = T // step
    last = n - 1
    est = 2 * ctok * nexp * nrow * LANES * 4 + 6 * step * nrow * LANES * 4
    return pl.pallas_call(
        functools.partial(_peer_ffn_body, ctok=ctok, nrow=nrow),
        grid=(n,),
        in_specs=[
            pl.BlockSpec((step, nexp), lambda i: (i, 0), memory_space=pltpu.SMEM),
            pl.BlockSpec((step, nexp), lambda i: (jnp.minimum(i + 1, last), 0), memory_space=pltpu.SMEM),
            pl.BlockSpec((2, SUBLANES, LANES), lambda i: (i, 0, 0)),
            pl.BlockSpec((step, nrow, LANES), lambda i: (i, 0, 0)),
            pl.BlockSpec((step, nrow, LANES), lambda i: (i, 0, 0)),
            pl.BlockSpec((nrow, LANES), lambda i: (0, 0)),
            pl.BlockSpec(memory_space=pl.ANY),
        ],
        out_specs=pl.BlockSpec((step, nrow, LANES), lambda i: (i, 0, 0)),
        out_shape=jax.ShapeDtypeStruct((T, nrow, LANES), F32),
        scratch_shapes=[pltpu.VMEM((2, ctok * NGROUP, nrow, GSIZE, LANES), jnp.uint32),
                        pltpu.SemaphoreType.DMA((2,))],
        compiler_params=pltpu.CompilerParams(
            dimension_semantics=("arbitrary",),
            vmem_limit_bytes=_vmem_limit(est)),
        name="peer_ffn",
    )(idx, idx, gate_h, xn3, h3, gf, table)


def _pack_expert_table(u, v):
    ub = lax.bitcast_convert_type(u.astype(BF16), jnp.uint16).astype(jnp.uint32)
    vb = lax.bitcast_convert_type(v.astype(BF16), jnp.uint16).astype(jnp.uint32)
    n, d = u.shape
    return (ub | (vb << 16)).reshape(n, d // LANES, LANES)


def _rope_tables(seq):
    half = QK_ROPE // 2
    freqs = ROPE_THETA ** (-jnp.arange(half, dtype=F32) / half)
    ang = jnp.arange(seq, dtype=F32)[:, None] * freqs[None, :]
    cos, sin = jnp.cos(ang), jnp.sin(ang)
    zeros = jnp.zeros((seq, LANES - QK_ROPE), F32)
    return (jnp.concatenate([cos, cos, zeros], axis=1),
            jnp.concatenate([-sin, sin, zeros], axis=1))


def _layer(h, ln1_g, w_in, q_a_norm, kv_a_norm, w_uq, w_ukv, out_norm_dil, out_norm_mla, w_o,
           ln2_g, peer_wq, peer_sub_keys, peer_u, peer_v, final_g):
    B, S, D = h.shape
    T = B * S
    H = N_HEADS
    d_dil = H * HEAD_DIM
    q_lora = q_a_norm.shape[0]
    kv_lora = kv_a_norm.shape[0]
    x2 = h.reshape(T, D)

    d_in = w_in.shape[1]
    d_in_pad = -(-d_in // LANES) * LANES
    w_in_p = jnp.pad(w_in, ((0, 0), (0, d_in_pad - d_in))).astype(BF16)
    z = _norm_matmul(x2, ln1_g, w_in_p, tn=d_in_pad // 3 if d_in_pad % (3 * LANES) == 0 else LANES)
    z3 = z.reshape(B, S, d_in_pad)

    slopes = 2.0 ** (-8.0 * jnp.arange(1, H + 1, dtype=F32) / H)
    o_a = _dilated_attention(z3, slopes, n_heads=H)

    wq = w_uq.reshape(q_lora, H, QK_NOPE + QK_ROPE)
    wq_pe = jnp.pad(wq[:, :, QK_NOPE:], ((0, 0), (0, 0), (0, LANES - QK_ROPE)))
    wq_p = jnp.concatenate([wq[:, :, :QK_NOPE].reshape(q_lora, H * QK_NOPE),
                            wq_pe.reshape(q_lora, H * LANES)], axis=1).astype(BF16)
    wkv = w_ukv.reshape(kv_lora, H, QK_NOPE + HEAD_DIM)
    wkv_p = jnp.concatenate([wkv[:, :, :QK_NOPE].reshape(kv_lora, H * QK_NOPE),
                             wkv[:, :, QK_NOPE:].reshape(kv_lora, H * HEAD_DIM)], axis=1).astype(BF16)
    assert q_lora == kv_lora and (3 * d_dil) % q_lora == 0
    q_mla = _norm_matmul(z, q_a_norm, wq_p, xcol=3 * d_dil // q_lora, kdim=q_lora)
    kv_mla = _norm_matmul(z, kv_a_norm, wkv_p, xcol=3 * d_dil // q_lora + 1, kdim=kv_lora)
    cos, sin_signed = _rope_tables(S)
    o_b = _mla_attention(q_mla.reshape(B, S, -1), kv_mla.reshape(B, S, -1), z3,
                         (3 * d_dil + q_lora + kv_lora) // LANES, cos, sin_signed, n_heads=H)

    h1 = _outproj(o_a.reshape(T, d_dil), o_b.reshape(T, -1), out_norm_dil, out_norm_mla,
                  w_o.astype(BF16), x2)

    q_slabs, xn2 = _norm_matmul(h1, ln2_g, peer_wq.astype(BF16), emit_xn=True, slab_out=True)
    keys = peer_sub_keys.reshape(PEER_HEADS * 2, PEER_NKEYS, -1).astype(BF16)
    idx_t, gate_t = _peer_topk(q_slabs, keys)
    table = _pack_expert_table(peer_u, peer_v)
    rows = D // LANES
    gate_h = gate_t.reshape(GSIZE, NGROUP, T // GSIZE, GSIZE).transpose(2, 0, 3, 1).reshape(T // GSIZE, GSIZE, LANES)
    out = _peer_ffn(idx_t.T, gate_h, xn2.reshape(T, rows, LANES), h1.reshape(T, rows, LANES),
                    final_g.reshape(rows, LANES), table)
    return out.reshape(B, S, D)


def kernel(x, ln1_g, w_in, q_a_norm, kv_a_norm, w_uq, w_ukv, out_norm_dil, out_norm_mla, w_o,
           ln2_g, peer_wq, peer_sub_keys, peer_u, peer_v, lnf_g):
    assert ln1_g.shape[0] == 1, "single-layer trunk"
    return _layer(x, ln1_g[0], w_in[0], q_a_norm[0], kv_a_norm[0], w_uq[0], w_ukv[0],
                  out_norm_dil[0], out_norm_mla[0], w_o[0], ln2_g[0], peer_wq[0],
                  peer_sub_keys[0], peer_u[0], peer_v[0], lnf_g)
```

```python
import functools
import math

import jax
import jax.numpy as jnp
from jax import lax
from jax.experimental import pallas as pl
from jax.experimental.pallas import tpu as pltpu

F32 = jnp.float32
BF16 = jnp.bfloat16

EPS = 1e-6
NEG = -1e30
HEAD_DIM = 128
BLOCK = 128
DIL_PATTERNS = ((128, 1), (512, 4), (2048, 16))
DIL_UNROLL = 8
N_HEADS = 8
QK_NOPE = 128
QK_ROPE = 64
ROPE_THETA = 10000.0
PEER_HEADS = 8
PEER_NKEYS = 128
PEER_TOPK = 16
PEER_E = PEER_HEADS * PEER_TOPK

LANES = 128
SUBLANES = 8
VMEM_CAP = 60000 * 1024
ROW_TILE = 1024


def _vmem_limit(nbytes):
    return int(min(VMEM_CAP, max(16 * 1024 * 1024, nbytes * 3 // 2)))


def _rms(x, g):
    return x * lax.rsqrt(jnp.mean(x * x, axis=-1, keepdims=True) + EPS) * g


def _norm_matmul_body(x_ref, g_ref, w_ref, *rest, emit_xn, slabs):
    if emit_xn:
        o_ref, xn_out_ref, xn_ref = rest
    else:
        o_ref, xn_ref = rest
    j = pl.program_id(1)

    @pl.when(j == 0)
    def _():
        xn = _rms(x_ref[...].astype(F32), g_ref[...])
        xn_ref[...] = xn.astype(BF16)
        if emit_xn:
            xn_out_ref[...] = xn

    res = jnp.dot(xn_ref[...], w_ref[...], preferred_element_type=F32).astype(o_ref.dtype)
    if slabs:
        for s in range(slabs):
            o_ref[s] = res[:, s * LANES:(s + 1) * LANES]
    else:
        o_ref[...] = res


def _norm_matmul(x, g, w, *, xcol=0, kdim=None, tm=512, tn=512, emit_xn=False, slab_out=False,
                 out_dtype=F32):
    T = x.shape[0]
    kdim = kdim or x.shape[1]
    N = w.shape[1]
    tm = min(tm, T)
    tn = min(tn, N)
    assert T % tm == 0 and N % tn == 0 and w.shape[0] == kdim
    slabs = tn // LANES if slab_out else 0
    if slab_out:
        out_shape = [jax.ShapeDtypeStruct((N // LANES, T, LANES), out_dtype)]
        out_specs = [pl.BlockSpec((slabs, tm, LANES), lambda i, j: (j, i, 0))]
    else:
        out_shape = [jax.ShapeDtypeStruct((T, N), out_dtype)]
        out_specs = [pl.BlockSpec((tm, tn), lambda i, j: (i, j))]
    if emit_xn:
        out_shape.append(jax.ShapeDtypeStruct((T, kdim), F32))
        out_specs.append(pl.BlockSpec((tm, kdim), lambda i, j: (i, 0)))
    est = 2 * (tm * kdim * 4 + kdim * tn * 2 + tm * tn * 4) + tm * kdim * 2
    if emit_xn:
        est += 2 * tm * kdim * 4
    outs = pl.pallas_call(
        functools.partial(_norm_matmul_body, emit_xn=emit_xn, slabs=slabs),
        grid=(T // tm, N // tn),
        in_specs=[
            pl.BlockSpec((tm, kdim), lambda i, j: (i, xcol)),
            pl.BlockSpec((1, kdim), lambda i, j: (0, 0)),
            pl.BlockSpec((kdim, tn), lambda i, j: (0, j)),
        ],
        out_specs=out_specs,
        out_shape=out_shape,
        scratch_shapes=[pltpu.VMEM((tm, kdim), BF16)],
        compiler_params=pltpu.CompilerParams(
            dimension_semantics=("arbitrary", "arbitrary"),
            vmem_limit_bytes=_vmem_limit(est)),
        name="norm_matmul",
    )(x, g.reshape(1, kdim).astype(F32), w)
    return outs if emit_xn else outs[0]


def _dilated_body(slopes_ref, q_ref, k_ref, v_ref, o_ref, o_scr, l_scr, *, seq, patterns, scale):
    h = pl.program_id(1)
    slope = slopes_ref[h]
    qi = lax.broadcasted_iota(jnp.int32, (BLOCK, 2 * BLOCK), 0)
    kj = lax.broadcasted_iota(jnp.int32, (BLOCK, 2 * BLOCK), 1)
    delta = qi + BLOCK - kj
    nt = (((1,), (1,)), ((), ()))

    for p, (window, d) in enumerate(patterns):
        steps = window // d
        nb = seq // d // BLOCK
        in_window = (delta >= 0) & (delta <= steps)
        bias = -slope * (delta * d).astype(F32)

        def block(t, carry, d=d, nb=nb, in_window=in_window, bias=bias, p=p):
            r = t // nb
            n = t % nb
            start = n * (BLOCK * d) + r
            pstart = jnp.maximum(n - 1, 0) * (BLOCK * d) + r
            rows = pl.ds(start, BLOCK, stride=d) if d > 1 else pl.ds(start, BLOCK)
            prow = pl.ds(pstart, BLOCK, stride=d) if d > 1 else pl.ds(pstart, BLOCK)
            q = q_ref[0, rows, :].astype(BF16)
            kk = jnp.concatenate([k_ref[0, prow, :], k_ref[0, rows, :]], axis=0).astype(BF16)
            vv = jnp.concatenate([v_ref[0, prow, :], v_ref[0, rows, :]], axis=0).astype(BF16)
            s = lax.dot_general(q, kk, nt, preferred_element_type=F32) * scale
            valid = in_window & ((kj >= BLOCK) | (n > 0))
            s = jnp.where(valid, s + bias, NEG)
            m = jnp.max(s, axis=-1, keepdims=True)
            e = jnp.exp(s - m)
            l = jnp.sum(e, axis=-1, keepdims=True)
            o = jnp.dot((e / l).astype(BF16), vv, preferred_element_type=F32)
            o_scr[p, rows, :] = o
            l_scr[p, rows, :] = jnp.broadcast_to(m + jnp.log(l), (BLOCK, HEAD_DIM))
            return carry

        lax.fori_loop(0, d * nb, block, 0, unroll=DIL_UNROLL)

    def mix(c, carry):
        rows = pl.ds(pl.multiple_of(c * BLOCK, BLOCK), BLOCK)
        ls = [l_scr[p, rows, :] for p in range(len(patterns))]
        m = functools.reduce(jnp.maximum, ls)
        es = [jnp.exp(l - m) for l in ls]
        den = functools.reduce(jnp.add, es)
        num = functools.reduce(jnp.add, [e * o_scr[p, rows, :] for p, e in enumerate(es)])
        o_ref[0, rows, :] = (num / den).astype(o_ref.dtype)
        return carry

    lax.fori_loop(0, seq // BLOCK, mix, 0)


def _dilated_attention(z3, slopes, *, n_heads, patterns=DIL_PATTERNS):
    B, S, _ = z3.shape
    for _, d in patterns:
        assert S % (d * BLOCK) == 0
    blk = (1, S, HEAD_DIM)
    est = 2 * 4 * S * HEAD_DIM * 4 + 2 * len(patterns) * S * HEAD_DIM * 4
    return pl.pallas_call(
        functools.partial(_dilated_body, seq=S, patterns=patterns, scale=HEAD_DIM ** -0.5),
        grid=(B, n_heads),
        in_specs=[
            pl.BlockSpec(memory_space=pltpu.SMEM),
            pl.BlockSpec(blk, lambda b, h: (b, 0, h)),
            pl.BlockSpec(blk, lambda b, h: (b, 0, n_heads + h)),
            pl.BlockSpec(blk, lambda b, h: (b, 0, 2 * n_heads + h)),
        ],
        out_specs=pl.BlockSpec(blk, lambda b, h: (b, 0, h)),
        out_shape=jax.ShapeDtypeStruct((B, S, n_heads * HEAD_DIM), BF16),
        scratch_shapes=[pltpu.VMEM((len(patterns), S, HEAD_DIM), F32),
                        pltpu.VMEM((len(patterns), S, HEAD_DIM), F32)],
        compiler_params=pltpu.CompilerParams(
            dimension_semantics=("arbitrary", "arbitrary"),
            vmem_limit_bytes=_vmem_limit(est)),
        name="dilated_attention",
    )(slopes, z3, z3, z3)


def _rope(x, cos, sin_signed):
    lane = lax.broadcasted_iota(jnp.int32, x.shape, 1)
    half = QK_ROPE // 2
    swapped = jnp.where(lane < half, pltpu.roll(x, LANES - half, 1), pltpu.roll(x, half, 1))
    return x * cos + swapped * sin_signed


def _mla_body(qn_ref, qp_ref, kn_ref, v_ref, kr_ref, cos_ref, sin_ref,
              o_ref, qcat_scr, kcat_scr, *, seq, tq, scale):
    cos, sin = cos_ref[...], sin_ref[...]
    qcat_scr[:, :LANES] = qn_ref[0]
    qcat_scr[:, LANES:] = _rope(qp_ref[0].astype(F32), cos, sin).astype(BF16)
    kcat_scr[:, :LANES] = kn_ref[0]
    kcat_scr[:, LANES:] = _rope(kr_ref[0], cos, sin).astype(BF16)
    nt = (((1,), (1,)), ((), ()))

    for qi in range(seq // tq):
        ext = (qi + 1) * tq
        rows = slice(qi * tq, ext)
        s = lax.dot_general(qcat_scr[rows, :], kcat_scr[:ext, :], nt, preferred_element_type=F32) * scale
        qpos = qi * tq + lax.broadcasted_iota(jnp.int32, (tq, ext), 0)
        kpos = lax.broadcasted_iota(jnp.int32, (tq, ext), 1)
        s = jnp.where(kpos <= qpos, s, NEG)
        e = jnp.exp(s - jnp.max(s, axis=-1, keepdims=True))
        l = jnp.sum(e, axis=-1, keepdims=True)
        o = jnp.dot(e.astype(BF16), v_ref[0, :ext, :], preferred_element_type=F32) / l
        o_ref[0, rows, :] = o.astype(o_ref.dtype)


def _mla_attention(q3, kv3, z3, kr_col, cos, sin_signed, *, n_heads, tq=512):
    B, S, _ = q3.shape
    tq = min(tq, S)
    assert S % tq == 0 and q3.dtype == BF16 and kv3.dtype == BF16
    blk = (1, S, LANES)
    est = 2 * (5 * 2 + 3 * 4) * S * LANES + 4 * S * LANES * 2 + 3 * tq * S * 4
    return pl.pallas_call(
        functools.partial(_mla_body, seq=S, tq=tq, scale=(QK_NOPE + QK_ROPE) ** -0.5),
        grid=(B, n_heads),
        in_specs=[
            pl.BlockSpec(blk, lambda b, h: (b, 0, h)),
            pl.BlockSpec(blk, lambda b, h: (b, 0, n_heads + h)),
            pl.BlockSpec(blk, lambda b, h: (b, 0, h)),
            pl.BlockSpec(blk, lambda b, h: (b, 0, n_heads + h)),
            pl.BlockSpec(blk, lambda b, h: (b, 0, kr_col)),
            pl.BlockSpec((S, LANES), lambda b, h: (0, 0)),
            pl.BlockSpec((S, LANES), lambda b, h: (0, 0)),
        ],
        out_specs=pl.BlockSpec(blk, lambda b, h: (b, 0, h)),
        out_shape=jax.ShapeDtypeStruct((B, S, n_heads * HEAD_DIM), BF16),
        scratch_shapes=[pltpu.VMEM((S, 2 * LANES), BF16), pltpu.VMEM((S, 2 * LANES), BF16)],
        compiler_params=pltpu.CompilerParams(
            dimension_semantics=("arbitrary", "arbitrary"),
            vmem_limit_bytes=_vmem_limit(est)),
        name="mla_attention",
    )(q3, q3, kv3, kv3, z3, cos, sin_signed)


def _outproj_body(oa_ref, ob_ref, ga_ref, gb_ref, w_ref, x_ref, o_ref, xn_ref, *, da):
    j = pl.program_id(1)

    @pl.when(j == 0)
    def _():
        xn_ref[:, :da] = _rms(oa_ref[...].astype(F32), ga_ref[...]).astype(BF16)
        xn_ref[:, da:] = _rms(ob_ref[...].astype(F32), gb_ref[...]).astype(BF16)

    o_ref[...] = x_ref[...] + jnp.dot(xn_ref[...], w_ref[...], preferred_element_type=F32)


def _outproj(oa, ob, ga, gb, w, x, *, tm=512, tn=512):
    T, da = oa.shape
    db = ob.shape[1]
    N = w.shape[1]
    tm, tn = min(tm, T), min(tn, N)
    assert T % tm == 0 and N % tn == 0 and w.shape[0] == da + db
    est = 2 * (tm * (da + db) * 4 + (da + db) * tn * 2 + 2 * tm * tn * 4) + tm * (da + db) * 2
    return pl.pallas_call(
        functools.partial(_outproj_body, da=da),
        grid=(T // tm, N // tn),
        in_specs=[
            pl.BlockSpec((tm, da), lambda i, j: (i, 0)),
            pl.BlockSpec((tm, db), lambda i, j: (i, 0)),
            pl.BlockSpec((1, da), lambda i, j: (0, 0)),
            pl.BlockSpec((1, db), lambda i, j: (0, 0)),
            pl.BlockSpec((da + db, tn), lambda i, j: (0, j)),
            pl.BlockSpec((tm, tn), lambda i, j: (i, j)),
        ],
        out_specs=pl.BlockSpec((tm, tn), lambda i, j: (i, j)),
        out_shape=jax.ShapeDtypeStruct((T, N), F32),
        scratch_shapes=[pltpu.VMEM((tm, da + db), BF16)],
        compiler_params=pltpu.CompilerParams(
            dimension_semantics=("arbitrary", "arbitrary"),
            vmem_limit_bytes=_vmem_limit(est)),
        name="out_projection",
    )(oa, ob, ga.reshape(1, da), gb.reshape(1, db), w, x)


TOPK_HEADS = 2


def _candidate_blocks(topk):
    blocks = []
    a = 0
    while a < topk:
        nb = topk // (a + 1)
        if nb >= SUBLANES:
            blocks += [(a, 1, b0, min(SUBLANES, nb - b0)) for b0 in range(0, nb, SUBLANES)]
            a += 1
        elif nb > 1:
            blocks.append((a, 1, 0, nb))
            a += 1
        else:
            na = min(SUBLANES, topk - a)
            blocks.append((a, na, 0, 1))
            a += na
    return blocks


def _topk_body(q_ref, keys_ref, idx_ref, gate_ref, s_ref, sv_ref, si_ref, cs_ref, ci_ref, bs_ref, be_ref,
               *, tt, nkeys, topk, blocks):
    lowest = float(jnp.finfo(jnp.float32).min)
    nchain = 2 * TOPK_HEADS
    iota_n = lax.broadcasted_iota(jnp.int32, (nkeys, tt), 0)
    sub = lax.broadcasted_iota(jnp.int32, (SUBLANES, tt), 0)
    nt = (((1,), (1,)), ((), ()))

    for ch in range(nchain):
        s_ref[ch] = lax.dot_general(keys_ref[ch], q_ref[ch], nt, preferred_element_type=F32)

    def pick(k, carry):
        for ch in range(nchain):
            s = s_ref[ch]
            m = jnp.max(s, axis=0, keepdims=True)
            ix = jnp.min(jnp.where(s == m, iota_n, nkeys), axis=0, keepdims=True)
            sv_ref[ch, pl.ds(k, 1), :] = m
            si_ref[ch, pl.ds(k, 1), :] = ix
            s_ref[ch] = jnp.where(iota_n == ix, lowest, s)
        return carry

    lax.fori_loop(0, topk, pick, 0)

    pos_blocks = []
    for r, (a0, na, b0, nb) in enumerate(blocks):
        rows = slice(r * SUBLANES, (r + 1) * SUBLANES)
        if na == 1:
            live = sub < nb
            pos_blocks.append(a0 * topk + b0 + sub)
        else:
            live = sub < na
            pos_blocks.append((a0 + sub) * topk)
        for hd in range(TOPK_HEADS):
            if na == 1:
                cs = sv_ref[2 * hd, a0:a0 + 1, :] + sv_ref[2 * hd + 1, b0:b0 + SUBLANES, :]
                ci = si_ref[2 * hd, a0:a0 + 1, :] * nkeys + si_ref[2 * hd + 1, b0:b0 + SUBLANES, :]
            else:
                cs = sv_ref[2 * hd, a0:a0 + SUBLANES, :] + sv_ref[2 * hd + 1, 0:1, :]
                ci = si_ref[2 * hd, a0:a0 + SUBLANES, :] * nkeys + si_ref[2 * hd + 1, 0:1, :]
            cs_ref[hd, rows, :] = jnp.where(live, cs, lowest)
            ci_ref[hd, rows, :] = ci
    pos = jnp.concatenate(pos_blocks, axis=0)
    big = topk * topk

    def pick2(k, carry):
        for hd in range(TOPK_HEADS):
            c = cs_ref[hd]
            m = jnp.max(c, axis=0, keepdims=True)
            first = jnp.min(jnp.where(c == m, pos, big), axis=0, keepdims=True)
            hit = pos == first
            bs_ref[hd, pl.ds(k, 1), :] = m
            be_ref[hd, pl.ds(k, 1), :] = jnp.max(jnp.where(hit, ci_ref[hd], -1), axis=0, keepdims=True)
            cs_ref[hd] = jnp.where(hit, lowest, c)
        return carry

    lax.fori_loop(0, topk, pick2, 0)

    for hd in range(TOPK_HEADS):
        b = bs_ref[hd]
        e = jnp.exp(b - jnp.max(b, axis=0, keepdims=True))
        gate_ref[hd * topk:(hd + 1) * topk, :] = e / jnp.sum(e, axis=0, keepdims=True)
        idx_ref[hd * topk:(hd + 1) * topk, :] = be_ref[hd]


def _peer_topk(q_slabs, keys, *, tt=128):
    hp, T, c = q_slabs.shape
    heads = hp // 2
    nkeys = keys.shape[1]
    tt = min(tt, T)
    topk = PEER_TOPK
    assert T % tt == 0 and heads % TOPK_HEADS == 0 and topk % SUBLANES == 0
    blocks = _candidate_blocks(topk)
    ncand = len(blocks) * SUBLANES
    nchain = 2 * TOPK_HEADS
    rows = TOPK_HEADS * topk
    return pl.pallas_call(
        functools.partial(_topk_body, tt=tt, nkeys=nkeys, topk=topk, blocks=blocks),
        grid=(T // tt, heads // TOPK_HEADS),
        in_specs=[
            pl.BlockSpec((nchain, tt, c), lambda i, h: (h, i, 0)),
            pl.BlockSpec((nchain, nkeys, c), lambda i, h: (h, 0, 0)),
        ],
        out_specs=[pl.BlockSpec((rows, tt), lambda i, h: (h, i)),
                   pl.BlockSpec((rows, tt), lambda i, h: (h, i))],
        out_shape=[jax.ShapeDtypeStruct((heads * topk, T), jnp.int32),
                   jax.ShapeDtypeStruct((heads * topk, T), F32)],
        scratch_shapes=[pltpu.VMEM((nchain, nkeys, tt), F32),
                        pltpu.VMEM((nchain, topk, tt), F32), pltpu.VMEM((nchain, topk, tt), jnp.int32),
                        pltpu.VMEM((TOPK_HEADS, ncand, tt), F32), pltpu.VMEM((TOPK_HEADS, ncand, tt), jnp.int32),
                        pltpu.VMEM((TOPK_HEADS, topk, tt), F32), pltpu.VMEM((TOPK_HEADS, topk, tt), jnp.int32)],
        compiler_params=pltpu.CompilerParams(dimension_semantics=("arbitrary", "arbitrary")),
        name="peer_topk",
    )(q_slabs, keys)


GSIZE = SUBLANES
NGROUP = PEER_E // GSIZE


def _gelu_exact(x):
    return 0.5 * x * (1.0 + lax.erf(x * (2.0 ** -0.5)))


def _sublane_sums(vs, sub):
    lo = sub < 4
    halves = []
    for j in range(4):
        a, b = vs[j], vs[j + 4]
        halves.append(jnp.where(lo, a, b) + pltpu.roll(jnp.where(lo, b, a), 4, 0))
    done = []
    for j, (sa, sb) in enumerate(((6, 7), (6, 1), (2, 7), (2, 1))):
        c = halves[j] + pltpu.roll(halves[j], sa, 0)
        done.append(c + pltpu.roll(c, sb, 0))
    q = sub & 3
    return jnp.where(q == 0, done[0], jnp.where(q == 1, done[1], jnp.where(q == 2, done[2], done[3])))


def _peer_ffn_body(idxc_ref, idxn_ref, gate_ref, x_ref, h_ref, gf_ref, tab_ref, o_ref,
                   buf, sem, *, ctok, nrow):
    i = pl.program_id(0)
    n = pl.num_programs(0)
    sub = lax.broadcasted_iota(jnp.int32, (SUBLANES, LANES), 0)
    lane = lax.broadcasted_iota(jnp.int32, (SUBLANES, LANES), 1)
    hi_mask = jnp.uint32(0xFFFF0000)

    def start_row(idx_ref, row, dst_slot, c, j):
        k, g = divmod(j, NGROUP)
        pltpu.make_async_copy(tab_ref.at[idx_ref[row, j]], buf.at[dst_slot, c * NGROUP + g, :, k, :],
                              sem.at[dst_slot]).start(priority=j % 2)

    def wait_slot(s):
        pltpu.make_async_copy(buf.at[s], buf.at[s], sem.at[s]).wait()

    @pl.when(i == 0)
    def _():
        def first(c, carry):
            for j in range(PEER_E):
                start_row(idxc_ref, c, 0, c, j)
            return carry
        lax.fori_loop(0, ctok, first, 0)

    for half in range(2):
        wait_slot(half)
        gate_h = gate_ref[half]
        yaccs = []
        for c in range(ctok):
            tok = half * ctok + c
            if half == 0:
                nxt = lambda j, c=c: start_row(idxc_ref, ctok + c, 1, c, j)
            else:
                nxt = lambda j, c=c: start_row(idxn_ref, c, 0, c, j)
            xb = [jnp.broadcast_to(x_ref[tok:tok + 1, s * LANES:(s + 1) * LANES], (SUBLANES, LANES))
                  for s in range(nrow)]
            dots = jnp.zeros((SUBLANES, LANES), F32)
            for g in range(NGROUP):
                for j in range(g * 4, g * 4 + 4):
                    nxt(j)
                r = None
                for s in range(nrow):
                    w = buf[half, c * NGROUP + g, s]
                    p = pltpu.bitcast(w << 16, F32) * xb[s]
                    r = p if r is None else r + p
                d = jnp.sum(r, axis=1, keepdims=True)
                dots = jnp.where(lane == c * NGROUP + g, d, dots)
            act = _gelu_exact(dots) * gate_h
            yacc = [None] * nrow
            for g in range(NGROUP):
                for j in range(PEER_E // 2 + g * 4, PEER_E // 2 + g * 4 + 4):
                    nxt(j)
                col = c * NGROUP + g
                a = jnp.broadcast_to(act[:, col:col + 1], (SUBLANES, LANES))
                for s in range(nrow):
                    w = buf[half, c * NGROUP + g, s]
                    t = a * pltpu.bitcast(w & hi_mask, F32)
                    yacc[s] = t if yacc[s] is None else yacc[s] + t
            yaccs.append(yacc)
        toks = slice(half * ctok, (half + 1) * ctok)
        zs = []
        ss = jnp.zeros((SUBLANES, LANES), F32)
        for s in range(nrow):
            cols = slice(s * LANES, (s + 1) * LANES)
            z = h_ref[toks, cols] + _sublane_sums([yaccs[c][s] for c in range(ctok)], sub)
            ss = ss + z * z
            zs.append(z)
        inv = lax.rsqrt(jnp.sum(ss, axis=1, keepdims=True) * (1.0 / (nrow * LANES)) + EPS)
        for s in range(nrow):
            cols = slice(s * LANES, (s + 1) * LANES)
            o_ref[toks, cols] = zs[s] * inv * gf_ref[:, cols]

    @pl.when(i == n - 1)
    def _():
        wait_slot(0)


def _peer_ffn(idx, gate_h, xn, h, gf, table, *, ctok=8):
    T, nexp = idx.shape
    D = xn.shape[1]
    nrow = D // LANES
    step = 2 * ctok
    assert nexp == PEER_E and ctok == SUBLANES and T % step == 0 and table.shape[1:] == (nrow, LANES)
    n = T // step
    last = n - 1
    est = 2 * ctok * nexp * nrow * LANES * 4 + 6 * step * D * 4
    return pl.pallas_call(
        functools.partial(_peer_ffn_body, ctok=ctok, nrow=nrow),
        grid=(n,),
        in_specs=[
            pl.BlockSpec((step, nexp), lambda i: (i, 0), memory_space=pltpu.SMEM),
            pl.BlockSpec((step, nexp), lambda i: (jnp.minimum(i + 1, last), 0), memory_space=pltpu.SMEM),
            pl.BlockSpec((2, SUBLANES, LANES), lambda i: (i, 0, 0)),
            pl.BlockSpec((step, D), lambda i: (i, 0)),
            pl.BlockSpec((step, D), lambda i: (i, 0)),
            pl.BlockSpec((1, D), lambda i: (0, 0)),
            pl.BlockSpec(memory_space=pl.ANY),
        ],
        out_specs=pl.BlockSpec((step, D), lambda i: (i, 0)),
        out_shape=jax.ShapeDtypeStruct((T, D), F32),
        scratch_shapes=[pltpu.VMEM((2, ctok * NGROUP, nrow, GSIZE, LANES), jnp.uint32),
                        pltpu.SemaphoreType.DMA((2,))],
        compiler_params=pltpu.CompilerParams(
            dimension_semantics=("arbitrary",),
            vmem_limit_bytes=_vmem_limit(est)),
        name="peer_ffn",
    )(idx, idx, gate_h, xn, h, gf, table)


def _pack_expert_table(u, v):
    ub = lax.bitcast_convert_type(u.astype(BF16), jnp.uint16).astype(jnp.uint32)
    vb = lax.bitcast_convert_type(v.astype(BF16), jnp.uint16).astype(jnp.uint32)
    n, d = u.shape
    return (ub | (vb << 16)).reshape(n, d // LANES, LANES)


def _rope_tables(seq):
    half = QK_ROPE // 2
    freqs = ROPE_THETA ** (-jnp.arange(half, dtype=F32) / half)
    ang = jnp.arange(seq, dtype=F32)[:, None] * freqs[None, :]
    cos, sin = jnp.cos(ang), jnp.sin(ang)
    zeros = jnp.zeros((seq, LANES - QK_ROPE), F32)
    return (jnp.concatenate([cos, cos, zeros], axis=1),
            jnp.concatenate([-sin, sin, zeros], axis=1))


def _layer(h, ln1_g, w_in, q_a_norm, kv_a_norm, w_uq, w_ukv, out_norm_dil, out_norm_mla, w_o,
           ln2_g, peer_wq, peer_sub_keys, peer_u, peer_v, final_g):
    B, S, D = h.shape
    T = B * S
    H = N_HEADS
    d_dil = H * HEAD_DIM
    q_lora = q_a_norm.shape[0]
    kv_lora = kv_a_norm.shape[0]
    x2 = h.reshape(T, D)

    d_in = w_in.shape[1]
    d_in_pad = -(-d_in // LANES) * LANES
    w_in_p = jnp.pad(w_in, ((0, 0), (0, d_in_pad - d_in))).astype(BF16)
    z = _norm_matmul(x2, ln1_g, w_in_p, tm=ROW_TILE,
                     tn=d_in_pad // 3 if d_in_pad % (3 * LANES) == 0 else LANES)
    z3 = z.reshape(B, S, d_in_pad)

    slopes = 2.0 ** (-8.0 * jnp.arange(1, H + 1, dtype=F32) / H)
    o_a = _dilated_attention(z3, slopes, n_heads=H)

    wq = w_uq.reshape(q_lora, H, QK_NOPE + QK_ROPE)
    wq_pe = jnp.pad(wq[:, :, QK_NOPE:], ((0, 0), (0, 0), (0, LANES - QK_ROPE)))
    wq_p = jnp.concatenate([wq[:, :, :QK_NOPE].reshape(q_lora, H * QK_NOPE),
                            wq_pe.reshape(q_lora, H * LANES)], axis=1).astype(BF16)
    wkv = w_ukv.reshape(kv_lora, H, QK_NOPE + HEAD_DIM)
    wkv_p = jnp.concatenate([wkv[:, :, :QK_NOPE].reshape(kv_lora, H * QK_NOPE),
                             wkv[:, :, QK_NOPE:].reshape(kv_lora, H * HEAD_DIM)], axis=1).astype(BF16)
    assert q_lora == kv_lora and (3 * d_dil) % q_lora == 0
    q_mla = _norm_matmul(z, q_a_norm, wq_p, xcol=3 * d_dil // q_lora, kdim=q_lora, out_dtype=BF16)
    kv_mla = _norm_matmul(z, kv_a_norm, wkv_p, xcol=3 * d_dil // q_lora + 1, kdim=kv_lora, out_dtype=BF16)
    cos, sin_signed = _rope_tables(S)
    o_b = _mla_attention(q_mla.reshape(B, S, -1), kv_mla.reshape(B, S, -1), z3,
                         (3 * d_dil + q_lora + kv_lora) // LANES, cos, sin_signed, n_heads=H)

    h1 = _outproj(o_a.reshape(T, d_dil), o_b.reshape(T, -1), out_norm_dil, out_norm_mla,
                  w_o.astype(BF16), x2, tm=ROW_TILE)

    q_slabs, xn2 = _norm_matmul(h1, ln2_g, peer_wq.astype(BF16), tm=ROW_TILE, emit_xn=True,
                                slab_out=True, out_dtype=BF16)
    keys = peer_sub_keys.reshape(PEER_HEADS * 2, PEER_NKEYS, -1).astype(BF16)
    idx_t, gate_t = _peer_topk(q_slabs, keys)
    table = _pack_expert_table(peer_u, peer_v)
    gate_h = gate_t.reshape(GSIZE, NGROUP, T // GSIZE, GSIZE).transpose(2, 0, 3, 1).reshape(T // GSIZE, GSIZE, LANES)
    out = _peer_ffn(idx_t.T, gate_h, xn2, h1, final_g.reshape(1, D), table)
    return out.reshape(B, S, D)


def kernel(x, ln1_g, w_in, q_a_norm, kv_a_norm, w_uq, w_ukv, out_norm_dil, out_norm_mla, w_o,
           ln2_g, peer_wq, peer_sub_keys, peer_u, peer_v, lnf_g):
    assert ln1_g.shape[0] == 1, "single-layer trunk"
    return _layer(x, ln1_g[0], w_in[0], q_a_norm[0], kv_a_norm[0], w_uq[0], w_ukv[0],
                  out_norm_dil[0], out_norm_mla[0], w_o[0], ln2_g[0], peer_wq[0],
                  peer_sub_keys[0], peer_u[0], peer_v[0], lnf_g)
```

```python
import functools
import math

import jax
import jax.numpy as jnp
from jax import lax
from jax.experimental import pallas as pl
from jax.experimental.pallas import tpu as pltpu

F32 = jnp.float32
BF16 = jnp.bfloat16

EPS = 1e-6
NEG = -1e30
HEAD_DIM = 128
BLOCK = 128
DIL_PATTERNS = ((128, 1), (512, 4), (2048, 16))
DIL_UNROLL = 16
N_HEADS = 8
QK_NOPE = 128
QK_ROPE = 64
ROPE_THETA = 10000.0
PEER_HEADS = 8
PEER_NKEYS = 128
PEER_TOPK = 16
PEER_E = PEER_HEADS * PEER_TOPK

LANES = 128
SUBLANES = 8
VMEM_CAP = 60000 * 1024
ROW_TILE = 1024


def _vmem_limit(nbytes):
    return int(min(VMEM_CAP, max(16 * 1024 * 1024, nbytes * 3 // 2)))


def _rms(x, g):
    return x * lax.rsqrt(jnp.mean(x * x, axis=-1, keepdims=True) + EPS) * g


def _norm_matmul_body(x_ref, g_ref, w_ref, *rest, emit_xn, slabs):
    if emit_xn:
        o_ref, xn_out_ref, xn_ref = rest
    else:
        o_ref, xn_ref = rest
    j = pl.program_id(1)

    @pl.when(j == 0)
    def _():
        xn = _rms(x_ref[...].astype(F32), g_ref[...])
        xn_ref[...] = xn.astype(BF16)
        if emit_xn:
            xn_out_ref[...] = xn

    res = jnp.dot(xn_ref[...], w_ref[...], preferred_element_type=F32).astype(o_ref.dtype)
    if slabs:
        for s in range(slabs):
            o_ref[s] = res[:, s * LANES:(s + 1) * LANES]
    else:
        o_ref[...] = res


def _norm_matmul(x, g, w, *, xcol=0, kdim=None, tm=512, tn=512, emit_xn=False, slab_out=False,
                 out_dtype=F32):
    T = x.shape[0]
    kdim = kdim or x.shape[1]
    N = w.shape[1]
    tm = min(tm, T)
    tn = min(tn, N)
    assert T % tm == 0 and N % tn == 0 and w.shape[0] == kdim
    slabs = tn // LANES if slab_out else 0
    if slab_out:
        out_shape = [jax.ShapeDtypeStruct((N // LANES, T, LANES), out_dtype)]
        out_specs = [pl.BlockSpec((slabs, tm, LANES), lambda i, j: (j, i, 0))]
    else:
        out_shape = [jax.ShapeDtypeStruct((T, N), out_dtype)]
        out_specs = [pl.BlockSpec((tm, tn), lambda i, j: (i, j))]
    if emit_xn:
        out_shape.append(jax.ShapeDtypeStruct((T, kdim), F32))
        out_specs.append(pl.BlockSpec((tm, kdim), lambda i, j: (i, 0)))
    est = 2 * (tm * kdim * 4 + kdim * tn * 2 + tm * tn * 4) + tm * kdim * 2
    if emit_xn:
        est += 2 * tm * kdim * 4
    outs = pl.pallas_call(
        functools.partial(_norm_matmul_body, emit_xn=emit_xn, slabs=slabs),
        grid=(T // tm, N // tn),
        in_specs=[
            pl.BlockSpec((tm, kdim), lambda i, j: (i, xcol)),
            pl.BlockSpec((1, kdim), lambda i, j: (0, 0)),
            pl.BlockSpec((kdim, tn), lambda i, j: (0, j)),
        ],
        out_specs=out_specs,
        out_shape=out_shape,
        scratch_shapes=[pltpu.VMEM((tm, kdim), BF16)],
        compiler_params=pltpu.CompilerParams(
            dimension_semantics=("arbitrary", "arbitrary"),
            vmem_limit_bytes=_vmem_limit(est)),
        name="norm_matmul",
    )(x, g.reshape(1, kdim).astype(F32), w)
    return outs if emit_xn else outs[0]


def _dilated_body(slopes_ref, q_ref, k_ref, v_ref, o_ref, o_scr, l_scr, *, seq, patterns, scale):
    h = pl.program_id(1)
    slope = slopes_ref[h]
    qi = lax.broadcasted_iota(jnp.int32, (BLOCK, 2 * BLOCK), 0)
    kj = lax.broadcasted_iota(jnp.int32, (BLOCK, 2 * BLOCK), 1)
    delta = qi + BLOCK - kj
    nt = (((1,), (1,)), ((), ()))

    for p, (window, d) in enumerate(patterns):
        steps = window // d
        nb = seq // d // BLOCK
        in_window = (delta >= 0) & (delta <= steps)
        bias = -slope * (delta * d).astype(F32)

        def block(t, carry, d=d, nb=nb, in_window=in_window, bias=bias, p=p):
            r = t // nb
            n = t % nb
            start = n * (BLOCK * d) + r
            pstart = jnp.maximum(n - 1, 0) * (BLOCK * d) + r
            rows = pl.ds(start, BLOCK, stride=d) if d > 1 else pl.ds(start, BLOCK)
            prow = pl.ds(pstart, BLOCK, stride=d) if d > 1 else pl.ds(pstart, BLOCK)
            q = q_ref[0, rows, :].astype(BF16)
            kk = jnp.concatenate([k_ref[0, prow, :], k_ref[0, rows, :]], axis=0).astype(BF16)
            vv = jnp.concatenate([v_ref[0, prow, :], v_ref[0, rows, :]], axis=0).astype(BF16)
            s = lax.dot_general(q, kk, nt, preferred_element_type=F32) * scale
            valid = in_window & ((kj >= BLOCK) | (n > 0))
            s = jnp.where(valid, s + bias, NEG)
            m = jnp.max(s, axis=-1, keepdims=True)
            e = jnp.exp(s - m)
            l = jnp.sum(e, axis=-1, keepdims=True)
            o = jnp.dot((e / l).astype(BF16), vv, preferred_element_type=F32)
            o_scr[p, rows, :] = o
            l_scr[p, rows, :] = jnp.broadcast_to(m + jnp.log(l), (BLOCK, HEAD_DIM))
            return carry

        lax.fori_loop(0, d * nb, block, 0, unroll=DIL_UNROLL)

    def mix(c, carry):
        rows = pl.ds(pl.multiple_of(c * BLOCK, BLOCK), BLOCK)
        ls = [l_scr[p, rows, :] for p in range(len(patterns))]
        m = functools.reduce(jnp.maximum, ls)
        es = [jnp.exp(l - m) for l in ls]
        den = functools.reduce(jnp.add, es)
        num = functools.reduce(jnp.add, [e * o_scr[p, rows, :] for p, e in enumerate(es)])
        o_ref[0, rows, :] = (num / den).astype(o_ref.dtype)
        return carry

    lax.fori_loop(0, seq // BLOCK, mix, 0)


def _dilated_attention(z3, slopes, *, n_heads, patterns=DIL_PATTERNS):
    B, S, _ = z3.shape
    for _, d in patterns:
        assert S % (d * BLOCK) == 0
    blk = (1, S, HEAD_DIM)
    est = 2 * 4 * S * HEAD_DIM * 4 + 2 * len(patterns) * S * HEAD_DIM * 4
    return pl.pallas_call(
        functools.partial(_dilated_body, seq=S, patterns=patterns, scale=HEAD_DIM ** -0.5),
        grid=(B, n_heads),
        in_specs=[
            pl.BlockSpec(memory_space=pltpu.SMEM),
            pl.BlockSpec(blk, lambda b, h: (b, 0, h)),
            pl.BlockSpec(blk, lambda b, h: (b, 0, n_heads + h)),
            pl.BlockSpec(blk, lambda b, h: (b, 0, 2 * n_heads + h)),
        ],
        out_specs=pl.BlockSpec(blk, lambda b, h: (b, 0, h)),
        out_shape=jax.ShapeDtypeStruct((B, S, n_heads * HEAD_DIM), BF16),
        scratch_shapes=[pltpu.VMEM((len(patterns), S, HEAD_DIM), F32),
                        pltpu.VMEM((len(patterns), S, HEAD_DIM), F32)],
        compiler_params=pltpu.CompilerParams(
            dimension_semantics=("arbitrary", "arbitrary"),
            vmem_limit_bytes=_vmem_limit(est)),
        name="dilated_attention",
    )(slopes, z3, z3, z3)


def _rope(x, cos, sin_signed):
    lane = lax.broadcasted_iota(jnp.int32, x.shape, 1)
    half = QK_ROPE // 2
    swapped = jnp.where(lane < half, pltpu.roll(x, LANES - half, 1), pltpu.roll(x, half, 1))
    return x * cos + swapped * sin_signed


def _mla_body(qn_ref, qp_ref, kn_ref, v_ref, kr_ref, cos_ref, sin_ref,
              o_ref, qcat_scr, kcat_scr, *, seq, tq, scale):
    cos, sin = cos_ref[...], sin_ref[...]
    qcat_scr[:, :LANES] = qn_ref[0]
    qcat_scr[:, LANES:] = _rope(qp_ref[0].astype(F32), cos, sin).astype(BF16)
    kcat_scr[:, :LANES] = kn_ref[0]
    kcat_scr[:, LANES:] = _rope(kr_ref[0], cos, sin).astype(BF16)
    nt = (((1,), (1,)), ((), ()))

    for qi in range(seq // tq):
        ext = (qi + 1) * tq
        rows = slice(qi * tq, ext)
        s = lax.dot_general(qcat_scr[rows, :], kcat_scr[:ext, :], nt, preferred_element_type=F32) * scale
        qpos = qi * tq + lax.broadcasted_iota(jnp.int32, (tq, ext), 0)
        kpos = lax.broadcasted_iota(jnp.int32, (tq, ext), 1)
        s = jnp.where(kpos <= qpos, s, NEG)
        e = jnp.exp(s - jnp.max(s, axis=-1, keepdims=True))
        l = jnp.sum(e, axis=-1, keepdims=True)
        o = jnp.dot(e.astype(BF16), v_ref[0, :ext, :], preferred_element_type=F32) / l
        o_ref[0, rows, :] = o.astype(o_ref.dtype)


def _mla_attention(q3, kv3, z3, kr_col, cos, sin_signed, *, n_heads, tq=512):
    B, S, _ = q3.shape
    tq = min(tq, S)
    assert S % tq == 0 and q3.dtype == BF16 and kv3.dtype == BF16
    blk = (1, S, LANES)
    est = 2 * (5 * 2 + 3 * 4) * S * LANES + 4 * S * LANES * 2 + 3 * tq * S * 4
    return pl.pallas_call(
        functools.partial(_mla_body, seq=S, tq=tq, scale=(QK_NOPE + QK_ROPE) ** -0.5),
        grid=(B, n_heads),
        in_specs=[
            pl.BlockSpec(blk, lambda b, h: (b, 0, h)),
            pl.BlockSpec(blk, lambda b, h: (b, 0, n_heads + h)),
            pl.BlockSpec(blk, lambda b, h: (b, 0, h)),
            pl.BlockSpec(blk, lambda b, h: (b, 0, n_heads + h)),
            pl.BlockSpec(blk, lambda b, h: (b, 0, kr_col)),
            pl.BlockSpec((S, LANES), lambda b, h: (0, 0)),
            pl.BlockSpec((S, LANES), lambda b, h: (0, 0)),
        ],
        out_specs=pl.BlockSpec(blk, lambda b, h: (b, 0, h)),
        out_shape=jax.ShapeDtypeStruct((B, S, n_heads * HEAD_DIM), BF16),
        scratch_shapes=[pltpu.VMEM((S, 2 * LANES), BF16), pltpu.VMEM((S, 2 * LANES), BF16)],
        compiler_params=pltpu.CompilerParams(
            dimension_semantics=("arbitrary", "arbitrary"),
            vmem_limit_bytes=_vmem_limit(est)),
        name="mla_attention",
    )(q3, q3, kv3, kv3, z3, cos, sin_signed)


def _outproj_body(oa_ref, ob_ref, ga_ref, gb_ref, w_ref, x_ref, o_ref, xn_ref, *, da):
    j = pl.program_id(1)

    @pl.when(j == 0)
    def _():
        xn_ref[:, :da] = _rms(oa_ref[...].astype(F32), ga_ref[...]).astype(BF16)
        xn_ref[:, da:] = _rms(ob_ref[...].astype(F32), gb_ref[...]).astype(BF16)

    o_ref[...] = x_ref[...] + jnp.dot(xn_ref[...], w_ref[...], preferred_element_type=F32)


def _outproj(oa, ob, ga, gb, w, x, *, tm=512, tn=512):
    T, da = oa.shape
    db = ob.shape[1]
    N = w.shape[1]
    tm, tn = min(tm, T), min(tn, N)
    assert T % tm == 0 and N % tn == 0 and w.shape[0] == da + db
    est = 2 * (tm * (da + db) * 4 + (da + db) * tn * 2 + 2 * tm * tn * 4) + tm * (da + db) * 2
    return pl.pallas_call(
        functools.partial(_outproj_body, da=da),
        grid=(T // tm, N // tn),
        in_specs=[
            pl.BlockSpec((tm, da), lambda i, j: (i, 0)),
            pl.BlockSpec((tm, db), lambda i, j: (i, 0)),
            pl.BlockSpec((1, da), lambda i, j: (0, 0)),
            pl.BlockSpec((1, db), lambda i, j: (0, 0)),
            pl.BlockSpec((da + db, tn), lambda i, j: (0, j)),
            pl.BlockSpec((tm, tn), lambda i, j: (i, j)),
        ],
        out_specs=pl.BlockSpec((tm, tn), lambda i, j: (i, j)),
        out_shape=jax.ShapeDtypeStruct((T, N), F32),
        scratch_shapes=[pltpu.VMEM((tm, da + db), BF16)],
        compiler_params=pltpu.CompilerParams(
            dimension_semantics=("arbitrary", "arbitrary"),
            vmem_limit_bytes=_vmem_limit(est)),
        name="out_projection",
    )(oa, ob, ga.reshape(1, da), gb.reshape(1, db), w, x)


TOPK_HEADS = 2


def _candidate_blocks(topk):
    blocks = []
    a = 0
    while a < topk:
        nb = topk // (a + 1)
        if nb >= SUBLANES:
            blocks += [(a, 1, b0, min(SUBLANES, nb - b0)) for b0 in range(0, nb, SUBLANES)]
            a += 1
        elif nb > 1:
            blocks.append((a, 1, 0, nb))
            a += 1
        else:
            na = min(SUBLANES, topk - a)
            blocks.append((a, na, 0, 1))
            a += na
    return blocks


def _topk_body(q_ref, keys_ref, idx_ref, gate_ref, s_ref, sv_ref, si_ref, cs_ref, ci_ref, bs_ref, be_ref,
               *, tt, nkeys, topk, blocks):
    lowest = float(jnp.finfo(jnp.float32).min)
    nchain = 2 * TOPK_HEADS
    iota_n = lax.broadcasted_iota(jnp.int32, (nkeys, tt), 0)
    sub = lax.broadcasted_iota(jnp.int32, (SUBLANES, tt), 0)
    nt = (((1,), (1,)), ((), ()))

    for ch in range(nchain):
        s_ref[ch] = lax.dot_general(keys_ref[ch], q_ref[ch], nt, preferred_element_type=F32)

    def pick(k, carry):
        for ch in range(nchain):
            s = s_ref[ch]
            m = jnp.max(s, axis=0, keepdims=True)
            ix = jnp.min(jnp.where(s == m, iota_n, nkeys), axis=0, keepdims=True)
            sv_ref[ch, pl.ds(k, 1), :] = m
            si_ref[ch, pl.ds(k, 1), :] = ix
            s_ref[ch] = jnp.where(iota_n == ix, lowest, s)
        return carry

    lax.fori_loop(0, topk, pick, 0)

    pos_blocks = []
    for r, (a0, na, b0, nb) in enumerate(blocks):
        rows = slice(r * SUBLANES, (r + 1) * SUBLANES)
        if na == 1:
            live = sub < nb
            pos_blocks.append(a0 * topk + b0 + sub)
        else:
            live = sub < na
            pos_blocks.append((a0 + sub) * topk)
        for hd in range(TOPK_HEADS):
            if na == 1:
                cs = sv_ref[2 * hd, a0:a0 + 1, :] + sv_ref[2 * hd + 1, b0:b0 + SUBLANES, :]
                ci = si_ref[2 * hd, a0:a0 + 1, :] * nkeys + si_ref[2 * hd + 1, b0:b0 + SUBLANES, :]
            else:
                cs = sv_ref[2 * hd, a0:a0 + SUBLANES, :] + sv_ref[2 * hd + 1, 0:1, :]
                ci = si_ref[2 * hd, a0:a0 + SUBLANES, :] * nkeys + si_ref[2 * hd + 1, 0:1, :]
            cs_ref[hd, rows, :] = jnp.where(live, cs, lowest)
            ci_ref[hd, rows, :] = ci
    pos = jnp.concatenate(pos_blocks, axis=0)
    big = topk * topk

    def pick2(k, carry):
        for hd in range(TOPK_HEADS):
            c = cs_ref[hd]
            m = jnp.max(c, axis=0, keepdims=True)
            first = jnp.min(jnp.where(c == m, pos, big), axis=0, keepdims=True)
            hit = pos == first
            bs_ref[hd, pl.ds(k, 1), :] = m
            be_ref[hd, pl.ds(k, 1), :] = jnp.max(jnp.where(hit, ci_ref[hd], -1), axis=0, keepdims=True)
            cs_ref[hd] = jnp.where(hit, lowest, c)
        return carry

    lax.fori_loop(0, topk, pick2, 0)

    for hd in range(TOPK_HEADS):
        b = bs_ref[hd]
        e = jnp.exp(b - jnp.max(b, axis=0, keepdims=True))
        gate_ref[hd * topk:(hd + 1) * topk, :] = e / jnp.sum(e, axis=0, keepdims=True)
        idx_ref[hd * topk:(hd + 1) * topk, :] = be_ref[hd]


def _peer_topk(q_slabs, keys, *, tt=128):
    hp, T, c = q_slabs.shape
    heads = hp // 2
    nkeys = keys.shape[1]
    tt = min(tt, T)
    topk = PEER_TOPK
    assert T % tt == 0 and heads % TOPK_HEADS == 0 and topk % SUBLANES == 0
    blocks = _candidate_blocks(topk)
    ncand = len(blocks) * SUBLANES
    nchain = 2 * TOPK_HEADS
    rows = TOPK_HEADS * topk
    return pl.pallas_call(
        functools.partial(_topk_body, tt=tt, nkeys=nkeys, topk=topk, blocks=blocks),
        grid=(T // tt, heads // TOPK_HEADS),
        in_specs=[
            pl.BlockSpec((nchain, tt, c), lambda i, h: (h, i, 0)),
            pl.BlockSpec((nchain, nkeys, c), lambda i, h: (h, 0, 0)),
        ],
        out_specs=[pl.BlockSpec((rows, tt), lambda i, h: (h, i)),
                   pl.BlockSpec((rows, tt), lambda i, h: (h, i))],
        out_shape=[jax.ShapeDtypeStruct((heads * topk, T), jnp.int32),
                   jax.ShapeDtypeStruct((heads * topk, T), F32)],
        scratch_shapes=[pltpu.VMEM((nchain, nkeys, tt), F32),
                        pltpu.VMEM((nchain, topk, tt), F32), pltpu.VMEM((nchain, topk, tt), jnp.int32),
                        pltpu.VMEM((TOPK_HEADS, ncand, tt), F32), pltpu.VMEM((TOPK_HEADS, ncand, tt), jnp.int32),
                        pltpu.VMEM((TOPK_HEADS, topk, tt), F32), pltpu.VMEM((TOPK_HEADS, topk, tt), jnp.int32)],
        compiler_params=pltpu.CompilerParams(dimension_semantics=("arbitrary", "arbitrary")),
        name="peer_topk",
    )(q_slabs, keys)


GSIZE = SUBLANES
NGROUP = PEER_E // GSIZE
ROW_PITCH = 17


def _gelu_exact(x):
    return 0.5 * x * (1.0 + lax.erf(x * (2.0 ** -0.5)))


def _sublane_sums(vs, sub):
    lo = sub < 4
    halves = []
    for j in range(4):
        a, b = vs[j], vs[j + 4]
        halves.append(jnp.where(lo, a, b) + pltpu.roll(jnp.where(lo, b, a), 4, 0))
    done = []
    for j, (sa, sb) in enumerate(((6, 7), (6, 1), (2, 7), (2, 1))):
        c = halves[j] + pltpu.roll(halves[j], sa, 0)
        done.append(c + pltpu.roll(c, sb, 0))
    q = sub & 3
    return jnp.where(q == 0, done[0], jnp.where(q == 1, done[1], jnp.where(q == 2, done[2], done[3])))


def _peer_ffn_body(idxc_ref, idxn_ref, gate_ref, x_ref, h_ref, gf_ref, tab_ref, o_ref,
                   buf, sem, *, ctok, nrow):
    i = pl.program_id(0)
    n = pl.num_programs(0)
    sub = lax.broadcasted_iota(jnp.int32, (SUBLANES, LANES), 0)
    lane = lax.broadcasted_iota(jnp.int32, (SUBLANES, LANES), 1)
    hi_mask = jnp.uint32(0xFFFF0000)

    def start_row(idx_ref, row, dst_slot, c, j):
        k, g = divmod(j, NGROUP)
        first_row = ((c * NGROUP + g) * GSIZE + k) * ROW_PITCH
        pltpu.make_async_copy(tab_ref.at[idx_ref[row, j]], buf.at[dst_slot, pl.ds(first_row, nrow), :],
                              sem.at[dst_slot]).start(priority=j % 2)

    def wait_slot(s):
        copied = pl.ds(0, ctok * PEER_E * nrow)
        pltpu.make_async_copy(buf.at[s, copied, :], buf.at[s, copied, :], sem.at[s]).wait()

    def group_row(half, c, g, s):
        return buf[half, pl.ds((c * NGROUP + g) * GSIZE * ROW_PITCH + s, GSIZE, stride=ROW_PITCH), :]

    @pl.when(i == 0)
    def _():
        def first(c, carry):
            for j in range(PEER_E):
                start_row(idxc_ref, c, 0, c, j)
            return carry
        lax.fori_loop(0, ctok, first, 0)

    for half in range(2):
        wait_slot(half)
        gate_h = gate_ref[half]
        yaccs = []
        for c in range(ctok):
            tok = half * ctok + c
            if half == 0:
                nxt = lambda j, c=c: start_row(idxc_ref, ctok + c, 1, c, j)
            else:
                nxt = lambda j, c=c: start_row(idxn_ref, c, 0, c, j)
            xb = [jnp.broadcast_to(x_ref[tok:tok + 1, s * LANES:(s + 1) * LANES], (SUBLANES, LANES))
                  for s in range(nrow)]
            dots = jnp.zeros((SUBLANES, LANES), F32)
            for g in range(NGROUP):
                for j in range(g * 4, g * 4 + 4):
                    nxt(j)
                r = None
                for s in range(nrow):
                    w = group_row(half, c, g, s)
                    p = pltpu.bitcast(w << 16, F32) * xb[s]
                    r = p if r is None else r + p
                d = jnp.sum(r, axis=1, keepdims=True)
                dots = jnp.where(lane == c * NGROUP + g, d, dots)
            act = _gelu_exact(dots) * gate_h
            yacc = [None] * nrow
            for g in range(NGROUP):
                for j in range(PEER_E // 2 + g * 4, PEER_E // 2 + g * 4 + 4):
                    nxt(j)
                col = c * NGROUP + g
                a = jnp.broadcast_to(act[:, col:col + 1], (SUBLANES, LANES))
                for s in range(nrow):
                    w = group_row(half, c, g, s)
                    t = a * pltpu.bitcast(w & hi_mask, F32)
                    yacc[s] = t if yacc[s] is None else yacc[s] + t
            yaccs.append(yacc)
        toks = slice(half * ctok, (half + 1) * ctok)
        zs = []
        ss = jnp.zeros((SUBLANES, LANES), F32)
        for s in range(nrow):
            cols = slice(s * LANES, (s + 1) * LANES)
            z = h_ref[toks, cols] + _sublane_sums([yaccs[c][s] for c in range(ctok)], sub)
            ss = ss + z * z
            zs.append(z)
        inv = lax.rsqrt(jnp.sum(ss, axis=1, keepdims=True) * (1.0 / (nrow * LANES)) + EPS)
        for s in range(nrow):
            cols = slice(s * LANES, (s + 1) * LANES)
            o_ref[toks, cols] = zs[s] * inv * gf_ref[:, cols]

    @pl.when(i == n - 1)
    def _():
        wait_slot(0)


def _peer_ffn(idx, gate_h, xn, h, gf, table, *, ctok=8):
    T, nexp = idx.shape
    D = xn.shape[1]
    nrow = D // LANES
    step = 2 * ctok
    assert nexp == PEER_E and ctok == SUBLANES and T % step == 0 and table.shape[1:] == (nrow, LANES)
    n = T // step
    last = n - 1
    assert ROW_PITCH > nrow
    est = 2 * ctok * nexp * ROW_PITCH * LANES * 4 + 6 * step * D * 4
    return pl.pallas_call(
        functools.partial(_peer_ffn_body, ctok=ctok, nrow=nrow),
        grid=(n,),
        in_specs=[
            pl.BlockSpec((step, nexp), lambda i: (i, 0), memory_space=pltpu.SMEM),
            pl.BlockSpec((step, nexp), lambda i: (jnp.minimum(i + 1, last), 0), memory_space=pltpu.SMEM),
            pl.BlockSpec((2, SUBLANES, LANES), lambda i: (i, 0, 0)),
            pl.BlockSpec((step, D), lambda i: (i, 0)),
            pl.BlockSpec((step, D), lambda i: (i, 0)),
            pl.BlockSpec((1, D), lambda i: (0, 0)),
            pl.BlockSpec(memory_space=pl.ANY),
        ],
        out_specs=pl.BlockSpec((step, D), lambda i: (i, 0)),
        out_shape=jax.ShapeDtypeStruct((T, D), F32),
        scratch_shapes=[pltpu.VMEM((2, ctok * nexp * ROW_PITCH, LANES), jnp.uint32),
                        pltpu.SemaphoreType.DMA((2,))],
        compiler_params=pltpu.CompilerParams(
            dimension_semantics=("arbitrary",),
            vmem_limit_bytes=_vmem_limit(est)),
        name="peer_ffn",
    )(idx, idx, gate_h, xn, h, gf, table)


def _pack_expert_table(u, v):
    ub = lax.bitcast_convert_type(u.astype(BF16), jnp.uint16).astype(jnp.uint32)
    vb = lax.bitcast_convert_type(v.astype(BF16), jnp.uint16).astype(jnp.uint32)
    n, d = u.shape
    return (ub | (vb << 16)).reshape(n, d // LANES, LANES)


def _rope_tables(seq):
    half = QK_ROPE // 2
    freqs = ROPE_THETA ** (-jnp.arange(half, dtype=F32) / half)
    ang = jnp.arange(seq, dtype=F32)[:, None] * freqs[None, :]
    cos, sin = jnp.cos(ang), jnp.sin(ang)
    zeros = jnp.zeros((seq, LANES - QK_ROPE), F32)
    return (jnp.concatenate([cos, cos, zeros], axis=1),
            jnp.concatenate([-sin, sin, zeros], axis=1))


def _layer(h, ln1_g, w_in, q_a_norm, kv_a_norm, w_uq, w_ukv, out_norm_dil, out_norm_mla, w_o,
           ln2_g, peer_wq, peer_sub_keys, peer_u, peer_v, final_g):
    B, S, D = h.shape
    T = B * S
    H = N_HEADS
    d_dil = H * HEAD_DIM
    q_lora = q_a_norm.shape[0]
    kv_lora = kv_a_norm.shape[0]
    x2 = h.reshape(T, D)

    d_in = w_in.shape[1]
    d_in_pad = -(-d_in // LANES) * LANES
    w_in_p = jnp.pad(w_in, ((0, 0), (0, d_in_pad - d_in))).astype(BF16)
    z = _norm_matmul(x2, ln1_g, w_in_p, tm=ROW_TILE,
                     tn=d_in_pad // 3 if d_in_pad % (3 * LANES) == 0 else LANES)
    z3 = z.reshape(B, S, d_in_pad)

    slopes = 2.0 ** (-8.0 * jnp.arange(1, H + 1, dtype=F32) / H)
    o_a = _dilated_attention(z3, slopes, n_heads=H)

    wq = w_uq.reshape(q_lora, H, QK_NOPE + QK_ROPE)
    wq_pe = jnp.pad(wq[:, :, QK_NOPE:], ((0, 0), (0, 0), (0, LANES - QK_ROPE)))
    wq_p = jnp.concatenate([wq[:, :, :QK_NOPE].reshape(q_lora, H * QK_NOPE),
                            wq_pe.reshape(q_lora, H * LANES)], axis=1).astype(BF16)
    wkv = w_ukv.reshape(kv_lora, H, QK_NOPE + HEAD_DIM)
    wkv_p = jnp.concatenate([wkv[:, :, :QK_NOPE].reshape(kv_lora, H * QK_NOPE),
                             wkv[:, :, QK_NOPE:].reshape(kv_lora, H * HEAD_DIM)], axis=1).astype(BF16)
    assert q_lora == kv_lora and (3 * d_dil) % q_lora == 0
    q_mla = _norm_matmul(z, q_a_norm, wq_p, xcol=3 * d_dil // q_lora, kdim=q_lora, out_dtype=BF16)
    kv_mla = _norm_matmul(z, kv_a_norm, wkv_p, xcol=3 * d_dil // q_lora + 1, kdim=kv_lora, out_dtype=BF16)
    cos, sin_signed = _rope_tables(S)
    o_b = _mla_attention(q_mla.reshape(B, S, -1), kv_mla.reshape(B, S, -1), z3,
                         (3 * d_dil + q_lora + kv_lora) // LANES, cos, sin_signed, n_heads=H)

    h1 = _outproj(o_a.reshape(T, d_dil), o_b.reshape(T, -1), out_norm_dil, out_norm_mla,
                  w_o.astype(BF16), x2, tm=ROW_TILE)

    q_slabs, xn2 = _norm_matmul(h1, ln2_g, peer_wq.astype(BF16), tm=ROW_TILE, emit_xn=True,
                                slab_out=True, out_dtype=BF16)
    keys = peer_sub_keys.reshape(PEER_HEADS * 2, PEER_NKEYS, -1).astype(BF16)
    idx_t, gate_t = _peer_topk(q_slabs, keys)
    table = _pack_expert_table(peer_u, peer_v)
    gate_h = gate_t.reshape(GSIZE, NGROUP, T // GSIZE, GSIZE).transpose(2, 0, 3, 1).reshape(T // GSIZE, GSIZE, LANES)
    out = _peer_ffn(idx_t.T, gate_h, xn2, h1, final_g.reshape(1, D), table)
    return out.reshape(B, S, D)


def kernel(x, ln1_g, w_in, q_a_norm, kv_a_norm, w_uq, w_ukv, out_norm_dil, out_norm_mla, w_o,
           ln2_g, peer_wq, peer_sub_keys, peer_u, peer_v, lnf_g):
    assert ln1_g.shape[0] == 1, "single-layer trunk"
    return _layer(x, ln1_g[0], w_in[0], q_a_norm[0], kv_a_norm[0], w_uq[0], w_ukv[0],
                  out_norm_dil[0], out_norm_mla[0], w_o[0], ln2_g[0], peer_wq[0],
                  peer_sub_keys[0], peer_u[0], peer_v[0], lnf_g)
```

```python
import functools

import jax
import jax.numpy as jnp
from jax import lax
from jax.experimental import pallas as pl
from jax.experimental.pallas import tpu as pltpu

F32 = jnp.float32
BF16 = jnp.bfloat16

EPS = 1e-6
NEG = -1e30
HEAD_DIM = 128
BLOCK = 128
DIL_PATTERNS = ((128, 1), (512, 4), (2048, 16))
DIL_UNROLL = 16
N_HEADS = 8
QK_NOPE = 128
QK_ROPE = 64
ROPE_THETA = 10000.0
PEER_HEADS = 8
PEER_NKEYS = 128
PEER_TOPK = 16
PEER_E = PEER_HEADS * PEER_TOPK

LANES = 128
SUBLANES = 8
VMEM_CAP = 60000 * 1024
ROW_TILE = 1024


def _vmem_limit(nbytes):
    return int(min(VMEM_CAP, max(16 * 1024 * 1024, nbytes * 3 // 2)))


def _rms(x, g):
    return x * lax.rsqrt(jnp.mean(x * x, axis=-1, keepdims=True) + EPS) * g


def _norm_matmul_body(x_ref, g_ref, w_ref, *rest, emit_xn, slabs):
    if emit_xn:
        o_ref, xn_out_ref, xn_ref = rest
    else:
        o_ref, xn_ref = rest
    j = pl.program_id(1)

    @pl.when(j == 0)
    def _():
        xn = _rms(x_ref[...].astype(F32), g_ref[...])
        xn_ref[...] = xn.astype(BF16)
        if emit_xn:
            xn_out_ref[...] = xn

    res = jnp.dot(xn_ref[...], w_ref[...], preferred_element_type=F32).astype(o_ref.dtype)
    if slabs:
        for s in range(slabs):
            o_ref[s] = res[:, s * LANES:(s + 1) * LANES]
    else:
        o_ref[...] = res


def _norm_matmul(x, g, w, *, xcol=0, kdim=None, tm=512, tn=512, emit_xn=False, slab_out=False,
                 out_dtype=F32):
    T = x.shape[0]
    kdim = kdim or x.shape[1]
    N = w.shape[1]
    tm = min(tm, T)
    tn = min(tn, N)
    assert T % tm == 0 and N % tn == 0 and w.shape[0] == kdim
    slabs = tn // LANES if slab_out else 0
    if slab_out:
        out_shape = [jax.ShapeDtypeStruct((N // LANES, T, LANES), out_dtype)]
        out_specs = [pl.BlockSpec((slabs, tm, LANES), lambda i, j: (j, i, 0))]
    else:
        out_shape = [jax.ShapeDtypeStruct((T, N), out_dtype)]
        out_specs = [pl.BlockSpec((tm, tn), lambda i, j: (i, j))]
    if emit_xn:
        out_shape.append(jax.ShapeDtypeStruct((T, kdim), F32))
        out_specs.append(pl.BlockSpec((tm, kdim), lambda i, j: (i, 0)))
    est = 2 * (tm * kdim * 4 + kdim * tn * 2 + tm * tn * 4) + tm * kdim * 2
    if emit_xn:
        est += 2 * tm * kdim * 4
    outs = pl.pallas_call(
        functools.partial(_norm_matmul_body, emit_xn=emit_xn, slabs=slabs),
        grid=(T // tm, N // tn),
        in_specs=[
            pl.BlockSpec((tm, kdim), lambda i, j: (i, xcol)),
            pl.BlockSpec((1, kdim), lambda i, j: (0, 0)),
            pl.BlockSpec((kdim, tn), lambda i, j: (0, j)),
        ],
        out_specs=out_specs,
        out_shape=out_shape,
        scratch_shapes=[pltpu.VMEM((tm, kdim), BF16)],
        compiler_params=pltpu.CompilerParams(
            dimension_semantics=("arbitrary", "arbitrary"),
            vmem_limit_bytes=_vmem_limit(est)),
        name="norm_matmul",
    )(x, g.reshape(1, kdim).astype(F32), w)
    return outs if emit_xn else outs[0]


def _dilated_body(slopes_ref, q_ref, k_ref, v_ref, o_ref, o_scr, l_scr, *, seq, patterns, scale):
    h = pl.program_id(1)
    slope = slopes_ref[h]
    qi = lax.broadcasted_iota(jnp.int32, (BLOCK, 2 * BLOCK), 0)
    kj = lax.broadcasted_iota(jnp.int32, (BLOCK, 2 * BLOCK), 1)
    delta = qi + BLOCK - kj
    nt = (((1,), (1,)), ((), ()))

    for p, (window, d) in enumerate(patterns):
        steps = window // d
        nb = seq // d // BLOCK
        in_window = (delta >= 0) & (delta <= steps)
        bias = -slope * (delta * d).astype(F32)

        def block(t, carry, d=d, nb=nb, in_window=in_window, bias=bias, p=p):
            r = t // nb
            n = t % nb
            start = n * (BLOCK * d) + r
            pstart = jnp.maximum(n - 1, 0) * (BLOCK * d) + r
            rows = pl.ds(start, BLOCK, stride=d) if d > 1 else pl.ds(start, BLOCK)
            prow = pl.ds(pstart, BLOCK, stride=d) if d > 1 else pl.ds(pstart, BLOCK)
            q = q_ref[0, rows, :].astype(BF16)
            kk = jnp.concatenate([k_ref[0, prow, :], k_ref[0, rows, :]], axis=0).astype(BF16)
            vv = jnp.concatenate([v_ref[0, prow, :], v_ref[0, rows, :]], axis=0).astype(BF16)
            s = lax.dot_general(q, kk, nt, preferred_element_type=F32) * scale
            valid = in_window & ((kj >= BLOCK) | (n > 0))
            s = jnp.where(valid, s + bias, NEG)
            m = jnp.max(s, axis=-1, keepdims=True)
            e = jnp.exp(s - m)
            l = jnp.sum(e, axis=-1, keepdims=True)
            o = jnp.dot((e / l).astype(BF16), vv, preferred_element_type=F32)
            o_scr[p, rows, :] = o
            l_scr[p, rows, :] = jnp.broadcast_to(m + jnp.log(l), (BLOCK, HEAD_DIM))
            return carry

        lax.fori_loop(0, d * nb, block, 0, unroll=DIL_UNROLL)

    def mix(c, carry):
        rows = pl.ds(pl.multiple_of(c * BLOCK, BLOCK), BLOCK)
        ls = [l_scr[p, rows, :] for p in range(len(patterns))]
        m = functools.reduce(jnp.maximum, ls)
        es = [jnp.exp(l - m) for l in ls]
        den = functools.reduce(jnp.add, es)
        num = functools.reduce(jnp.add, [e * o_scr[p, rows, :] for p, e in enumerate(es)])
        o_ref[0, rows, :] = (num / den).astype(o_ref.dtype)
        return carry

    lax.fori_loop(0, seq // BLOCK, mix, 0)


def _dilated_attention(z3, slopes, *, n_heads, patterns=DIL_PATTERNS):
    B, S, _ = z3.shape
    for _, d in patterns:
        assert S % (d * BLOCK) == 0
    blk = (1, S, HEAD_DIM)
    est = 2 * 4 * S * HEAD_DIM * 4 + 2 * len(patterns) * S * HEAD_DIM * 4
    return pl.pallas_call(
        functools.partial(_dilated_body, seq=S, patterns=patterns, scale=HEAD_DIM ** -0.5),
        grid=(B, n_heads),
        in_specs=[
            pl.BlockSpec(memory_space=pltpu.SMEM),
            pl.BlockSpec(blk, lambda b, h: (b, 0, h)),
            pl.BlockSpec(blk, lambda b, h: (b, 0, n_heads + h)),
            pl.BlockSpec(blk, lambda b, h: (b, 0, 2 * n_heads + h)),
        ],
        out_specs=pl.BlockSpec(blk, lambda b, h: (b, 0, h)),
        out_shape=jax.ShapeDtypeStruct((B, S, n_heads * HEAD_DIM), BF16),
        scratch_shapes=[pltpu.VMEM((len(patterns), S, HEAD_DIM), F32),
                        pltpu.VMEM((len(patterns), S, HEAD_DIM), F32)],
        compiler_params=pltpu.CompilerParams(
            dimension_semantics=("arbitrary", "arbitrary"),
            vmem_limit_bytes=_vmem_limit(est)),
        name="dilated_attention",
    )(slopes, z3, z3, z3)


def _rope(x, cos, sin_signed):
    lane = lax.broadcasted_iota(jnp.int32, x.shape, 1)
    half = QK_ROPE // 2
    swapped = jnp.where(lane < half, pltpu.roll(x, LANES - half, 1), pltpu.roll(x, half, 1))
    return x * cos + swapped * sin_signed


def _mla_body(qn_ref, qp_ref, kn_ref, v_ref, kr_ref, cos_ref, sin_ref,
              o_ref, qcat_scr, kcat_scr, *, seq, tq, scale):
    cos, sin = cos_ref[...], sin_ref[...]
    qcat_scr[:, :LANES] = qn_ref[0]
    qcat_scr[:, LANES:] = _rope(qp_ref[0].astype(F32), cos, sin).astype(BF16)
    kcat_scr[:, :LANES] = kn_ref[0]
    kcat_scr[:, LANES:] = _rope(kr_ref[0], cos, sin).astype(BF16)
    nt = (((1,), (1,)), ((), ()))

    for qi in range(seq // tq):
        ext = (qi + 1) * tq
        rows = slice(qi * tq, ext)
        s = lax.dot_general(qcat_scr[rows, :], kcat_scr[:ext, :], nt, preferred_element_type=F32) * scale
        qpos = qi * tq + lax.broadcasted_iota(jnp.int32, (tq, ext), 0)
        kpos = lax.broadcasted_iota(jnp.int32, (tq, ext), 1)
        s = jnp.where(kpos <= qpos, s, NEG)
        e = jnp.exp(s - jnp.max(s, axis=-1, keepdims=True))
        l = jnp.sum(e, axis=-1, keepdims=True)
        o = jnp.dot(e.astype(BF16), v_ref[0, :ext, :], preferred_element_type=F32) / l
        o_ref[0, rows, :] = o.astype(o_ref.dtype)


def _mla_attention(q3, kv3, z3, kr_col, cos, sin_signed, *, n_heads, tq=512):
    B, S, _ = q3.shape
    tq = min(tq, S)
    assert S % tq == 0 and q3.dtype == BF16 and kv3.dtype == BF16
    blk = (1, S, LANES)
    est = 2 * (5 * 2 + 3 * 4) * S * LANES + 4 * S * LANES * 2 + 3 * tq * S * 4
    return pl.pallas_call(
        functools.partial(_mla_body, seq=S, tq=tq, scale=(QK_NOPE + QK_ROPE) ** -0.5),
        grid=(B, n_heads),
        in_specs=[
            pl.BlockSpec(blk, lambda b, h: (b, 0, h)),
            pl.BlockSpec(blk, lambda b, h: (b, 0, n_heads + h)),
            pl.BlockSpec(blk, lambda b, h: (b, 0, h)),
            pl.BlockSpec(blk, lambda b, h: (b, 0, n_heads + h)),
            pl.BlockSpec(blk, lambda b, h: (b, 0, kr_col)),
            pl.BlockSpec((S, LANES), lambda b, h: (0, 0)),
            pl.BlockSpec((S, LANES), lambda b, h: (0, 0)),
        ],
        out_specs=pl.BlockSpec(blk, lambda b, h: (b, 0, h)),
        out_shape=jax.ShapeDtypeStruct((B, S, n_heads * HEAD_DIM), BF16),
        scratch_shapes=[pltpu.VMEM((S, 2 * LANES), BF16), pltpu.VMEM((S, 2 * LANES), BF16)],
        compiler_params=pltpu.CompilerParams(
            dimension_semantics=("arbitrary", "arbitrary"),
            vmem_limit_bytes=_vmem_limit(est)),
        name="mla_attention",
    )(q3, q3, kv3, kv3, z3, cos, sin_signed)


def _outproj_body(oa_ref, ob_ref, ga_ref, gb_ref, w_ref, x_ref, o_ref, xn_ref, *, da):
    j = pl.program_id(1)

    @pl.when(j == 0)
    def _():
        xn_ref[:, :da] = _rms(oa_ref[...].astype(F32), ga_ref[...]).astype(BF16)
        xn_ref[:, da:] = _rms(ob_ref[...].astype(F32), gb_ref[...]).astype(BF16)

    o_ref[...] = x_ref[...] + jnp.dot(xn_ref[...], w_ref[...], preferred_element_type=F32)


def _outproj(oa, ob, ga, gb, w, x, *, tm=512, tn=512):
    T, da = oa.shape
    db = ob.shape[1]
    N = w.shape[1]
    tm, tn = min(tm, T), min(tn, N)
    assert T % tm == 0 and N % tn == 0 and w.shape[0] == da + db
    est = 2 * (tm * (da + db) * 4 + (da + db) * tn * 2 + 2 * tm * tn * 4) + tm * (da + db) * 2
    return pl.pallas_call(
        functools.partial(_outproj_body, da=da),
        grid=(T // tm, N // tn),
        in_specs=[
            pl.BlockSpec((tm, da), lambda i, j: (i, 0)),
            pl.BlockSpec((tm, db), lambda i, j: (i, 0)),
            pl.BlockSpec((1, da), lambda i, j: (0, 0)),
            pl.BlockSpec((1, db), lambda i, j: (0, 0)),
            pl.BlockSpec((da + db, tn), lambda i, j: (0, j)),
            pl.BlockSpec((tm, tn), lambda i, j: (i, j)),
        ],
        out_specs=pl.BlockSpec((tm, tn), lambda i, j: (i, j)),
        out_shape=jax.ShapeDtypeStruct((T, N), F32),
        scratch_shapes=[pltpu.VMEM((tm, da + db), BF16)],
        compiler_params=pltpu.CompilerParams(
            dimension_semantics=("arbitrary", "arbitrary"),
            vmem_limit_bytes=_vmem_limit(est)),
        name="out_projection",
    )(oa, ob, ga.reshape(1, da), gb.reshape(1, db), w, x)


TOPK_HEADS = 2


def _candidate_blocks(topk):
    blocks = []
    a = 0
    while a < topk:
        nb = topk // (a + 1)
        if nb >= SUBLANES:
            blocks += [(a, 1, b0, min(SUBLANES, nb - b0)) for b0 in range(0, nb, SUBLANES)]
            a += 1
        elif nb > 1:
            blocks.append((a, 1, 0, nb))
            a += 1
        else:
            na = min(SUBLANES, topk - a)
            blocks.append((a, na, 0, 1))
            a += na
    return blocks


def _topk_body(q_ref, keys_ref, idx_ref, gate_ref, s_ref, sv_ref, si_ref, cs_ref, ci_ref, bs_ref, be_ref,
               *, tt, nkeys, topk, blocks):
    lowest = float(jnp.finfo(jnp.float32).min)
    nchain = 2 * TOPK_HEADS
    iota_n = lax.broadcasted_iota(jnp.int32, (nkeys, tt), 0)
    sub = lax.broadcasted_iota(jnp.int32, (SUBLANES, tt), 0)
    nt = (((1,), (1,)), ((), ()))

    for ch in range(nchain):
        s_ref[ch] = lax.dot_general(keys_ref[ch], q_ref[ch], nt, preferred_element_type=F32)

    def pick(k, carry):
        for ch in range(nchain):
            s = s_ref[ch]
            m = jnp.max(s, axis=0, keepdims=True)
            ix = jnp.min(jnp.where(s == m, iota_n, nkeys), axis=0, keepdims=True)
            sv_ref[ch, pl.ds(k, 1), :] = m
            si_ref[ch, pl.ds(k, 1), :] = ix
            s_ref[ch] = jnp.where(iota_n == ix, lowest, s)
        return carry

    lax.fori_loop(0, topk, pick, 0)

    pos_blocks = []
    for r, (a0, na, b0, nb) in enumerate(blocks):
        rows = slice(r * SUBLANES, (r + 1) * SUBLANES)
        if na == 1:
            live = sub < nb
            pos_blocks.append(a0 * topk + b0 + sub)
        else:
            live = sub < na
            pos_blocks.append((a0 + sub) * topk)
        for hd in range(TOPK_HEADS):
            if na == 1:
                cs = sv_ref[2 * hd, a0:a0 + 1, :] + sv_ref[2 * hd + 1, b0:b0 + SUBLANES, :]
                ci = si_ref[2 * hd, a0:a0 + 1, :] * nkeys + si_ref[2 * hd + 1, b0:b0 + SUBLANES, :]
            else:
                cs = sv_ref[2 * hd, a0:a0 + SUBLANES, :] + sv_ref[2 * hd + 1, 0:1, :]
                ci = si_ref[2 * hd, a0:a0 + SUBLANES, :] * nkeys + si_ref[2 * hd + 1, 0:1, :]
            cs_ref[hd, rows, :] = jnp.where(live, cs, lowest)
            ci_ref[hd, rows, :] = ci
    pos = jnp.concatenate(pos_blocks, axis=0)
    big = topk * topk

    def pick2(k, carry):
        for hd in range(TOPK_HEADS):
            c = cs_ref[hd]
            m = jnp.max(c, axis=0, keepdims=True)
            first = jnp.min(jnp.where(c == m, pos, big), axis=0, keepdims=True)
            hit = pos == first
            bs_ref[hd, pl.ds(k, 1), :] = m
            be_ref[hd, pl.ds(k, 1), :] = jnp.max(jnp.where(hit, ci_ref[hd], -1), axis=0, keepdims=True)
            cs_ref[hd] = jnp.where(hit, lowest, c)
        return carry

    lax.fori_loop(0, topk, pick2, 0)

    for hd in range(TOPK_HEADS):
        b = bs_ref[hd]
        e = jnp.exp(b - jnp.max(b, axis=0, keepdims=True))
        gate_ref[hd * topk:(hd + 1) * topk, :] = e / jnp.sum(e, axis=0, keepdims=True)
        idx_ref[hd * topk:(hd + 1) * topk, :] = be_ref[hd]


def _peer_topk(q_slabs, keys, *, tt=128):
    hp, T, c = q_slabs.shape
    heads = hp // 2
    nkeys = keys.shape[1]
    tt = min(tt, T)
    topk = PEER_TOPK
    assert T % tt == 0 and heads % TOPK_HEADS == 0 and topk % SUBLANES == 0
    blocks = _candidate_blocks(topk)
    ncand = len(blocks) * SUBLANES
    nchain = 2 * TOPK_HEADS
    rows = TOPK_HEADS * topk
    return pl.pallas_call(
        functools.partial(_topk_body, tt=tt, nkeys=nkeys, topk=topk, blocks=blocks),
        grid=(T // tt, heads // TOPK_HEADS),
        in_specs=[
            pl.BlockSpec((nchain, tt, c), lambda i, h: (h, i, 0)),
            pl.BlockSpec((nchain, nkeys, c), lambda i, h: (h, 0, 0)),
        ],
        out_specs=[pl.BlockSpec((rows, tt), lambda i, h: (h, i)),
                   pl.BlockSpec((rows, tt), lambda i, h: (h, i))],
        out_shape=[jax.ShapeDtypeStruct((heads * topk, T), jnp.int32),
                   jax.ShapeDtypeStruct((heads * topk, T), F32)],
        scratch_shapes=[pltpu.VMEM((nchain, nkeys, tt), F32),
                        pltpu.VMEM((nchain, topk, tt), F32), pltpu.VMEM((nchain, topk, tt), jnp.int32),
                        pltpu.VMEM((TOPK_HEADS, ncand, tt), F32), pltpu.VMEM((TOPK_HEADS, ncand, tt), jnp.int32),
                        pltpu.VMEM((TOPK_HEADS, topk, tt), F32), pltpu.VMEM((TOPK_HEADS, topk, tt), jnp.int32)],
        compiler_params=pltpu.CompilerParams(dimension_semantics=("arbitrary", "arbitrary")),
        name="peer_topk",
    )(q_slabs, keys)


GSIZE = SUBLANES
NGROUP = PEER_E // GSIZE
ROW_PITCH = 17


def _gelu_exact(x):
    return 0.5 * x * (1.0 + lax.erf(x * (2.0 ** -0.5)))


def _sublane_sums(vs, sub):
    lo = sub < 4
    halves = []
    for j in range(4):
        a, b = vs[j], vs[j + 4]
        halves.append(jnp.where(lo, a, b) + pltpu.roll(jnp.where(lo, b, a), 4, 0))
    done = []
    for j, (sa, sb) in enumerate(((6, 7), (6, 1), (2, 7), (2, 1))):
        c = halves[j] + pltpu.roll(halves[j], sa, 0)
        done.append(c + pltpu.roll(c, sb, 0))
    q = sub & 3
    return jnp.where(q == 0, done[0], jnp.where(q == 1, done[1], jnp.where(q == 2, done[2], done[3])))


def _peer_ffn_body(idxc_ref, idxn_ref, gate_ref, x_ref, h_ref, gf_ref, tab_ref, o_ref,
                   buf0, buf1, sem, *, ctok, nrow):
    bufs = (buf0, buf1)
    i = pl.program_id(0)
    n = pl.num_programs(0)
    sub = lax.broadcasted_iota(jnp.int32, (SUBLANES, LANES), 0)
    lane = lax.broadcasted_iota(jnp.int32, (SUBLANES, LANES), 1)
    hi_mask = jnp.uint32(0xFFFF0000)

    def start_row(idx_ref, row, dst_slot, c, j):
        k, g = divmod(j, NGROUP)
        first_row = ((c * NGROUP + g) * GSIZE + k) * ROW_PITCH
        pltpu.make_async_copy(tab_ref.at[idx_ref[row, j]], bufs[dst_slot].at[pl.ds(first_row, nrow), :],
                              sem.at[dst_slot]).start(priority=j % 2)

    def wait_slot(s):
        copied = pl.ds(0, ctok * PEER_E * nrow)
        pltpu.make_async_copy(bufs[s].at[copied, :], bufs[s].at[copied, :], sem.at[s]).wait()

    def group_row(half, c, g, s):
        return bufs[half][pl.ds((c * NGROUP + g) * GSIZE * ROW_PITCH + s, GSIZE, stride=ROW_PITCH), :]

    @pl.when(i == 0)
    def _():
        def first(c, carry):
            for j in range(PEER_E):
                start_row(idxc_ref, c, 0, c, j)
            return carry
        lax.fori_loop(0, ctok, first, 0)

    for half in range(2):
        wait_slot(half)
        gate_h = gate_ref[half]

        yaccs = []
        for c in range(ctok):
            tok = half * ctok + c
            if half == 0:
                nxt = lambda j, c=c: start_row(idxc_ref, ctok + c, 1, c, j)
            else:
                nxt = lambda j, c=c: start_row(idxn_ref, c, 0, c, j)
            xb = [jnp.broadcast_to(x_ref[tok:tok + 1, s * LANES:(s + 1) * LANES], (SUBLANES, LANES))
                  for s in range(nrow)]
            dots = jnp.zeros((SUBLANES, LANES), F32)
            for g in range(NGROUP):
                for j in range(g * 4, g * 4 + 4):
                    nxt(j)
                r = None
                for s in range(nrow):
                    w = group_row(half, c, g, s)
                    p = pltpu.bitcast(w << 16, F32) * xb[s]
                    r = p if r is None else r + p
                d = jnp.sum(r, axis=1, keepdims=True)
                dots = jnp.where(lane == c * NGROUP + g, d, dots)
            act = _gelu_exact(dots) * gate_h
            yacc = [None] * nrow
            for g in range(NGROUP):
                for j in range(PEER_E // 2 + g * 4, PEER_E // 2 + g * 4 + 4):
                    nxt(j)
                col = c * NGROUP + g
                a = jnp.broadcast_to(act[:, col:col + 1], (SUBLANES, LANES))
                for s in range(nrow):
                    w = group_row(half, c, g, s)
                    t = a * pltpu.bitcast(w & hi_mask, F32)
                    yacc[s] = t if yacc[s] is None else yacc[s] + t
            yaccs.append(yacc)
        toks = slice(half * ctok, (half + 1) * ctok)
        zs = []
        ss = jnp.zeros((SUBLANES, LANES), F32)
        for s in range(nrow):
            cols = slice(s * LANES, (s + 1) * LANES)
            z = h_ref[toks, cols] + _sublane_sums([yaccs[c][s] for c in range(ctok)], sub)
            ss = ss + z * z
            zs.append(z)
        inv = lax.rsqrt(jnp.sum(ss, axis=1, keepdims=True) * (1.0 / (nrow * LANES)) + EPS)
        for s in range(nrow):
            cols = slice(s * LANES, (s + 1) * LANES)
            o_ref[toks, cols] = zs[s] * inv * gf_ref[:, cols]

    @pl.when(i == n - 1)
    def _():
        wait_slot(0)


def _peer_ffn(idx, gate_h, xn, h, gf, table, *, ctok=8):
    T, nexp = idx.shape
    D = xn.shape[1]
    nrow = D // LANES
    step = 2 * ctok
    assert nexp == PEER_E and ctok == SUBLANES and T % step == 0 and table.shape[1:] == (nrow, LANES)
    n = T // step
    last = n - 1
    assert ROW_PITCH > nrow
    est = 2 * ctok * nexp * ROW_PITCH * LANES * 4 + 6 * step * D * 4
    return pl.pallas_call(
        functools.partial(_peer_ffn_body, ctok=ctok, nrow=nrow),
        grid=(n,),
        in_specs=[
            pl.BlockSpec((step, nexp), lambda i: (i, 0), memory_space=pltpu.SMEM),
            pl.BlockSpec((step, nexp), lambda i: (jnp.minimum(i + 1, last), 0), memory_space=pltpu.SMEM),
            pl.BlockSpec((2, SUBLANES, LANES), lambda i: (i, 0, 0)),
            pl.BlockSpec((step, D), lambda i: (i, 0)),
            pl.BlockSpec((step, D), lambda i: (i, 0)),
            pl.BlockSpec((1, D), lambda i: (0, 0)),
            pl.BlockSpec(memory_space=pl.ANY),
        ],
        out_specs=pl.BlockSpec((step, D), lambda i: (i, 0)),
        out_shape=jax.ShapeDtypeStruct((T, D), F32),
        scratch_shapes=[pltpu.VMEM((ctok * nexp * ROW_PITCH, LANES), jnp.uint32),
                        pltpu.VMEM((ctok * nexp * ROW_PITCH, LANES), jnp.uint32),
                        pltpu.SemaphoreType.DMA((2,))],
        compiler_params=pltpu.CompilerParams(
            dimension_semantics=("arbitrary",),
            vmem_limit_bytes=_vmem_limit(est)),
        name="peer_ffn",
    )(idx, idx, gate_h, xn, h, gf, table)


def _pack_expert_table(u, v):
    ub = lax.bitcast_convert_type(u.astype(BF16), jnp.uint16).astype(jnp.uint32)
    vb = lax.bitcast_convert_type(v.astype(BF16), jnp.uint16).astype(jnp.uint32)
    n, d = u.shape
    return (ub | (vb << 16)).reshape(n, d // LANES, LANES)


def _rope_tables(seq):
    half = QK_ROPE // 2
    freqs = ROPE_THETA ** (-jnp.arange(half, dtype=F32) / half)
    ang = jnp.arange(seq, dtype=F32)[:, None] * freqs[None, :]
    cos, sin = jnp.cos(ang), jnp.sin(ang)
    zeros = jnp.zeros((seq, LANES - QK_ROPE), F32)
    return (jnp.concatenate([cos, cos, zeros], axis=1),
            jnp.concatenate([-sin, sin, zeros], axis=1))


def _layer(h, ln1_g, w_in, q_a_norm, kv_a_norm, w_uq, w_ukv, out_norm_dil, out_norm_mla, w_o,
           ln2_g, peer_wq, peer_sub_keys, peer_u, peer_v, final_g):
    B, S, D = h.shape
    T = B * S
    H = N_HEADS
    d_dil = H * HEAD_DIM
    q_lora = q_a_norm.shape[0]
    kv_lora = kv_a_norm.shape[0]
    x2 = h.reshape(T, D)

    d_in = w_in.shape[1]
    d_in_pad = -(-d_in // LANES) * LANES
    w_in_p = jnp.pad(w_in, ((0, 0), (0, d_in_pad - d_in))).astype(BF16)
    z = _norm_matmul(x2, ln1_g, w_in_p, tm=ROW_TILE,
                     tn=d_in_pad // 3 if d_in_pad % (3 * LANES) == 0 else LANES)
    z3 = z.reshape(B, S, d_in_pad)

    slopes = 2.0 ** (-8.0 * jnp.arange(1, H + 1, dtype=F32) / H)
    o_a = _dilated_attention(z3, slopes, n_heads=H)

    wq = w_uq.reshape(q_lora, H, QK_NOPE + QK_ROPE)
    wq_pe = jnp.pad(wq[:, :, QK_NOPE:], ((0, 0), (0, 0), (0, LANES - QK_ROPE)))
    wq_p = jnp.concatenate([wq[:, :, :QK_NOPE].reshape(q_lora, H * QK_NOPE),
                            wq_pe.reshape(q_lora, H * LANES)], axis=1).astype(BF16)
    wkv = w_ukv.reshape(kv_lora, H, QK_NOPE + HEAD_DIM)
    wkv_p = jnp.concatenate([wkv[:, :, :QK_NOPE].reshape(kv_lora, H * QK_NOPE),
                             wkv[:, :, QK_NOPE:].reshape(kv_lora, H * HEAD_DIM)], axis=1).astype(BF16)
    assert q_lora == kv_lora and (3 * d_dil) % q_lora == 0
    q_mla = _norm_matmul(z, q_a_norm, wq_p, xcol=3 * d_dil // q_lora, kdim=q_lora, out_dtype=BF16)
    kv_mla = _norm_matmul(z, kv_a_norm, wkv_p, xcol=3 * d_dil // q_lora + 1, kdim=kv_lora, out_dtype=BF16)
    cos, sin_signed = _rope_tables(S)
    o_b = _mla_attention(q_mla.reshape(B, S, -1), kv_mla.reshape(B, S, -1), z3,
                         (3 * d_dil + q_lora + kv_lora) // LANES, cos, sin_signed, n_heads=H)

    h1 = _outproj(o_a.reshape(T, d_dil), o_b.reshape(T, -1), out_norm_dil, out_norm_mla,
                  w_o.astype(BF16), x2, tm=ROW_TILE)

    q_slabs, xn2 = _norm_matmul(h1, ln2_g, peer_wq.astype(BF16), tm=ROW_TILE, emit_xn=True,
                                slab_out=True, out_dtype=BF16)
    keys = peer_sub_keys.reshape(PEER_HEADS * 2, PEER_NKEYS, -1).astype(BF16)
    idx_t, gate_t = _peer_topk(q_slabs, keys)
    table = _pack_expert_table(peer_u, peer_v)
    gate_h = gate_t.reshape(GSIZE, NGROUP, T // GSIZE, GSIZE).transpose(2, 0, 3, 1).reshape(T // GSIZE, GSIZE, LANES)
    out = _peer_ffn(idx_t.T, gate_h, xn2, h1, final_g.reshape(1, D), table)
    return out.reshape(B, S, D)


def kernel(x, ln1_g, w_in, q_a_norm, kv_a_norm, w_uq, w_ukv, out_norm_dil, out_norm_mla, w_o,
           ln2_g, peer_wq, peer_sub_keys, peer_u, peer_v, lnf_g):
    assert ln1_g.shape[0] == 1, "single-layer trunk"
    return _layer(x, ln1_g[0], w_in[0], q_a_norm[0], kv_a_norm[0], w_uq[0], w_ukv[0],
                  out_norm_dil[0], out_norm_mla[0], w_o[0], ln2_g[0], peer_wq[0],
                  peer_sub_keys[0], peer_u[0], peer_v[0], lnf_g)
```

```python
import functools

import jax
import jax.numpy as jnp
from jax import lax
from jax.experimental import pallas as pl
from jax.experimental.pallas import tpu as pltpu

F32 = jnp.float32
BF16 = jnp.bfloat16

EPS = 1e-6
NEG = -1e30
HEAD_DIM = 128
BLOCK = 128
DIL_PATTERNS = ((128, 1), (512, 4), (2048, 16))
DIL_UNROLL = 16
N_HEADS = 8
QK_NOPE = 128
QK_ROPE = 64
ROPE_THETA = 10000.0
PEER_HEADS = 8
PEER_NKEYS = 128
PEER_TOPK = 16
PEER_E = PEER_HEADS * PEER_TOPK

LANES = 128
SUBLANES = 8
VMEM_CAP = 60000 * 1024
ROW_TILE = 1024


def _vmem_limit(nbytes):
    return int(min(VMEM_CAP, max(16 * 1024 * 1024, nbytes * 3 // 2)))


def _rms(x, g):
    return x * lax.rsqrt(jnp.mean(x * x, axis=-1, keepdims=True) + EPS) * g


def _norm_matmul_body(x_ref, g_ref, w_ref, *rest, emit_xn, slabs):
    if emit_xn:
        o_ref, xn_out_ref, xn_ref = rest
    else:
        o_ref, xn_ref = rest
    j = pl.program_id(1)

    @pl.when(j == 0)
    def _():
        xn = _rms(x_ref[...].astype(F32), g_ref[...])
        xn_ref[...] = xn.astype(BF16)
        if emit_xn:
            xn_out_ref[...] = xn

    res = jnp.dot(xn_ref[...], w_ref[...], preferred_element_type=F32).astype(o_ref.dtype)
    if slabs:
        for s in range(slabs):
            o_ref[s] = res[:, s * LANES:(s + 1) * LANES]
    else:
        o_ref[...] = res


def _norm_matmul(x, g, w, *, xcol=0, kdim=None, tm=512, tn=512, emit_xn=False, slab_out=False,
                 out_dtype=F32):
    T = x.shape[0]
    kdim = kdim or x.shape[1]
    N = w.shape[1]
    tm = min(tm, T)
    tn = min(tn, N)
    assert T % tm == 0 and N % tn == 0 and w.shape[0] == kdim
    slabs = tn // LANES if slab_out else 0
    if slab_out:
        out_shape = [jax.ShapeDtypeStruct((N // LANES, T, LANES), out_dtype)]
        out_specs = [pl.BlockSpec((slabs, tm, LANES), lambda i, j: (j, i, 0))]
    else:
        out_shape = [jax.ShapeDtypeStruct((T, N), out_dtype)]
        out_specs = [pl.BlockSpec((tm, tn), lambda i, j: (i, j))]
    if emit_xn:
        out_shape.append(jax.ShapeDtypeStruct((T, kdim), F32))
        out_specs.append(pl.BlockSpec((tm, kdim), lambda i, j: (i, 0)))
    est = 2 * (tm * kdim * 4 + kdim * tn * 2 + tm * tn * 4) + tm * kdim * 2
    if emit_xn:
        est += 2 * tm * kdim * 4
    outs = pl.pallas_call(
        functools.partial(_norm_matmul_body, emit_xn=emit_xn, slabs=slabs),
        grid=(T // tm, N // tn),
        in_specs=[
            pl.BlockSpec((tm, kdim), lambda i, j: (i, xcol)),
            pl.BlockSpec((1, kdim), lambda i, j: (0, 0)),
            pl.BlockSpec((kdim, tn), lambda i, j: (0, j)),
        ],
        out_specs=out_specs,
        out_shape=out_shape,
        scratch_shapes=[pltpu.VMEM((tm, kdim), BF16)],
        compiler_params=pltpu.CompilerParams(
            dimension_semantics=("arbitrary", "arbitrary"),
            vmem_limit_bytes=_vmem_limit(est)),
        name="norm_matmul",
    )(x, g.reshape(1, kdim).astype(F32), w)
    return outs if emit_xn else outs[0]


def _dilated_body(slopes_ref, q_ref, k_ref, v_ref, o_ref, o_scr, l_scr, *, seq, patterns, scale):
    h = pl.program_id(1)
    slope = slopes_ref[h]
    qi = lax.broadcasted_iota(jnp.int32, (BLOCK, 2 * BLOCK), 0)
    kj = lax.broadcasted_iota(jnp.int32, (BLOCK, 2 * BLOCK), 1)
    delta = qi + BLOCK - kj
    nt = (((1,), (1,)), ((), ()))

    for p, (window, d) in enumerate(patterns):
        steps = window // d
        nb = seq // d // BLOCK
        in_window = (delta >= 0) & (delta <= steps)
        bias = -slope * (delta * d).astype(F32)

        def block(t, carry, d=d, nb=nb, in_window=in_window, bias=bias, p=p):
            r = t // nb
            n = t % nb
            start = n * (BLOCK * d) + r
            pstart = jnp.maximum(n - 1, 0) * (BLOCK * d) + r
            rows = pl.ds(start, BLOCK, stride=d) if d > 1 else pl.ds(start, BLOCK)
            prow = pl.ds(pstart, BLOCK, stride=d) if d > 1 else pl.ds(pstart, BLOCK)
            q = q_ref[0, rows, :].astype(BF16)
            kk = jnp.concatenate([k_ref[0, prow, :], k_ref[0, rows, :]], axis=0).astype(BF16)
            vv = jnp.concatenate([v_ref[0, prow, :], v_ref[0, rows, :]], axis=0).astype(BF16)
            s = lax.dot_general(q, kk, nt, preferred_element_type=F32) * scale
            valid = in_window & ((kj >= BLOCK) | (n > 0))
            s = jnp.where(valid, s + bias, NEG)
            m = jnp.max(s, axis=-1, keepdims=True)
            e = jnp.exp(s - m)
            l = jnp.sum(e, axis=-1, keepdims=True)
            o = jnp.dot((e / l).astype(BF16), vv, preferred_element_type=F32)
            o_scr[p, rows, :] = o
            l_scr[p, rows, :] = jnp.broadcast_to(m + jnp.log(l), (BLOCK, HEAD_DIM))
            return carry

        lax.fori_loop(0, d * nb, block, 0, unroll=DIL_UNROLL)

    def mix(c, carry):
        rows = pl.ds(pl.multiple_of(c * BLOCK, BLOCK), BLOCK)
        ls = [l_scr[p, rows, :] for p in range(len(patterns))]
        m = functools.reduce(jnp.maximum, ls)
        es = [jnp.exp(l - m) for l in ls]
        den = functools.reduce(jnp.add, es)
        num = functools.reduce(jnp.add, [e * o_scr[p, rows, :] for p, e in enumerate(es)])
        o_ref[0, rows, :] = (num / den).astype(o_ref.dtype)
        return carry

    lax.fori_loop(0, seq // BLOCK, mix, 0)


def _dilated_attention(z3, slopes, *, n_heads, patterns=DIL_PATTERNS):
    B, S, _ = z3.shape
    for _, d in patterns:
        assert S % (d * BLOCK) == 0
    blk = (1, S, HEAD_DIM)
    est = 2 * 4 * S * HEAD_DIM * 4 + 2 * len(patterns) * S * HEAD_DIM * 4
    return pl.pallas_call(
        functools.partial(_dilated_body, seq=S, patterns=patterns, scale=HEAD_DIM ** -0.5),
        grid=(B, n_heads),
        in_specs=[
            pl.BlockSpec(memory_space=pltpu.SMEM),
            pl.BlockSpec(blk, lambda b, h: (b, 0, h)),
            pl.BlockSpec(blk, lambda b, h: (b, 0, n_heads + h)),
            pl.BlockSpec(blk, lambda b, h: (b, 0, 2 * n_heads + h)),
        ],
        out_specs=pl.BlockSpec(blk, lambda b, h: (b, 0, h)),
        out_shape=jax.ShapeDtypeStruct((B, S, n_heads * HEAD_DIM), BF16),
        scratch_shapes=[pltpu.VMEM((len(patterns), S, HEAD_DIM), F32),
                        pltpu.VMEM((len(patterns), S, HEAD_DIM), F32)],
        compiler_params=pltpu.CompilerParams(
            dimension_semantics=("arbitrary", "arbitrary"),
            vmem_limit_bytes=_vmem_limit(est)),
        name="dilated_attention",
    )(slopes, z3, z3, z3)


def _rope(x, cos, sin_signed):
    lane = lax.broadcasted_iota(jnp.int32, x.shape, 1)
    half = QK_ROPE // 2
    swapped = jnp.where(lane < half, pltpu.roll(x, LANES - half, 1), pltpu.roll(x, half, 1))
    return x * cos + swapped * sin_signed


def _mla_body(qn_ref, qp_ref, kn_ref, v_ref, kr_ref, cos_ref, sin_ref,
              o_ref, qcat_scr, kcat_scr, *, seq, tq, scale):
    cos, sin = cos_ref[...], sin_ref[...]
    qcat_scr[:, :LANES] = qn_ref[0]
    qcat_scr[:, LANES:] = _rope(qp_ref[0].astype(F32), cos, sin).astype(BF16)
    kcat_scr[:, :LANES] = kn_ref[0]
    kcat_scr[:, LANES:] = _rope(kr_ref[0], cos, sin).astype(BF16)
    nt = (((1,), (1,)), ((), ()))

    for qi in range(seq // tq):
        ext = (qi + 1) * tq
        rows = slice(qi * tq, ext)
        s = lax.dot_general(qcat_scr[rows, :], kcat_scr[:ext, :], nt, preferred_element_type=F32) * scale
        qpos = qi * tq + lax.broadcasted_iota(jnp.int32, (tq, ext), 0)
        kpos = lax.broadcasted_iota(jnp.int32, (tq, ext), 1)
        s = jnp.where(kpos <= qpos, s, NEG)
        e = jnp.exp(s - jnp.max(s, axis=-1, keepdims=True))
        l = jnp.sum(e, axis=-1, keepdims=True)
        o = jnp.dot(e.astype(BF16), v_ref[0, :ext, :], preferred_element_type=F32) / l
        o_ref[0, rows, :] = o.astype(o_ref.dtype)


def _mla_attention(q3, kv3, z3, kr_col, cos, sin_signed, *, n_heads, tq=512):
    B, S, _ = q3.shape
    tq = min(tq, S)
    assert S % tq == 0 and q3.dtype == BF16 and kv3.dtype == BF16
    blk = (1, S, LANES)
    est = 2 * (5 * 2 + 3 * 4) * S * LANES + 4 * S * LANES * 2 + 3 * tq * S * 4
    return pl.pallas_call(
        functools.partial(_mla_body, seq=S, tq=tq, scale=(QK_NOPE + QK_ROPE) ** -0.5),
        grid=(B, n_heads),
        in_specs=[
            pl.BlockSpec(blk, lambda b, h: (b, 0, h)),
            pl.BlockSpec(blk, lambda b, h: (b, 0, n_heads + h)),
            pl.BlockSpec(blk, lambda b, h: (b, 0, h)),
            pl.BlockSpec(blk, lambda b, h: (b, 0, n_heads + h)),
            pl.BlockSpec(blk, lambda b, h: (b, 0, kr_col)),
            pl.BlockSpec((S, LANES), lambda b, h: (0, 0)),
            pl.BlockSpec((S, LANES), lambda b, h: (0, 0)),
        ],
        out_specs=pl.BlockSpec(blk, lambda b, h: (b, 0, h)),
        out_shape=jax.ShapeDtypeStruct((B, S, n_heads * HEAD_DIM), BF16),
        scratch_shapes=[pltpu.VMEM((S, 2 * LANES), BF16), pltpu.VMEM((S, 2 * LANES), BF16)],
        compiler_params=pltpu.CompilerParams(
            dimension_semantics=("arbitrary", "arbitrary"),
            vmem_limit_bytes=_vmem_limit(est)),
        name="mla_attention",
    )(q3, q3, kv3, kv3, z3, cos, sin_signed)


def _outproj_body(oa_ref, ob_ref, ga_ref, gb_ref, w_ref, x_ref, o_ref, xn_ref, *, da):
    j = pl.program_id(1)

    @pl.when(j == 0)
    def _():
        xn_ref[:, :da] = _rms(oa_ref[...].astype(F32), ga_ref[...]).astype(BF16)
        xn_ref[:, da:] = _rms(ob_ref[...].astype(F32), gb_ref[...]).astype(BF16)

    o_ref[...] = x_ref[...] + jnp.dot(xn_ref[...], w_ref[...], preferred_element_type=F32)


def _outproj(oa, ob, ga, gb, w, x, *, tm=512, tn=512):
    T, da = oa.shape
    db = ob.shape[1]
    N = w.shape[1]
    tm, tn = min(tm, T), min(tn, N)
    assert T % tm == 0 and N % tn == 0 and w.shape[0] == da + db
    est = 2 * (tm * (da + db) * 4 + (da + db) * tn * 2 + 2 * tm * tn * 4) + tm * (da + db) * 2
    return pl.pallas_call(
        functools.partial(_outproj_body, da=da),
        grid=(T // tm, N // tn),
        in_specs=[
            pl.BlockSpec((tm, da), lambda i, j: (i, 0)),
            pl.BlockSpec((tm, db), lambda i, j: (i, 0)),
            pl.BlockSpec((1, da), lambda i, j: (0, 0)),
            pl.BlockSpec((1, db), lambda i, j: (0, 0)),
            pl.BlockSpec((da + db, tn), lambda i, j: (0, j)),
            pl.BlockSpec((tm, tn), lambda i, j: (i, j)),
        ],
        out_specs=pl.BlockSpec((tm, tn), lambda i, j: (i, j)),
        out_shape=jax.ShapeDtypeStruct((T, N), F32),
        scratch_shapes=[pltpu.VMEM((tm, da + db), BF16)],
        compiler_params=pltpu.CompilerParams(
            dimension_semantics=("arbitrary", "arbitrary"),
            vmem_limit_bytes=_vmem_limit(est)),
        name="out_projection",
    )(oa, ob, ga.reshape(1, da), gb.reshape(1, db), w, x)


TOPK_HEADS = 2


def _candidate_blocks(topk):
    blocks = []
    a = 0
    while a < topk:
        nb = topk // (a + 1)
        if nb >= SUBLANES:
            blocks += [(a, 1, b0, min(SUBLANES, nb - b0)) for b0 in range(0, nb, SUBLANES)]
            a += 1
        elif nb > 1:
            blocks.append((a, 1, 0, nb))
            a += 1
        else:
            na = min(SUBLANES, topk - a)
            blocks.append((a, na, 0, 1))
            a += na
    return blocks


def _topk_body(q_ref, keys_ref, idx_ref, gate_ref, s_ref, sv_ref, si_ref, cs_ref, ci_ref, bs_ref, be_ref,
               *, tt, nkeys, topk, blocks):
    lowest = float(jnp.finfo(jnp.float32).min)
    nchain = 2 * TOPK_HEADS
    iota_n = lax.broadcasted_iota(jnp.int32, (nkeys, tt), 0)
    sub = lax.broadcasted_iota(jnp.int32, (SUBLANES, tt), 0)
    nt = (((1,), (1,)), ((), ()))

    for ch in range(nchain):
        s_ref[ch] = lax.dot_general(keys_ref[ch], q_ref[ch], nt, preferred_element_type=F32)

    def pick(k, carry):
        for ch in range(nchain):
            s = s_ref[ch]
            m = jnp.max(s, axis=0, keepdims=True)
            ix = jnp.min(jnp.where(s == m, iota_n, nkeys), axis=0, keepdims=True)
            sv_ref[ch, pl.ds(k, 1), :] = m
            si_ref[ch, pl.ds(k, 1), :] = ix
            s_ref[ch] = jnp.where(iota_n == ix, lowest, s)
        return carry

    lax.fori_loop(0, topk, pick, 0)

    pos_blocks = []
    for r, (a0, na, b0, nb) in enumerate(blocks):
        rows = slice(r * SUBLANES, (r + 1) * SUBLANES)
        if na == 1:
            live = sub < nb
            pos_blocks.append(a0 * topk + b0 + sub)
        else:
            live = sub < na
            pos_blocks.append((a0 + sub) * topk)
        for hd in range(TOPK_HEADS):
            if na == 1:
                cs = sv_ref[2 * hd, a0:a0 + 1, :] + sv_ref[2 * hd + 1, b0:b0 + SUBLANES, :]
                ci = si_ref[2 * hd, a0:a0 + 1, :] * nkeys + si_ref[2 * hd + 1, b0:b0 + SUBLANES, :]
            else:
                cs = sv_ref[2 * hd, a0:a0 + SUBLANES, :] + sv_ref[2 * hd + 1, 0:1, :]
                ci = si_ref[2 * hd, a0:a0 + SUBLANES, :] * nkeys + si_ref[2 * hd + 1, 0:1, :]
            cs_ref[hd, rows, :] = jnp.where(live, cs, lowest)
            ci_ref[hd, rows, :] = ci
    pos = jnp.concatenate(pos_blocks, axis=0)
    big = topk * topk

    def pick2(k, carry):
        for hd in range(TOPK_HEADS):
            c = cs_ref[hd]
            m = jnp.max(c, axis=0, keepdims=True)
            first = jnp.min(jnp.where(c == m, pos, big), axis=0, keepdims=True)
            hit = pos == first
            bs_ref[hd, pl.ds(k, 1), :] = m
            be_ref[hd, pl.ds(k, 1), :] = jnp.max(jnp.where(hit, ci_ref[hd], -1), axis=0, keepdims=True)
            cs_ref[hd] = jnp.where(hit, lowest, c)
        return carry

    lax.fori_loop(0, topk, pick2, 0)

    for hd in range(TOPK_HEADS):
        b = bs_ref[hd]
        e = jnp.exp(b - jnp.max(b, axis=0, keepdims=True))
        gate_ref[hd * topk:(hd + 1) * topk, :] = e / jnp.sum(e, axis=0, keepdims=True)
        idx_ref[hd * topk:(hd + 1) * topk, :] = be_ref[hd]


def _peer_topk(q_slabs, keys, *, tt=128):
    hp, T, c = q_slabs.shape
    heads = hp // 2
    nkeys = keys.shape[1]
    tt = min(tt, T)
    topk = PEER_TOPK
    assert T % tt == 0 and heads % TOPK_HEADS == 0 and topk % SUBLANES == 0
    blocks = _candidate_blocks(topk)
    ncand = len(blocks) * SUBLANES
    nchain = 2 * TOPK_HEADS
    rows = TOPK_HEADS * topk
    return pl.pallas_call(
        functools.partial(_topk_body, tt=tt, nkeys=nkeys, topk=topk, blocks=blocks),
        grid=(T // tt, heads // TOPK_HEADS),
        in_specs=[
            pl.BlockSpec((nchain, tt, c), lambda i, h: (h, i, 0)),
            pl.BlockSpec((nchain, nkeys, c), lambda i, h: (h, 0, 0)),
        ],
        out_specs=[pl.BlockSpec((rows, tt), lambda i, h: (h, i)),
                   pl.BlockSpec((rows, tt), lambda i, h: (h, i))],
        out_shape=[jax.ShapeDtypeStruct((heads * topk, T), jnp.int32),
                   jax.ShapeDtypeStruct((heads * topk, T), F32)],
        scratch_shapes=[pltpu.VMEM((nchain, nkeys, tt), F32),
                        pltpu.VMEM((nchain, topk, tt), F32), pltpu.VMEM((nchain, topk, tt), jnp.int32),
                        pltpu.VMEM((TOPK_HEADS, ncand, tt), F32), pltpu.VMEM((TOPK_HEADS, ncand, tt), jnp.int32),
                        pltpu.VMEM((TOPK_HEADS, topk, tt), F32), pltpu.VMEM((TOPK_HEADS, topk, tt), jnp.int32)],
        compiler_params=pltpu.CompilerParams(dimension_semantics=("arbitrary", "arbitrary")),
        name="peer_topk",
    )(q_slabs, keys)


GSIZE = SUBLANES
NGROUP = PEER_E // GSIZE
ROW_PITCH = 17
FFN_SLOTS = 4
FFN_AHEAD = 2


def _gelu_exact(x):
    return 0.5 * x * (1.0 + lax.erf(x * (2.0 ** -0.5)))


def _sublane_sums(vs, sub):
    lo = sub < 4
    halves = []
    for j in range(4):
        a, b = vs[j], vs[j + 4]
        halves.append(jnp.where(lo, a, b) + pltpu.roll(jnp.where(lo, b, a), 4, 0))
    done = []
    for j, (sa, sb) in enumerate(((6, 7), (6, 1), (2, 7), (2, 1))):
        c = halves[j] + pltpu.roll(halves[j], sa, 0)
        done.append(c + pltpu.roll(c, sb, 0))
    q = sub & 3
    return jnp.where(q == 0, done[0], jnp.where(q == 1, done[1], jnp.where(q == 2, done[2], done[3])))


def _peer_ffn_body(idxc_ref, idxn_ref, gate_ref, x_ref, h_ref, gf_ref, tab_ref, o_ref, *scratch,
                   ctok, nrow):
    bufs, sem = scratch[:FFN_SLOTS], scratch[FFN_SLOTS]
    i = pl.program_id(0)
    n = pl.num_programs(0)
    sub = lax.broadcasted_iota(jnp.int32, (SUBLANES, LANES), 0)
    lane = lax.broadcasted_iota(jnp.int32, (SUBLANES, LANES), 1)
    hi_mask = jnp.uint32(0xFFFF0000)

    def start_row(idx_ref, row, dst_slot, c, j):
        k, g = divmod(j, NGROUP)
        first_row = ((c * NGROUP + g) * GSIZE + k) * ROW_PITCH
        pltpu.make_async_copy(tab_ref.at[idx_ref[row, j]], bufs[dst_slot].at[pl.ds(first_row, nrow), :],
                              sem.at[dst_slot]).start(priority=j % 2)

    def wait_slot(s):
        copied = pl.ds(0, ctok * PEER_E * nrow)
        pltpu.make_async_copy(bufs[s].at[copied, :], bufs[s].at[copied, :], sem.at[s]).wait()

    def group_row(half, c, g, s):
        return bufs[half][pl.ds((c * NGROUP + g) * GSIZE * ROW_PITCH + s, GSIZE, stride=ROW_PITCH), :]

    @pl.when(i == 0)
    def _():
        for p in range(FFN_AHEAD):
            def first(c, carry, p=p):
                for j in range(PEER_E):
                    start_row(idxc_ref, p * ctok + c, p, c, j)
                return carry
            lax.fori_loop(0, ctok, first, 0)

    yaccs = []
    for p in range(FFN_SLOTS):
        wait_slot(p)
        ahead = p + FFN_AHEAD
        src_idx, ahead = (idxc_ref, ahead) if ahead < FFN_SLOTS else (idxn_ref, ahead - FFN_SLOTS)
        for c in range(ctok):
            tok = p * ctok + c
            gate_h = gate_ref[tok // SUBLANES]
            lane0 = (tok % SUBLANES) * NGROUP
            nxt = lambda j, c=c: start_row(src_idx, ahead * ctok + c, ahead, c, j)
            xb = [jnp.broadcast_to(x_ref[tok:tok + 1, s * LANES:(s + 1) * LANES], (SUBLANES, LANES))
                  for s in range(nrow)]
            dots = jnp.zeros((SUBLANES, LANES), F32)
            for g in range(NGROUP):
                for j in range(g * 4, g * 4 + 4):
                    nxt(j)
                r = None
                for s in range(nrow):
                    w = group_row(p, c, g, s)
                    q = pltpu.bitcast(w << 16, F32) * xb[s]
                    r = q if r is None else r + q
                d = jnp.sum(r, axis=1, keepdims=True)
                dots = jnp.where(lane == lane0 + g, d, dots)
            act = _gelu_exact(dots) * gate_h
            yacc = [None] * nrow
            for g in range(NGROUP):
                for j in range(PEER_E // 2 + g * 4, PEER_E // 2 + g * 4 + 4):
                    nxt(j)
                a = jnp.broadcast_to(act[:, lane0 + g:lane0 + g + 1], (SUBLANES, LANES))
                for s in range(nrow):
                    w = group_row(p, c, g, s)
                    t = a * pltpu.bitcast(w & hi_mask, F32)
                    yacc[s] = t if yacc[s] is None else yacc[s] + t
            yaccs.append(yacc)
        if len(yaccs) == SUBLANES:
            last_tok = (p + 1) * ctok
            toks = slice(last_tok - SUBLANES, last_tok)
            zs = []
            ss = jnp.zeros((SUBLANES, LANES), F32)
            for s in range(nrow):
                cols = slice(s * LANES, (s + 1) * LANES)
                z = h_ref[toks, cols] + _sublane_sums([ya[s] for ya in yaccs], sub)
                ss = ss + z * z
                zs.append(z)
            inv = lax.rsqrt(jnp.sum(ss, axis=1, keepdims=True) * (1.0 / (nrow * LANES)) + EPS)
            for s in range(nrow):
                cols = slice(s * LANES, (s + 1) * LANES)
                o_ref[toks, cols] = zs[s] * inv * gf_ref[:, cols]
            yaccs = []

    @pl.when(i == n - 1)
    def _():
        for p in range(FFN_AHEAD):
            wait_slot(p)


def _peer_ffn(idx, gate_h, xn, h, gf, table, *, ctok=4):
    T, nexp = idx.shape
    D = xn.shape[1]
    nrow = D // LANES
    step = FFN_SLOTS * ctok
    assert nexp == PEER_E and SUBLANES % ctok == 0 and step % SUBLANES == 0 and T % step == 0
    assert table.shape[1:] == (nrow, LANES) and ROW_PITCH > nrow and 0 < FFN_AHEAD < FFN_SLOTS
    n = T // step
    last = n - 1
    est = FFN_SLOTS * ctok * nexp * ROW_PITCH * LANES * 4 + 6 * step * D * 4
    return pl.pallas_call(
        functools.partial(_peer_ffn_body, ctok=ctok, nrow=nrow),
        grid=(n,),
        in_specs=[
            pl.BlockSpec((step, nexp), lambda i: (i, 0), memory_space=pltpu.SMEM),
            pl.BlockSpec((step, nexp), lambda i: (jnp.minimum(i + 1, last), 0), memory_space=pltpu.SMEM),
            pl.BlockSpec((step // SUBLANES, SUBLANES, LANES), lambda i: (i, 0, 0)),
            pl.BlockSpec((step, D), lambda i: (i, 0)),
            pl.BlockSpec((step, D), lambda i: (i, 0)),
            pl.BlockSpec((1, D), lambda i: (0, 0)),
            pl.BlockSpec(memory_space=pl.ANY),
        ],
        out_specs=pl.BlockSpec((step, D), lambda i: (i, 0)),
        out_shape=jax.ShapeDtypeStruct((T, D), F32),
        scratch_shapes=[pltpu.VMEM((ctok * nexp * ROW_PITCH, LANES), jnp.uint32) for _ in range(FFN_SLOTS)]
                       + [pltpu.SemaphoreType.DMA((FFN_SLOTS,))],
        compiler_params=pltpu.CompilerParams(
            dimension_semantics=("arbitrary",),
            vmem_limit_bytes=_vmem_limit(est)),
        name="peer_ffn",
    )(idx, idx, gate_h, xn, h, gf, table)


def _pack_expert_table(u, v):
    ub = lax.bitcast_convert_type(u.astype(BF16), jnp.uint16).astype(jnp.uint32)
    vb = lax.bitcast_convert_type(v.astype(BF16), jnp.uint16).astype(jnp.uint32)
    n, d = u.shape
    return (ub | (vb << 16)).reshape(n, d // LANES, LANES)


def _rope_tables(seq):
    half = QK_ROPE // 2
    freqs = ROPE_THETA ** (-jnp.arange(half, dtype=F32) / half)
    ang = jnp.arange(seq, dtype=F32)[:, None] * freqs[None, :]
    cos, sin = jnp.cos(ang), jnp.sin(ang)
    zeros = jnp.zeros((seq, LANES - QK_ROPE), F32)
    return (jnp.concatenate([cos, cos, zeros], axis=1),
            jnp.concatenate([-sin, sin, zeros], axis=1))


def _layer(h, ln1_g, w_in, q_a_norm, kv_a_norm, w_uq, w_ukv, out_norm_dil, out_norm_mla, w_o,
           ln2_g, peer_wq, peer_sub_keys, peer_u, peer_v, final_g):
    B, S, D = h.shape
    T = B * S
    H = N_HEADS
    d_dil = H * HEAD_DIM
    q_lora = q_a_norm.shape[0]
    kv_lora = kv_a_norm.shape[0]
    x2 = h.reshape(T, D)

    d_in = w_in.shape[1]
    d_in_pad = -(-d_in // LANES) * LANES
    w_in_p = jnp.pad(w_in, ((0, 0), (0, d_in_pad - d_in))).astype(BF16)
    z = _norm_matmul(x2, ln1_g, w_in_p, tm=ROW_TILE,
                     tn=d_in_pad // 3 if d_in_pad % (3 * LANES) == 0 else LANES)
    z3 = z.reshape(B, S, d_in_pad)

    slopes = 2.0 ** (-8.0 * jnp.arange(1, H + 1, dtype=F32) / H)
    o_a = _dilated_attention(z3, slopes, n_heads=H)

    wq = w_uq.reshape(q_lora, H, QK_NOPE + QK_ROPE)
    wq_pe = jnp.pad(wq[:, :, QK_NOPE:], ((0, 0), (0, 0), (0, LANES - QK_ROPE)))
    wq_p = jnp.concatenate([wq[:, :, :QK_NOPE].reshape(q_lora, H * QK_NOPE),
                            wq_pe.reshape(q_lora, H * LANES)], axis=1).astype(BF16)
    wkv = w_ukv.reshape(kv_lora, H, QK_NOPE + HEAD_DIM)
    wkv_p = jnp.concatenate([wkv[:, :, :QK_NOPE].reshape(kv_lora, H * QK_NOPE),
                             wkv[:, :, QK_NOPE:].reshape(kv_lora, H * HEAD_DIM)], axis=1).astype(BF16)
    assert q_lora == kv_lora and (3 * d_dil) % q_lora == 0
    q_mla = _norm_matmul(z, q_a_norm, wq_p, xcol=3 * d_dil // q_lora, kdim=q_lora, out_dtype=BF16,
                         tm=ROW_TILE, tn=wq_p.shape[1])
    kv_mla = _norm_matmul(z, kv_a_norm, wkv_p, xcol=3 * d_dil // q_lora + 1, kdim=kv_lora, out_dtype=BF16,
                          tm=ROW_TILE, tn=wkv_p.shape[1])
    cos, sin_signed = _rope_tables(S)
    o_b = _mla_attention(q_mla.reshape(B, S, -1), kv_mla.reshape(B, S, -1), z3,
                         (3 * d_dil + q_lora + kv_lora) // LANES, cos, sin_signed, n_heads=H)

    h1 = _outproj(o_a.reshape(T, d_dil), o_b.reshape(T, -1), out_norm_dil, out_norm_mla,
                  w_o.astype(BF16), x2, tm=ROW_TILE)

    q_slabs, xn2 = _norm_matmul(h1, ln2_g, peer_wq.astype(BF16), tm=ROW_TILE, emit_xn=True,
                                slab_out=True, out_dtype=BF16)
    keys = peer_sub_keys.reshape(PEER_HEADS * 2, PEER_NKEYS, -1).astype(BF16)
    idx_t, gate_t = _peer_topk(q_slabs, keys)
    table = _pack_expert_table(peer_u, peer_v)
    gate_h = gate_t.reshape(GSIZE, NGROUP, T // GSIZE, GSIZE).transpose(2, 0, 3, 1).reshape(T // GSIZE, GSIZE, LANES)
    out = _peer_ffn(idx_t.T, gate_h, xn2, h1, final_g.reshape(1, D), table)
    return out.reshape(B, S, D)


def kernel(x, ln1_g, w_in, q_a_norm, kv_a_norm, w_uq, w_ukv, out_norm_dil, out_norm_mla, w_o,
           ln2_g, peer_wq, peer_sub_keys, peer_u, peer_v, lnf_g):
    assert ln1_g.shape[0] == 1, "single-layer trunk"
    return _layer(x, ln1_g[0], w_in[0], q_a_norm[0], kv_a_norm[0], w_uq[0], w_ukv[0],
                  out_norm_dil[0], out_norm_mla[0], w_o[0], ln2_g[0], peer_wq[0],
                  peer_sub_keys[0], peer_u[0], peer_v[0], lnf_g)
```

```python
import functools

import jax
import jax.numpy as jnp
from jax import lax
from jax.experimental import pallas as pl
from jax.experimental.pallas import tpu as pltpu

F32 = jnp.float32
BF16 = jnp.bfloat16

EPS = 1e-6
NEG = -1e30
HEAD_DIM = 128
BLOCK = 128
DIL_PATTERNS = ((128, 1), (512, 4), (2048, 16))
DIL_UNROLL = 16
N_HEADS = 8
QK_NOPE = 128
QK_ROPE = 64
ROPE_THETA = 10000.0
PEER_HEADS = 8
PEER_NKEYS = 128
PEER_TOPK = 16
PEER_E = PEER_HEADS * PEER_TOPK

LANES = 128
SUBLANES = 8
VMEM_CAP = 60000 * 1024
ROW_TILE = 1024


def _vmem_limit(nbytes):
    return int(min(VMEM_CAP, max(16 * 1024 * 1024, nbytes * 3 // 2)))


def _rms(x, g):
    return x * lax.rsqrt(jnp.mean(x * x, axis=-1, keepdims=True) + EPS) * g


def _norm_matmul_body(x_ref, g_ref, w_ref, *rest, emit_xn, slabs):
    if emit_xn:
        o_ref, xn_out_ref, xn_ref = rest
    else:
        o_ref, xn_ref = rest
    j = pl.program_id(1)

    @pl.when(j == 0)
    def _():
        xn = _rms(x_ref[...].astype(F32), g_ref[...])
        xn_ref[...] = xn.astype(BF16)
        if emit_xn:
            xn_out_ref[...] = xn

    res = jnp.dot(xn_ref[...], w_ref[...], preferred_element_type=F32).astype(o_ref.dtype)
    if slabs:
        for s in range(slabs):
            o_ref[s] = res[:, s * LANES:(s + 1) * LANES]
    else:
        o_ref[...] = res


def _norm_matmul(x, g, w, *, xcol=0, kdim=None, tm=512, tn=512, emit_xn=False, slab_out=False,
                 out_dtype=F32):
    T = x.shape[0]
    kdim = kdim or x.shape[1]
    N = w.shape[1]
    tm = min(tm, T)
    tn = min(tn, N)
    assert T % tm == 0 and N % tn == 0 and w.shape[0] == kdim
    slabs = tn // LANES if slab_out else 0
    if slab_out:
        out_shape = [jax.ShapeDtypeStruct((N // LANES, T, LANES), out_dtype)]
        out_specs = [pl.BlockSpec((slabs, tm, LANES), lambda i, j: (j, i, 0))]
    else:
        out_shape = [jax.ShapeDtypeStruct((T, N), out_dtype)]
        out_specs = [pl.BlockSpec((tm, tn), lambda i, j: (i, j))]
    if emit_xn:
        out_shape.append(jax.ShapeDtypeStruct((T, kdim), F32))
        out_specs.append(pl.BlockSpec((tm, kdim), lambda i, j: (i, 0)))
    est = 2 * (tm * kdim * 4 + kdim * tn * 2 + tm * tn * 4) + tm * kdim * 2
    if emit_xn:
        est += 2 * tm * kdim * 4
    outs = pl.pallas_call(
        functools.partial(_norm_matmul_body, emit_xn=emit_xn, slabs=slabs),
        grid=(T // tm, N // tn),
        in_specs=[
            pl.BlockSpec((tm, kdim), lambda i, j: (i, xcol)),
            pl.BlockSpec((1, kdim), lambda i, j: (0, 0)),
            pl.BlockSpec((kdim, tn), lambda i, j: (0, j)),
        ],
        out_specs=out_specs,
        out_shape=out_shape,
        scratch_shapes=[pltpu.VMEM((tm, kdim), BF16)],
        compiler_params=pltpu.CompilerParams(
            dimension_semantics=("arbitrary", "arbitrary"),
            vmem_limit_bytes=_vmem_limit(est)),
        name="norm_matmul",
    )(x, g.reshape(1, kdim).astype(F32), w)
    return outs if emit_xn else outs[0]


def _dilated_body(slopes_ref, q_ref, k_ref, v_ref, o_ref, o_scr, l_scr, *, seq, patterns, scale):
    h = pl.program_id(1)
    slope = slopes_ref[h]
    qi = lax.broadcasted_iota(jnp.int32, (BLOCK, 2 * BLOCK), 0)
    kj = lax.broadcasted_iota(jnp.int32, (BLOCK, 2 * BLOCK), 1)
    delta = qi + BLOCK - kj
    nt = (((1,), (1,)), ((), ()))

    for p, (window, d) in enumerate(patterns):
        steps = window // d
        nb = seq // d // BLOCK
        in_window = (delta >= 0) & (delta <= steps)
        bias = -slope * (delta * d).astype(F32)

        def block(t, carry, d=d, nb=nb, in_window=in_window, bias=bias, p=p):
            r = t // nb
            n = t % nb
            start = n * (BLOCK * d) + r
            pstart = jnp.maximum(n - 1, 0) * (BLOCK * d) + r
            rows = pl.ds(start, BLOCK, stride=d) if d > 1 else pl.ds(start, BLOCK)
            prow = pl.ds(pstart, BLOCK, stride=d) if d > 1 else pl.ds(pstart, BLOCK)
            q = q_ref[0, rows, :].astype(BF16)
            kk = jnp.concatenate([k_ref[0, prow, :], k_ref[0, rows, :]], axis=0).astype(BF16)
            vv = jnp.concatenate([v_ref[0, prow, :], v_ref[0, rows, :]], axis=0).astype(BF16)
            s = lax.dot_general(q, kk, nt, preferred_element_type=F32) * scale
            valid = in_window & ((kj >= BLOCK) | (n > 0))
            s = jnp.where(valid, s + bias, NEG)
            m = jnp.max(s, axis=-1, keepdims=True)
            e = jnp.exp(s - m)
            l = jnp.sum(e, axis=-1, keepdims=True)
            o = jnp.dot((e / l).astype(BF16), vv, preferred_element_type=F32)
            o_scr[p, rows, :] = o
            l_scr[p, rows, :] = jnp.broadcast_to(m + jnp.log(l), (BLOCK, HEAD_DIM))
            return carry

        lax.fori_loop(0, d * nb, block, 0, unroll=DIL_UNROLL)

    def mix(c, carry):
        rows = pl.ds(pl.multiple_of(c * BLOCK, BLOCK), BLOCK)
        ls = [l_scr[p, rows, :] for p in range(len(patterns))]
        m = functools.reduce(jnp.maximum, ls)
        es = [jnp.exp(l - m) for l in ls]
        den = functools.reduce(jnp.add, es)
        num = functools.reduce(jnp.add, [e * o_scr[p, rows, :] for p, e in enumerate(es)])
        o_ref[0, rows, :] = (num / den).astype(o_ref.dtype)
        return carry

    lax.fori_loop(0, seq // BLOCK, mix, 0)


def _dilated_attention(z3, slopes, *, n_heads, patterns=DIL_PATTERNS):
    B, S, _ = z3.shape
    for _, d in patterns:
        assert S % (d * BLOCK) == 0
    blk = (1, S, HEAD_DIM)
    est = 2 * 4 * S * HEAD_DIM * 4 + 2 * len(patterns) * S * HEAD_DIM * 4
    return pl.pallas_call(
        functools.partial(_dilated_body, seq=S, patterns=patterns, scale=HEAD_DIM ** -0.5),
        grid=(B, n_heads),
        in_specs=[
            pl.BlockSpec(memory_space=pltpu.SMEM),
            pl.BlockSpec(blk, lambda b, h: (b, 0, h)),
            pl.BlockSpec(blk, lambda b, h: (b, 0, n_heads + h)),
            pl.BlockSpec(blk, lambda b, h: (b, 0, 2 * n_heads + h)),
        ],
        out_specs=pl.BlockSpec(blk, lambda b, h: (b, 0, h)),
        out_shape=jax.ShapeDtypeStruct((B, S, n_heads * HEAD_DIM), BF16),
        scratch_shapes=[pltpu.VMEM((len(patterns), S, HEAD_DIM), F32),
                        pltpu.VMEM((len(patterns), S, HEAD_DIM), F32)],
        compiler_params=pltpu.CompilerParams(
            dimension_semantics=("arbitrary", "arbitrary"),
            vmem_limit_bytes=_vmem_limit(est)),
        name="dilated_attention",
    )(slopes, z3, z3, z3)


def _rope(x, cos, sin_signed):
    lane = lax.broadcasted_iota(jnp.int32, x.shape, 1)
    half = QK_ROPE // 2
    swapped = jnp.where(lane < half, pltpu.roll(x, LANES - half, 1), pltpu.roll(x, half, 1))
    return x * cos + swapped * sin_signed


def _mla_body(qn_ref, qp_ref, kn_ref, v_ref, kr_ref, cos_ref, sin_ref,
              o_ref, qcat_scr, kcat_scr, *, seq, tq, scale):
    cos, sin = cos_ref[...], sin_ref[...]
    qcat_scr[:, :LANES] = qn_ref[0]
    qcat_scr[:, LANES:] = _rope(qp_ref[0].astype(F32), cos, sin).astype(BF16)
    kcat_scr[:, :LANES] = kn_ref[0]
    kcat_scr[:, LANES:] = _rope(kr_ref[0], cos, sin).astype(BF16)
    nt = (((1,), (1,)), ((), ()))

    for qi in range(seq // tq):
        ext = (qi + 1) * tq
        rows = slice(qi * tq, ext)
        s = lax.dot_general(qcat_scr[rows, :], kcat_scr[:ext, :], nt, preferred_element_type=F32) * scale
        qpos = qi * tq + lax.broadcasted_iota(jnp.int32, (tq, ext), 0)
        kpos = lax.broadcasted_iota(jnp.int32, (tq, ext), 1)
        s = jnp.where(kpos <= qpos, s, NEG)
        e = jnp.exp(s - jnp.max(s, axis=-1, keepdims=True))
        l = jnp.sum(e, axis=-1, keepdims=True)
        o = jnp.dot(e.astype(BF16), v_ref[0, :ext, :], preferred_element_type=F32) / l
        o_ref[0, rows, :] = o.astype(o_ref.dtype)


def _mla_attention(q3, kv3, z3, kr_col, cos, sin_signed, *, n_heads, tq=512):
    B, S, _ = q3.shape
    tq = min(tq, S)
    assert S % tq == 0 and q3.dtype == BF16 and kv3.dtype == BF16
    blk = (1, S, LANES)
    est = 2 * (5 * 2 + 3 * 4) * S * LANES + 4 * S * LANES * 2 + 3 * tq * S * 4
    return pl.pallas_call(
        functools.partial(_mla_body, seq=S, tq=tq, scale=(QK_NOPE + QK_ROPE) ** -0.5),
        grid=(B, n_heads),
        in_specs=[
            pl.BlockSpec(blk, lambda b, h: (b, 0, h)),
            pl.BlockSpec(blk, lambda b, h: (b, 0, n_heads + h)),
            pl.BlockSpec(blk, lambda b, h: (b, 0, h)),
            pl.BlockSpec(blk, lambda b, h: (b, 0, n_heads + h)),
            pl.BlockSpec(blk, lambda b, h: (b, 0, kr_col)),
            pl.BlockSpec((S, LANES), lambda b, h: (0, 0)),
            pl.BlockSpec((S, LANES), lambda b, h: (0, 0)),
        ],
        out_specs=pl.BlockSpec(blk, lambda b, h: (b, 0, h)),
        out_shape=jax.ShapeDtypeStruct((B, S, n_heads * HEAD_DIM), BF16),
        scratch_shapes=[pltpu.VMEM((S, 2 * LANES), BF16), pltpu.VMEM((S, 2 * LANES), BF16)],
        compiler_params=pltpu.CompilerParams(
            dimension_semantics=("arbitrary", "arbitrary"),
            vmem_limit_bytes=_vmem_limit(est)),
        name="mla_attention",
    )(q3, q3, kv3, kv3, z3, cos, sin_signed)


def _outproj_body(oa_ref, ob_ref, ga_ref, gb_ref, w_ref, x_ref, o_ref, xn_ref, *, da):
    j = pl.program_id(1)

    @pl.when(j == 0)
    def _():
        xn_ref[:, :da] = _rms(oa_ref[...].astype(F32), ga_ref[...]).astype(BF16)
        xn_ref[:, da:] = _rms(ob_ref[...].astype(F32), gb_ref[...]).astype(BF16)

    o_ref[...] = x_ref[...] + jnp.dot(xn_ref[...], w_ref[...], preferred_element_type=F32)


def _outproj(oa, ob, ga, gb, w, x, *, tm=512, tn=512):
    T, da = oa.shape
    db = ob.shape[1]
    N = w.shape[1]
    tm, tn = min(tm, T), min(tn, N)
    assert T % tm == 0 and N % tn == 0 and w.shape[0] == da + db
    est = 2 * (tm * (da + db) * 4 + (da + db) * tn * 2 + 2 * tm * tn * 4) + tm * (da + db) * 2
    return pl.pallas_call(
        functools.partial(_outproj_body, da=da),
        grid=(T // tm, N // tn),
        in_specs=[
            pl.BlockSpec((tm, da), lambda i, j: (i, 0)),
            pl.BlockSpec((tm, db), lambda i, j: (i, 0)),
            pl.BlockSpec((1, da), lambda i, j: (0, 0)),
            pl.BlockSpec((1, db), lambda i, j: (0, 0)),
            pl.BlockSpec((da + db, tn), lambda i, j: (0, j)),
            pl.BlockSpec((tm, tn), lambda i, j: (i, j)),
        ],
        out_specs=pl.BlockSpec((tm, tn), lambda i, j: (i, j)),
        out_shape=jax.ShapeDtypeStruct((T, N), F32),
        scratch_shapes=[pltpu.VMEM((tm, da + db), BF16)],
        compiler_params=pltpu.CompilerParams(
            dimension_semantics=("arbitrary", "arbitrary"),
            vmem_limit_bytes=_vmem_limit(est)),
        name="out_projection",
    )(oa, ob, ga.reshape(1, da), gb.reshape(1, db), w, x)


TOPK_HEADS = 2


def _candidate_blocks(topk):
    blocks = []
    a = 0
    while a < topk:
        nb = topk // (a + 1)
        if nb >= SUBLANES:
            blocks += [(a, 1, b0, min(SUBLANES, nb - b0)) for b0 in range(0, nb, SUBLANES)]
            a += 1
        elif nb > 1:
            blocks.append((a, 1, 0, nb))
            a += 1
        else:
            na = min(SUBLANES, topk - a)
            blocks.append((a, na, 0, 1))
            a += na
    return blocks


def _topk_body(q_ref, keys_ref, idx_ref, gate_ref, s_ref, sv_ref, si_ref, cs_ref, ci_ref, bs_ref, be_ref,
               *, tt, nkeys, topk, blocks):
    lowest = float(jnp.finfo(jnp.float32).min)
    nchain = 2 * TOPK_HEADS
    iota_n = lax.broadcasted_iota(jnp.int32, (nkeys, tt), 0)
    sub = lax.broadcasted_iota(jnp.int32, (SUBLANES, tt), 0)
    nt = (((1,), (1,)), ((), ()))

    for ch in range(nchain):
        s_ref[ch] = lax.dot_general(keys_ref[ch], q_ref[ch], nt, preferred_element_type=F32)

    def pick(k, carry):
        for ch in range(nchain):
            s = s_ref[ch]
            m = jnp.max(s, axis=0, keepdims=True)
            ix = jnp.min(jnp.where(s == m, iota_n, nkeys), axis=0, keepdims=True)
            sv_ref[ch, pl.ds(k, 1), :] = m
            si_ref[ch, pl.ds(k, 1), :] = ix
            s_ref[ch] = jnp.where(iota_n == ix, lowest, s)
        return carry

    lax.fori_loop(0, topk, pick, 0)

    pos_blocks = []
    for r, (a0, na, b0, nb) in enumerate(blocks):
        rows = slice(r * SUBLANES, (r + 1) * SUBLANES)
        if na == 1:
            live = sub < nb
            pos_blocks.append(a0 * topk + b0 + sub)
        else:
            live = sub < na
            pos_blocks.append((a0 + sub) * topk)
        for hd in range(TOPK_HEADS):
            if na == 1:
                cs = sv_ref[2 * hd, a0:a0 + 1, :] + sv_ref[2 * hd + 1, b0:b0 + SUBLANES, :]
                ci = si_ref[2 * hd, a0:a0 + 1, :] * nkeys + si_ref[2 * hd + 1, b0:b0 + SUBLANES, :]
            else:
                cs = sv_ref[2 * hd, a0:a0 + SUBLANES, :] + sv_ref[2 * hd + 1, 0:1, :]
                ci = si_ref[2 * hd, a0:a0 + SUBLANES, :] * nkeys + si_ref[2 * hd + 1, 0:1, :]
            cs_ref[hd, rows, :] = jnp.where(live, cs, lowest)
            ci_ref[hd, rows, :] = ci
    pos = jnp.concatenate(pos_blocks, axis=0)
    big = topk * topk

    def pick2(k, carry):
        for hd in range(TOPK_HEADS):
            c = cs_ref[hd]
            m = jnp.max(c, axis=0, keepdims=True)
            first = jnp.min(jnp.where(c == m, pos, big), axis=0, keepdims=True)
            hit = pos == first
            bs_ref[hd, pl.ds(k, 1), :] = m
            be_ref[hd, pl.ds(k, 1), :] = jnp.max(jnp.where(hit, ci_ref[hd], -1), axis=0, keepdims=True)
            cs_ref[hd] = jnp.where(hit, lowest, c)
        return carry

    lax.fori_loop(0, topk, pick2, 0)

    for hd in range(TOPK_HEADS):
        b = bs_ref[hd]
        e = jnp.exp(b - jnp.max(b, axis=0, keepdims=True))
        gate_ref[hd * topk:(hd + 1) * topk, :] = e / jnp.sum(e, axis=0, keepdims=True)
        idx_ref[hd * topk:(hd + 1) * topk, :] = be_ref[hd]


def _peer_topk(q_slabs, keys, *, tt=128):
    hp, T, c = q_slabs.shape
    heads = hp // 2
    nkeys = keys.shape[1]
    tt = min(tt, T)
    topk = PEER_TOPK
    assert T % tt == 0 and heads % TOPK_HEADS == 0 and topk % SUBLANES == 0
    blocks = _candidate_blocks(topk)
    ncand = len(blocks) * SUBLANES
    nchain = 2 * TOPK_HEADS
    rows = TOPK_HEADS * topk
    return pl.pallas_call(
        functools.partial(_topk_body, tt=tt, nkeys=nkeys, topk=topk, blocks=blocks),
        grid=(T // tt, heads // TOPK_HEADS),
        in_specs=[
            pl.BlockSpec((nchain, tt, c), lambda i, h: (h, i, 0)),
            pl.BlockSpec((nchain, nkeys, c), lambda i, h: (h, 0, 0)),
        ],
        out_specs=[pl.BlockSpec((rows, tt), lambda i, h: (h, i)),
                   pl.BlockSpec((rows, tt), lambda i, h: (h, i))],
        out_shape=[jax.ShapeDtypeStruct((heads * topk, T), jnp.int32),
                   jax.ShapeDtypeStruct((heads * topk, T), F32)],
        scratch_shapes=[pltpu.VMEM((nchain, nkeys, tt), F32),
                        pltpu.VMEM((nchain, topk, tt), F32), pltpu.VMEM((nchain, topk, tt), jnp.int32),
                        pltpu.VMEM((TOPK_HEADS, ncand, tt), F32), pltpu.VMEM((TOPK_HEADS, ncand, tt), jnp.int32),
                        pltpu.VMEM((TOPK_HEADS, topk, tt), F32), pltpu.VMEM((TOPK_HEADS, topk, tt), jnp.int32)],
        compiler_params=pltpu.CompilerParams(dimension_semantics=("arbitrary", "arbitrary")),
        name="peer_topk",
    )(q_slabs, keys)


GSIZE = SUBLANES
NGROUP = PEER_E // GSIZE
ROW_PITCH = 17
FFN_SLOTS = 4
FFN_PHASES = 8
FFN_AHEAD = 2


def _gelu_exact(x):
    return 0.5 * x * (1.0 + lax.erf(x * (2.0 ** -0.5)))


def _sublane_sums(vs, sub):
    lo = sub < 4
    halves = []
    for j in range(4):
        a, b = vs[j], vs[j + 4]
        halves.append(jnp.where(lo, a, b) + pltpu.roll(jnp.where(lo, b, a), 4, 0))
    done = []
    for j, (sa, sb) in enumerate(((6, 7), (6, 1), (2, 7), (2, 1))):
        c = halves[j] + pltpu.roll(halves[j], sa, 0)
        done.append(c + pltpu.roll(c, sb, 0))
    q = sub & 3
    return jnp.where(q == 0, done[0], jnp.where(q == 1, done[1], jnp.where(q == 2, done[2], done[3])))


def _peer_ffn_body(idx_hbm, gate_ref, x_ref, h_ref, gf_ref, tab_ref, o_ref, *scratch,
                   ctok, nrow, ntok):
    bufs = scratch[:FFN_SLOTS]
    idx_a, idx_b, sem, isem = scratch[FFN_SLOTS:]
    i = pl.program_id(0)
    n = pl.num_programs(0)
    sub = lax.broadcasted_iota(jnp.int32, (SUBLANES, LANES), 0)
    lane = lax.broadcasted_iota(jnp.int32, (SUBLANES, LANES), 1)
    hi_mask = jnp.uint32(0xFFFF0000)
    step = FFN_PHASES * ctok
    half = step // 2

    def idx_copy(part, base):
        lo, cnt, dst = ((0, half // 2, idx_a), (half // 2, half // 2, idx_a), (half, half, idx_b))[part]
        src = idx_hbm.at[pl.ds(pl.multiple_of(base + lo, SUBLANES), cnt), :]
        return pltpu.make_async_copy(src, dst.at[pl.ds(lo % half, cnt), :], isem.at[part])

    next_base = jnp.minimum((i + 1) * step, ntok - step)

    def idx_of(tok, j):
        return idx_a[tok, j] if tok < half else idx_b[tok - half, j]

    def start_row(tok, dst_slot, c, j):
        k, g = divmod(j, NGROUP)
        first_row = ((c * NGROUP + g) * GSIZE + k) * ROW_PITCH
        pltpu.make_async_copy(tab_ref.at[idx_of(tok, j)], bufs[dst_slot].at[pl.ds(first_row, nrow), :],
                              sem.at[dst_slot]).start(priority=j % 2)

    def wait_slot(s):
        copied = pl.ds(0, ctok * PEER_E * nrow)
        pltpu.make_async_copy(bufs[s].at[copied, :], bufs[s].at[copied, :], sem.at[s]).wait()

    def group_row(slot, c, g, s):
        return bufs[slot][pl.ds((c * NGROUP + g) * GSIZE * ROW_PITCH + s, GSIZE, stride=ROW_PITCH), :]

    @pl.when(i == 0)
    def _():
        first_ids = idx_copy(0, 0)
        first_ids.start()
        first_ids.wait()
        idx_copy(1, 0).start()
        idx_copy(2, 0).start()
        for p in range(FFN_AHEAD):
            def first(c, carry, p=p):
                for j in range(PEER_E):
                    k, g = divmod(j, NGROUP)
                    first_row = ((c * NGROUP + g) * GSIZE + k) * ROW_PITCH
                    pltpu.make_async_copy(tab_ref.at[idx_a[p * ctok + c, j]],
                                          bufs[p].at[pl.ds(first_row, nrow), :], sem.at[p]).start(priority=j % 2)
                return carry
            lax.fori_loop(0, ctok, first, 0)

    idx_copy(0, next_base).start()
    idx_copy(1, 0).wait()
    yaccs = []
    for p in range(FFN_PHASES):
        if p * ctok == half // 2:
            idx_copy(1, next_base).start()
            idx_copy(2, 0).wait()
        if (p + FFN_AHEAD) * ctok == step:
            idx_copy(2, next_base).start()
            idx_copy(0, 0).wait()
        slot = p % FFN_SLOTS
        wait_slot(slot)
        ahead_slot = (p + FFN_AHEAD) % FFN_SLOTS
        ahead_tok0 = ((p + FFN_AHEAD) % FFN_PHASES) * ctok
        for c in range(ctok):
            tok = p * ctok + c
            gate_h = gate_ref[tok // SUBLANES]
            lane0 = (tok % SUBLANES) * NGROUP
            nxt = lambda j, c=c: start_row(ahead_tok0 + c, ahead_slot, c, j)
            xb = [jnp.broadcast_to(x_ref[tok:tok + 1, s * LANES:(s + 1) * LANES], (SUBLANES, LANES))
                  for s in range(nrow)]
            dots = jnp.zeros((SUBLANES, LANES), F32)
            for g in range(NGROUP):
                for j in range(g * 4, g * 4 + 4):
                    nxt(j)
                r = None
                for s in range(nrow):
                    w = group_row(slot, c, g, s)
                    q = pltpu.bitcast(w << 16, F32) * xb[s]
                    r = q if r is None else r + q
                d = jnp.sum(r, axis=1, keepdims=True)
                dots = jnp.where(lane == lane0 + g, d, dots)
            act = _gelu_exact(dots) * gate_h
            yacc = [None] * nrow
            for g in range(NGROUP):
                for j in range(PEER_E // 2 + g * 4, PEER_E // 2 + g * 4 + 4):
                    nxt(j)
                a = jnp.broadcast_to(act[:, lane0 + g:lane0 + g + 1], (SUBLANES, LANES))
                for s in range(nrow):
                    w = group_row(slot, c, g, s)
                    t = a * pltpu.bitcast(w & hi_mask, F32)
                    yacc[s] = t if yacc[s] is None else yacc[s] + t
            yaccs.append(yacc)
        if len(yaccs) == SUBLANES:
            last_tok = (p + 1) * ctok
            toks = slice(last_tok - SUBLANES, last_tok)
            zs = []
            ss = jnp.zeros((SUBLANES, LANES), F32)
            for s in range(nrow):
                cols = slice(s * LANES, (s + 1) * LANES)
                z = h_ref[toks, cols] + _sublane_sums([ya[s] for ya in yaccs], sub)
                ss = ss + z * z
                zs.append(z)
            inv = lax.rsqrt(jnp.sum(ss, axis=1, keepdims=True) * (1.0 / (nrow * LANES)) + EPS)
            for s in range(nrow):
                cols = slice(s * LANES, (s + 1) * LANES)
                o_ref[toks, cols] = zs[s] * inv * gf_ref[:, cols]
            yaccs = []

    @pl.when(i == n - 1)
    def _():
        for p in range(FFN_AHEAD):
            wait_slot(p)
        idx_copy(1, 0).wait()
        idx_copy(2, 0).wait()


def _peer_ffn(idx, gate_h, xn, h, gf, table, *, ctok=4):
    T, nexp = idx.shape
    D = xn.shape[1]
    nrow = D // LANES
    step = FFN_PHASES * ctok
    half = step // 2
    assert nexp == PEER_E and SUBLANES % ctok == 0 and half % (2 * SUBLANES) == 0 and T % step == 0 and T >= 2 * step
    assert table.shape[1:] == (nrow, LANES) and ROW_PITCH > nrow
    assert FFN_PHASES % FFN_SLOTS == 0 and 0 < FFN_AHEAD < FFN_SLOTS and FFN_AHEAD * ctok == half // 2
    n = T // step
    est = FFN_SLOTS * ctok * nexp * ROW_PITCH * LANES * 4 + 6 * step * D * 4
    return pl.pallas_call(
        functools.partial(_peer_ffn_body, ctok=ctok, nrow=nrow, ntok=T),
        grid=(n,),
        in_specs=[
            pl.BlockSpec(memory_space=pl.ANY),
            pl.BlockSpec((step // SUBLANES, SUBLANES, LANES), lambda i: (i, 0, 0)),
            pl.BlockSpec((step, D), lambda i: (i, 0)),
            pl.BlockSpec((step, D), lambda i: (i, 0)),
            pl.BlockSpec((1, D), lambda i: (0, 0)),
            pl.BlockSpec(memory_space=pl.ANY),
        ],
        out_specs=pl.BlockSpec((step, D), lambda i: (i, 0)),
        out_shape=jax.ShapeDtypeStruct((T, D), F32),
        scratch_shapes=[pltpu.VMEM((ctok * nexp * ROW_PITCH, LANES), jnp.uint32) for _ in range(FFN_SLOTS)]
                       + [pltpu.SMEM((half, nexp), jnp.int32), pltpu.SMEM((half, nexp), jnp.int32),
                          pltpu.SemaphoreType.DMA((FFN_SLOTS,)), pltpu.SemaphoreType.DMA((3,))],
        compiler_params=pltpu.CompilerParams(
            dimension_semantics=("arbitrary",),
            vmem_limit_bytes=_vmem_limit(est)),
        name="peer_ffn",
    )(idx, gate_h, xn, h, gf, table)


def _pack_expert_table(u, v):
    ub = lax.bitcast_convert_type(u.astype(BF16), jnp.uint16).astype(jnp.uint32)
    vb = lax.bitcast_convert_type(v.astype(BF16), jnp.uint16).astype(jnp.uint32)
    n, d = u.shape
    return (ub | (vb << 16)).reshape(n, d // LANES, LANES)


def _rope_tables(seq):
    half = QK_ROPE // 2
    freqs = ROPE_THETA ** (-jnp.arange(half, dtype=F32) / half)
    ang = jnp.arange(seq, dtype=F32)[:, None] * freqs[None, :]
    cos, sin = jnp.cos(ang), jnp.sin(ang)
    zeros = jnp.zeros((seq, LANES - QK_ROPE), F32)
    return (jnp.concatenate([cos, cos, zeros], axis=1),
            jnp.concatenate([-sin, sin, zeros], axis=1))


def _layer(h, ln1_g, w_in, q_a_norm, kv_a_norm, w_uq, w_ukv, out_norm_dil, out_norm_mla, w_o,
           ln2_g, peer_wq, peer_sub_keys, peer_u, peer_v, final_g):
    B, S, D = h.shape
    T = B * S
    H = N_HEADS
    d_dil = H * HEAD_DIM
    q_lora = q_a_norm.shape[0]
    kv_lora = kv_a_norm.shape[0]
    x2 = h.reshape(T, D)

    d_in = w_in.shape[1]
    d_in_pad = -(-d_in // LANES) * LANES
    w_in_p = jnp.pad(w_in, ((0, 0), (0, d_in_pad - d_in))).astype(BF16)
    z = _norm_matmul(x2, ln1_g, w_in_p, tm=ROW_TILE,
                     tn=d_in_pad // 3 if d_in_pad % (3 * LANES) == 0 else LANES)
    z3 = z.reshape(B, S, d_in_pad)

    slopes = 2.0 ** (-8.0 * jnp.arange(1, H + 1, dtype=F32) / H)
    o_a = _dilated_attention(z3, slopes, n_heads=H)

    wq = w_uq.reshape(q_lora, H, QK_NOPE + QK_ROPE)
    wq_pe = jnp.pad(wq[:, :, QK_NOPE:], ((0, 0), (0, 0), (0, LANES - QK_ROPE)))
    wq_p = jnp.concatenate([wq[:, :, :QK_NOPE].reshape(q_lora, H * QK_NOPE),
                            wq_pe.reshape(q_lora, H * LANES)], axis=1).astype(BF16)
    wkv = w_ukv.reshape(kv_lora, H, QK_NOPE + HEAD_DIM)
    wkv_p = jnp.concatenate([wkv[:, :, :QK_NOPE].reshape(kv_lora, H * QK_NOPE),
                             wkv[:, :, QK_NOPE:].reshape(kv_lora, H * HEAD_DIM)], axis=1).astype(BF16)
    assert q_lora == kv_lora and (3 * d_dil) % q_lora == 0
    q_mla = _norm_matmul(z, q_a_norm, wq_p, xcol=3 * d_dil // q_lora, kdim=q_lora, out_dtype=BF16,
                         tm=ROW_TILE, tn=wq_p.shape[1])
    kv_mla = _norm_matmul(z, kv_a_norm, wkv_p, xcol=3 * d_dil // q_lora + 1, kdim=kv_lora, out_dtype=BF16,
                          tm=ROW_TILE, tn=wkv_p.shape[1])
    cos, sin_signed = _rope_tables(S)
    o_b = _mla_attention(q_mla.reshape(B, S, -1), kv_mla.reshape(B, S, -1), z3,
                         (3 * d_dil + q_lora + kv_lora) // LANES, cos, sin_signed, n_heads=H)

    h1 = _outproj(o_a.reshape(T, d_dil), o_b.reshape(T, -1), out_norm_dil, out_norm_mla,
                  w_o.astype(BF16), x2, tm=ROW_TILE)

    q_slabs, xn2 = _norm_matmul(h1, ln2_g, peer_wq.astype(BF16), tm=ROW_TILE, emit_xn=True,
                                slab_out=True, out_dtype=BF16)
    keys = peer_sub_keys.reshape(PEER_HEADS * 2, PEER_NKEYS, -1).astype(BF16)
    idx_t, gate_t = _peer_topk(q_slabs, keys)
    table = _pack_expert_table(peer_u, peer_v)
    gate_h = gate_t.reshape(GSIZE, NGROUP, T // GSIZE, GSIZE).transpose(2, 0, 3, 1).reshape(T // GSIZE, GSIZE, LANES)
    out = _peer_ffn(idx_t.T, gate_h, xn2, h1, final_g.reshape(1, D), table)
    return out.reshape(B, S, D)


def kernel(x, ln1_g, w_in, q_a_norm, kv_a_norm, w_uq, w_ukv, out_norm_dil, out_norm_mla, w_o,
           ln2_g, peer_wq, peer_sub_keys, peer_u, peer_v, lnf_g):
    assert ln1_g.shape[0] == 1, "single-layer trunk"
    return _layer(x, ln1_g[0], w_in[0], q_a_norm[0], kv_a_norm[0], w_uq[0], w_ukv[0],
                  out_norm_dil[0], out_norm_mla[0], w_o[0], ln2_g[0], peer_wq[0],
                  peer_sub_keys[0], peer_u[0], peer_v[0], lnf_g)
```

```python
import functools

import jax
import jax.numpy as jnp
from jax import lax
from jax.experimental import pallas as pl
from jax.experimental.pallas import tpu as pltpu

F32 = jnp.float32
BF16 = jnp.bfloat16

EPS = 1e-6
NEG = -1e30
HEAD_DIM = 128
BLOCK = 128
DIL_PATTERNS = ((128, 1), (512, 4), (2048, 16))
DIL_UNROLL = 16
N_HEADS = 8
QK_NOPE = 128
QK_ROPE = 64
ROPE_THETA = 10000.0
PEER_HEADS = 8
PEER_NKEYS = 128
PEER_TOPK = 16
PEER_E = PEER_HEADS * PEER_TOPK

LANES = 128
SUBLANES = 8
VMEM_CAP = 60000 * 1024
ROW_TILE = 1024


def _vmem_limit(nbytes):
    return int(min(VMEM_CAP, max(16 * 1024 * 1024, nbytes * 3 // 2)))


def _rms(x, g):
    return x * lax.rsqrt(jnp.mean(x * x, axis=-1, keepdims=True) + EPS) * g


def _norm_matmul_body(x_ref, g_ref, w_ref, *rest, emit_xn, slabs):
    if emit_xn:
        o_ref, xn_out_ref, xn_ref = rest
    else:
        o_ref, xn_ref = rest
    j = pl.program_id(1)

    @pl.when(j == 0)
    def _():
        xn = _rms(x_ref[...].astype(F32), g_ref[...])
        xn_ref[...] = xn.astype(BF16)
        if emit_xn:
            xn_out_ref[...] = xn

    res = jnp.dot(xn_ref[...], w_ref[...], preferred_element_type=F32).astype(o_ref.dtype)
    if slabs:
        for s in range(slabs):
            o_ref[s] = res[:, s * LANES:(s + 1) * LANES]
    else:
        o_ref[...] = res


def _norm_matmul(x, g, w, *, xcol=0, kdim=None, tm=512, tn=512, emit_xn=False, slab_out=False,
                 out_dtype=F32):
    T = x.shape[0]
    kdim = kdim or x.shape[1]
    N = w.shape[1]
    tm = min(tm, T)
    tn = min(tn, N)
    assert T % tm == 0 and N % tn == 0 and w.shape[0] == kdim
    slabs = tn // LANES if slab_out else 0
    if slab_out:
        out_shape = [jax.ShapeDtypeStruct((N // LANES, T, LANES), out_dtype)]
        out_specs = [pl.BlockSpec((slabs, tm, LANES), lambda i, j: (j, i, 0))]
    else:
        out_shape = [jax.ShapeDtypeStruct((T, N), out_dtype)]
        out_specs = [pl.BlockSpec((tm, tn), lambda i, j: (i, j))]
    if emit_xn:
        out_shape.append(jax.ShapeDtypeStruct((T, kdim), F32))
        out_specs.append(pl.BlockSpec((tm, kdim), lambda i, j: (i, 0)))
    est = 2 * (tm * kdim * 4 + kdim * tn * 2 + tm * tn * 4) + tm * kdim * 2
    if emit_xn:
        est += 2 * tm * kdim * 4
    outs = pl.pallas_call(
        functools.partial(_norm_matmul_body, emit_xn=emit_xn, slabs=slabs),
        grid=(T // tm, N // tn),
        in_specs=[
            pl.BlockSpec((tm, kdim), lambda i, j: (i, xcol)),
            pl.BlockSpec((1, kdim), lambda i, j: (0, 0)),
            pl.BlockSpec((kdim, tn), lambda i, j: (0, j)),
        ],
        out_specs=out_specs,
        out_shape=out_shape,
        scratch_shapes=[pltpu.VMEM((tm, kdim), BF16)],
        compiler_params=pltpu.CompilerParams(
            dimension_semantics=("arbitrary", "arbitrary"),
            vmem_limit_bytes=_vmem_limit(est)),
        name="norm_matmul",
    )(x, g.reshape(1, kdim).astype(F32), w)
    return outs if emit_xn else outs[0]


def _dilated_body(slopes_ref, q_ref, k_ref, v_ref, o_ref, o_scr, l_scr, *, seq, patterns, scale):
    h = pl.program_id(1)
    slope = slopes_ref[h]
    qi = lax.broadcasted_iota(jnp.int32, (BLOCK, 2 * BLOCK), 0)
    kj = lax.broadcasted_iota(jnp.int32, (BLOCK, 2 * BLOCK), 1)
    delta = qi + BLOCK - kj
    nt = (((1,), (1,)), ((), ()))

    for p, (window, d) in enumerate(patterns):
        steps = window // d
        nb = seq // d // BLOCK
        in_window = (delta >= 0) & (delta <= steps)
        bias = -slope * (delta * d).astype(F32)

        def block(t, carry, d=d, nb=nb, in_window=in_window, bias=bias, p=p):
            r = t // nb
            n = t % nb
            start = n * (BLOCK * d) + r
            pstart = jnp.maximum(n - 1, 0) * (BLOCK * d) + r
            rows = pl.ds(start, BLOCK, stride=d) if d > 1 else pl.ds(start, BLOCK)
            prow = pl.ds(pstart, BLOCK, stride=d) if d > 1 else pl.ds(pstart, BLOCK)
            q = q_ref[0, rows, :].astype(BF16)
            kk = jnp.concatenate([k_ref[0, prow, :], k_ref[0, rows, :]], axis=0).astype(BF16)
            vv = jnp.concatenate([v_ref[0, prow, :], v_ref[0, rows, :]], axis=0).astype(BF16)
            s = lax.dot_general(q, kk, nt, preferred_element_type=F32) * scale
            valid = in_window & ((kj >= BLOCK) | (n > 0))
            s = jnp.where(valid, s + bias, NEG)
            m = jnp.max(s, axis=-1, keepdims=True)
            e = jnp.exp(s - m)
            l = jnp.sum(e, axis=-1, keepdims=True)
            o = jnp.dot((e / l).astype(BF16), vv, preferred_element_type=F32)
            o_scr[p, rows, :] = o
            l_scr[p, rows, :] = jnp.broadcast_to(m + jnp.log(l), (BLOCK, HEAD_DIM))
            return carry

        lax.fori_loop(0, d * nb, block, 0, unroll=DIL_UNROLL)

    def mix(c, carry):
        rows = pl.ds(pl.multiple_of(c * BLOCK, BLOCK), BLOCK)
        ls = [l_scr[p, rows, :] for p in range(len(patterns))]
        m = functools.reduce(jnp.maximum, ls)
        es = [jnp.exp(l - m) for l in ls]
        den = functools.reduce(jnp.add, es)
        num = functools.reduce(jnp.add, [e * o_scr[p, rows, :] for p, e in enumerate(es)])
        o_ref[0, rows, :] = (num / den).astype(o_ref.dtype)
        return carry

    lax.fori_loop(0, seq // BLOCK, mix, 0)


def _dilated_attention(z3, slopes, *, n_heads, patterns=DIL_PATTERNS):
    B, S, _ = z3.shape
    for _, d in patterns:
        assert S % (d * BLOCK) == 0
    blk = (1, S, HEAD_DIM)
    est = 2 * 4 * S * HEAD_DIM * 4 + 2 * len(patterns) * S * HEAD_DIM * 4
    return pl.pallas_call(
        functools.partial(_dilated_body, seq=S, patterns=patterns, scale=HEAD_DIM ** -0.5),
        grid=(B, n_heads),
        in_specs=[
            pl.BlockSpec(memory_space=pltpu.SMEM),
            pl.BlockSpec(blk, lambda b, h: (b, 0, h)),
            pl.BlockSpec(blk, lambda b, h: (b, 0, n_heads + h)),
            pl.BlockSpec(blk, lambda b, h: (b, 0, 2 * n_heads + h)),
        ],
        out_specs=pl.BlockSpec(blk, lambda b, h: (b, 0, h)),
        out_shape=jax.ShapeDtypeStruct((B, S, n_heads * HEAD_DIM), BF16),
        scratch_shapes=[pltpu.VMEM((len(patterns), S, HEAD_DIM), F32),
                        pltpu.VMEM((len(patterns), S, HEAD_DIM), F32)],
        compiler_params=pltpu.CompilerParams(
            dimension_semantics=("arbitrary", "arbitrary"),
            vmem_limit_bytes=_vmem_limit(est)),
        name="dilated_attention",
    )(slopes, z3, z3, z3)


def _rope(x, cos, sin_signed):
    lane = lax.broadcasted_iota(jnp.int32, x.shape, 1)
    half = QK_ROPE // 2
    swapped = jnp.where(lane < half, pltpu.roll(x, LANES - half, 1), pltpu.roll(x, half, 1))
    return x * cos + swapped * sin_signed


def _mla_body(qn_ref, qp_ref, kn_ref, v_ref, kr_ref, cos_ref, sin_ref,
              o_ref, qcat_scr, kcat_scr, *, seq, tq, scale):
    cos, sin = cos_ref[...], sin_ref[...]
    qcat_scr[:, :LANES] = qn_ref[0]
    qcat_scr[:, LANES:] = _rope(qp_ref[0].astype(F32), cos, sin).astype(BF16)
    kcat_scr[:, :LANES] = kn_ref[0]
    kcat_scr[:, LANES:] = _rope(kr_ref[0], cos, sin).astype(BF16)
    nt = (((1,), (1,)), ((), ()))

    for qi in range(seq // tq):
        ext = (qi + 1) * tq
        rows = slice(qi * tq, ext)
        s = lax.dot_general(qcat_scr[rows, :], kcat_scr[:ext, :], nt, preferred_element_type=F32) * scale
        qpos = qi * tq + lax.broadcasted_iota(jnp.int32, (tq, ext), 0)
        kpos = lax.broadcasted_iota(jnp.int32, (tq, ext), 1)
        s = jnp.where(kpos <= qpos, s, NEG)
        e = jnp.exp(s - jnp.max(s, axis=-1, keepdims=True))
        l = jnp.sum(e, axis=-1, keepdims=True)
        o = jnp.dot(e.astype(BF16), v_ref[0, :ext, :], preferred_element_type=F32) / l
        o_ref[0, rows, :] = o.astype(o_ref.dtype)


def _mla_attention(q3, kv3, z3, kr_col, cos, sin_signed, *, n_heads, tq=512):
    B, S, _ = q3.shape
    tq = min(tq, S)
    assert S % tq == 0 and q3.dtype == BF16 and kv3.dtype == BF16
    blk = (1, S, LANES)
    est = 2 * (5 * 2 + 3 * 4) * S * LANES + 4 * S * LANES * 2 + 3 * tq * S * 4
    return pl.pallas_call(
        functools.partial(_mla_body, seq=S, tq=tq, scale=(QK_NOPE + QK_ROPE) ** -0.5),
        grid=(B, n_heads),
        in_specs=[
            pl.BlockSpec(blk, lambda b, h: (b, 0, h)),
            pl.BlockSpec(blk, lambda b, h: (b, 0, n_heads + h)),
            pl.BlockSpec(blk, lambda b, h: (b, 0, h)),
            pl.BlockSpec(blk, lambda b, h: (b, 0, n_heads + h)),
            pl.BlockSpec(blk, lambda b, h: (b, 0, kr_col)),
            pl.BlockSpec((S, LANES), lambda b, h: (0, 0)),
            pl.BlockSpec((S, LANES), lambda b, h: (0, 0)),
        ],
        out_specs=pl.BlockSpec(blk, lambda b, h: (b, 0, h)),
        out_shape=jax.ShapeDtypeStruct((B, S, n_heads * HEAD_DIM), BF16),
        scratch_shapes=[pltpu.VMEM((S, 2 * LANES), BF16), pltpu.VMEM((S, 2 * LANES), BF16)],
        compiler_params=pltpu.CompilerParams(
            dimension_semantics=("arbitrary", "arbitrary"),
            vmem_limit_bytes=_vmem_limit(est)),
        name="mla_attention",
    )(q3, q3, kv3, kv3, z3, cos, sin_signed)


def _outproj_body(oa_ref, ob_ref, ga_ref, gb_ref, w_ref, x_ref, o_ref, xn_ref, *, da):
    j = pl.program_id(1)

    @pl.when(j == 0)
    def _():
        xn_ref[:, :da] = _rms(oa_ref[...].astype(F32), ga_ref[...]).astype(BF16)
        xn_ref[:, da:] = _rms(ob_ref[...].astype(F32), gb_ref[...]).astype(BF16)

    o_ref[...] = x_ref[...] + jnp.dot(xn_ref[...], w_ref[...], preferred_element_type=F32)


def _outproj(oa, ob, ga, gb, w, x, *, tm=512, tn=512):
    T, da = oa.shape
    db = ob.shape[1]
    N = w.shape[1]
    tm, tn = min(tm, T), min(tn, N)
    assert T % tm == 0 and N % tn == 0 and w.shape[0] == da + db
    est = 2 * (tm * (da + db) * 4 + (da + db) * tn * 2 + 2 * tm * tn * 4) + tm * (da + db) * 2
    return pl.pallas_call(
        functools.partial(_outproj_body, da=da),
        grid=(T // tm, N // tn),
        in_specs=[
            pl.BlockSpec((tm, da), lambda i, j: (i, 0)),
            pl.BlockSpec((tm, db), lambda i, j: (i, 0)),
            pl.BlockSpec((1, da), lambda i, j: (0, 0)),
            pl.BlockSpec((1, db), lambda i, j: (0, 0)),
            pl.BlockSpec((da + db, tn), lambda i, j: (0, j)),
            pl.BlockSpec((tm, tn), lambda i, j: (i, j)),
        ],
        out_specs=pl.BlockSpec((tm, tn), lambda i, j: (i, j)),
        out_shape=jax.ShapeDtypeStruct((T, N), F32),
        scratch_shapes=[pltpu.VMEM((tm, da + db), BF16)],
        compiler_params=pltpu.CompilerParams(
            dimension_semantics=("arbitrary", "arbitrary"),
            vmem_limit_bytes=_vmem_limit(est)),
        name="out_projection",
    )(oa, ob, ga.reshape(1, da), gb.reshape(1, db), w, x)


TOPK_HEADS = 4


def _candidate_blocks(topk):
    blocks = []
    a = 0
    while a < topk:
        nb = topk // (a + 1)
        if nb >= SUBLANES:
            blocks += [(a, 1, b0, min(SUBLANES, nb - b0)) for b0 in range(0, nb, SUBLANES)]
            a += 1
        elif nb > 1:
            blocks.append((a, 1, 0, nb))
            a += 1
        else:
            na = min(SUBLANES, topk - a)
            blocks.append((a, na, 0, 1))
            a += na
    return blocks


def _topk_body(q_ref, keys_ref, idx_ref, gate_ref, s_ref, sv_ref, si_ref, cs_ref, ci_ref, bs_ref, be_ref,
               *, tt, nkeys, topk, blocks):
    lowest = float(jnp.finfo(jnp.float32).min)
    nchain = 2 * TOPK_HEADS
    iota_n = lax.broadcasted_iota(jnp.int32, (nkeys, tt), 0)
    sub = lax.broadcasted_iota(jnp.int32, (SUBLANES, tt), 0)
    nt = (((1,), (1,)), ((), ()))

    for ch in range(nchain):
        s_ref[ch] = lax.dot_general(keys_ref[ch], q_ref[ch], nt, preferred_element_type=F32)

    def pick(k, carry):
        for ch in range(nchain):
            s = s_ref[ch]
            m = jnp.max(s, axis=0, keepdims=True)
            ix = jnp.min(jnp.where(s == m, iota_n, nkeys), axis=0, keepdims=True)
            sv_ref[ch, pl.ds(k, 1), :] = m
            si_ref[ch, pl.ds(k, 1), :] = ix
            s_ref[ch] = jnp.where(iota_n == ix, lowest, s)
        return carry

    lax.fori_loop(0, topk, pick, 0)

    pos_blocks = []
    for r, (a0, na, b0, nb) in enumerate(blocks):
        rows = slice(r * SUBLANES, (r + 1) * SUBLANES)
        if na == 1:
            live = sub < nb
            pos_blocks.append(a0 * topk + b0 + sub)
        else:
            live = sub < na
            pos_blocks.append((a0 + sub) * topk)
        for hd in range(TOPK_HEADS):
            if na == 1:
                cs = sv_ref[2 * hd, a0:a0 + 1, :] + sv_ref[2 * hd + 1, b0:b0 + SUBLANES, :]
                ci = si_ref[2 * hd, a0:a0 + 1, :] * nkeys + si_ref[2 * hd + 1, b0:b0 + SUBLANES, :]
            else:
                cs = sv_ref[2 * hd, a0:a0 + SUBLANES, :] + sv_ref[2 * hd + 1, 0:1, :]
                ci = si_ref[2 * hd, a0:a0 + SUBLANES, :] * nkeys + si_ref[2 * hd + 1, 0:1, :]
            cs_ref[hd, rows, :] = jnp.where(live, cs, lowest)
            ci_ref[hd, rows, :] = ci
    pos = jnp.concatenate(pos_blocks, axis=0)
    big = topk * topk

    def pick2(k, carry):
        for hd in range(TOPK_HEADS):
            c = cs_ref[hd]
            m = jnp.max(c, axis=0, keepdims=True)
            first = jnp.min(jnp.where(c == m, pos, big), axis=0, keepdims=True)
            hit = pos == first
            bs_ref[hd, pl.ds(k, 1), :] = m
            be_ref[hd, pl.ds(k, 1), :] = jnp.max(jnp.where(hit, ci_ref[hd], -1), axis=0, keepdims=True)
            cs_ref[hd] = jnp.where(hit, lowest, c)
        return carry

    lax.fori_loop(0, topk, pick2, 0)

    for hd in range(TOPK_HEADS):
        b = bs_ref[hd]
        e = jnp.exp(b - jnp.max(b, axis=0, keepdims=True))
        gate_ref[hd * topk:(hd + 1) * topk, :] = e / jnp.sum(e, axis=0, keepdims=True)
        idx_ref[hd * topk:(hd + 1) * topk, :] = be_ref[hd]


def _peer_topk(q_slabs, keys, *, tt=128):
    hp, T, c = q_slabs.shape
    heads = hp // 2
    nkeys = keys.shape[1]
    tt = min(tt, T)
    topk = PEER_TOPK
    assert T % tt == 0 and heads % TOPK_HEADS == 0 and topk % SUBLANES == 0
    blocks = _candidate_blocks(topk)
    ncand = len(blocks) * SUBLANES
    nchain = 2 * TOPK_HEADS
    rows = TOPK_HEADS * topk
    return pl.pallas_call(
        functools.partial(_topk_body, tt=tt, nkeys=nkeys, topk=topk, blocks=blocks),
        grid=(T // tt, heads // TOPK_HEADS),
        in_specs=[
            pl.BlockSpec((nchain, tt, c), lambda i, h: (h, i, 0)),
            pl.BlockSpec((nchain, nkeys, c), lambda i, h: (h, 0, 0)),
        ],
        out_specs=[pl.BlockSpec((rows, tt), lambda i, h: (h, i)),
                   pl.BlockSpec((rows, tt), lambda i, h: (h, i))],
        out_shape=[jax.ShapeDtypeStruct((heads * topk, T), jnp.int32),
                   jax.ShapeDtypeStruct((heads * topk, T), F32)],
        scratch_shapes=[pltpu.VMEM((nchain, nkeys, tt), F32),
                        pltpu.VMEM((nchain, topk, tt), F32), pltpu.VMEM((nchain, topk, tt), jnp.int32),
                        pltpu.VMEM((TOPK_HEADS, ncand, tt), F32), pltpu.VMEM((TOPK_HEADS, ncand, tt), jnp.int32),
                        pltpu.VMEM((TOPK_HEADS, topk, tt), F32), pltpu.VMEM((TOPK_HEADS, topk, tt), jnp.int32)],
        compiler_params=pltpu.CompilerParams(dimension_semantics=("arbitrary", "arbitrary")),
        name="peer_topk",
    )(q_slabs, keys)


GSIZE = SUBLANES
NGROUP = PEER_E // GSIZE
DOT_CHAINS = 4
FFN_SLOTS = 4
FFN_PHASES = 8
FFN_AHEAD = 2


def _gelu_exact(x):
    return 0.5 * x * (1.0 + lax.erf(x * (2.0 ** -0.5)))


def _peer_ffn_body(idx_hbm, gate_ref, x_ref, h_ref, gf_ref, tab_ref, o_ref, *scratch,
                   ctok, nrow, ntok):
    bufs = scratch[:FFN_SLOTS]
    idx_a, idx_b, sem, isem = scratch[FFN_SLOTS:]
    i = pl.program_id(0)
    n = pl.num_programs(0)
    sub = lax.broadcasted_iota(jnp.int32, (SUBLANES, LANES), 0)
    lane = lax.broadcasted_iota(jnp.int32, (SUBLANES, LANES), 1)
    step = FFN_PHASES * ctok
    half = step // 2
    tile = 2 * nrow

    def idx_copy(part, base):
        lo, cnt, dst = ((0, half // 2, idx_a), (half // 2, half // 2, idx_a), (half, half, idx_b))[part]
        src = idx_hbm.at[pl.ds(pl.multiple_of(base + lo, SUBLANES), cnt), :]
        return pltpu.make_async_copy(src, dst.at[pl.ds(lo % half, cnt), :], isem.at[part])

    next_base = jnp.minimum((i + 1) * step, ntok - step)

    def idx_of(tok, j):
        return idx_a[tok, j] if tok < half else idx_b[tok - half, j]

    def start_row(tok, dst_slot, c, j):
        pltpu.make_async_copy(tab_ref.at[idx_of(tok, j)], bufs[dst_slot].at[pl.ds((c * PEER_E + j) * tile, tile), :],
                              sem.at[dst_slot]).start(priority=j % 2)

    def wait_slot(s):
        pltpu.make_async_copy(bufs[s], bufs[s], sem.at[s]).wait()

    def expert_tile(slot, c, j, which):
        return bufs[slot][pl.ds((c * PEER_E + j) * tile + which * nrow, nrow), :].astype(F32)

    def token_tile(ref, tok):
        return jnp.concatenate([ref[tok:tok + 1, s * LANES:(s + 1) * LANES] for s in range(nrow)], axis=0)

    @pl.when(i == 0)
    def _():
        first_ids = idx_copy(0, 0)
        first_ids.start()
        first_ids.wait()
        idx_copy(1, 0).start()
        idx_copy(2, 0).start()
        for p in range(FFN_AHEAD):
            def first(c, carry, p=p):
                for j in range(PEER_E):
                    first_row = pl.multiple_of((c * PEER_E + j) * tile, tile)
                    pltpu.make_async_copy(tab_ref.at[idx_a[p * ctok + c, j]],
                                          bufs[p].at[pl.ds(first_row, tile), :], sem.at[p]).start(priority=j % 2)
                return carry
            lax.fori_loop(0, ctok, first, 0)

    idx_copy(0, next_base).start()
    idx_copy(1, 0).wait()
    results = []
    for p in range(FFN_PHASES):
        if p * ctok == half // 2:
            idx_copy(1, next_base).start()
            idx_copy(2, 0).wait()
        if (p + FFN_AHEAD) * ctok == step:
            idx_copy(2, next_base).start()
            idx_copy(0, 0).wait()
        slot = p % FFN_SLOTS
        wait_slot(slot)
        ahead_slot = (p + FFN_AHEAD) % FFN_SLOTS
        ahead_tok0 = ((p + FFN_AHEAD) % FFN_PHASES) * ctok
        for c in range(ctok):
            tok = p * ctok + c
            nxt = lambda j, c=c: start_row(ahead_tok0 + c, ahead_slot, c, j)
            xt = token_tile(x_ref, tok)
            partial = [jnp.zeros((SUBLANES, LANES), F32) for _ in range(DOT_CHAINS)]
            for e in range(PEER_E):
                if e % 2 == 0:
                    nxt(e // 2)
                q = expert_tile(slot, c, e, 0) * xt
                col = jnp.sum(q[:SUBLANES] + q[SUBLANES:], axis=1, keepdims=True)
                partial[e % DOT_CHAINS] = jnp.where(lane == e, col, partial[e % DOT_CHAINS])
            dots = jnp.sum(functools.reduce(jnp.add, partial), axis=0, keepdims=True)
            act = _gelu_exact(dots) * gate_ref[tok:tok + 1, :]
            act_rows = jnp.broadcast_to(act, (SUBLANES, LANES))
            ya = [jnp.zeros((SUBLANES, LANES), F32) for _ in range(4)]
            for g in range(NGROUP):
                onto = jnp.sum(jnp.where(lane == g * GSIZE + sub, act_rows, 0.0), axis=1, keepdims=True)
                a_g = jnp.broadcast_to(onto, (SUBLANES, LANES))
                for k in range(GSIZE):
                    e = g * GSIZE + k
                    if e % 2 == 0:
                        nxt(PEER_E // 2 + e // 2)
                    a = jnp.broadcast_to(a_g[k:k + 1, :], (SUBLANES, LANES))
                    vt = expert_tile(slot, c, e, 1)
                    ya[2 * (k % 2)] = ya[2 * (k % 2)] + a * vt[:SUBLANES]
                    ya[2 * (k % 2) + 1] = ya[2 * (k % 2) + 1] + a * vt[SUBLANES:]
            z = token_tile(h_ref, tok) + jnp.concatenate([ya[0] + ya[2], ya[1] + ya[3]], axis=0)
            ms = jnp.sum(jnp.sum(z * z, axis=1, keepdims=True), axis=0, keepdims=True) * (1.0 / (nrow * LANES))
            results.append((tok, z * lax.rsqrt(ms + EPS) * gf_ref[...]))
        if len(results) == SUBLANES:
            for tok, zn in results:
                for s in range(nrow):
                    o_ref[tok:tok + 1, s * LANES:(s + 1) * LANES] = zn[s:s + 1, :]
            results = []

    @pl.when(i == n - 1)
    def _():
        for p in range(FFN_AHEAD):
            wait_slot(p)
        idx_copy(1, 0).wait()
        idx_copy(2, 0).wait()


def _peer_ffn(idx, gate, xn, h, gf, table, *, ctok=4):
    T, nexp = idx.shape
    D = xn.shape[1]
    nrow = D // LANES
    step = FFN_PHASES * ctok
    half = step // 2
    assert nexp == PEER_E and SUBLANES % ctok == 0 and half % (2 * SUBLANES) == 0 and T % step == 0 and T >= 2 * step
    assert table.shape[1:] == (2 * nrow, LANES) and table.dtype == BF16 and nrow % (2 * SUBLANES) == 0
    assert FFN_PHASES % FFN_SLOTS == 0 and 0 < FFN_AHEAD < FFN_SLOTS and FFN_AHEAD * ctok == half // 2
    n = T // step
    slot_rows = ctok * nexp * 2 * nrow
    est = FFN_SLOTS * slot_rows * LANES * 2 + 6 * step * D * 4
    return pl.pallas_call(
        functools.partial(_peer_ffn_body, ctok=ctok, nrow=nrow, ntok=T),
        grid=(n,),
        in_specs=[
            pl.BlockSpec(memory_space=pl.ANY),
            pl.BlockSpec((step, nexp), lambda i: (i, 0)),
            pl.BlockSpec((step, D), lambda i: (i, 0)),
            pl.BlockSpec((step, D), lambda i: (i, 0)),
            pl.BlockSpec((nrow, LANES), lambda i: (0, 0)),
            pl.BlockSpec(memory_space=pl.ANY),
        ],
        out_specs=pl.BlockSpec((step, D), lambda i: (i, 0)),
        out_shape=jax.ShapeDtypeStruct((T, D), F32),
        scratch_shapes=[pltpu.VMEM((slot_rows, LANES), BF16) for _ in range(FFN_SLOTS)]
                       + [pltpu.SMEM((half, nexp), jnp.int32), pltpu.SMEM((half, nexp), jnp.int32),
                          pltpu.SemaphoreType.DMA((FFN_SLOTS,)), pltpu.SemaphoreType.DMA((3,))],
        compiler_params=pltpu.CompilerParams(
            dimension_semantics=("arbitrary",),
            vmem_limit_bytes=_vmem_limit(est)),
        name="peer_ffn",
    )(idx, gate, xn, h, gf, table)


def _expert_table(u, v):
    n, d = u.shape
    both = jnp.concatenate([u.astype(BF16).reshape(n, d // LANES, LANES),
                            v.astype(BF16).reshape(n, d // LANES, LANES)], axis=1)
    return both


def _rope_tables(seq):
    half = QK_ROPE // 2
    freqs = ROPE_THETA ** (-jnp.arange(half, dtype=F32) / half)
    ang = jnp.arange(seq, dtype=F32)[:, None] * freqs[None, :]
    cos, sin = jnp.cos(ang), jnp.sin(ang)
    zeros = jnp.zeros((seq, LANES - QK_ROPE), F32)
    return (jnp.concatenate([cos, cos, zeros], axis=1),
            jnp.concatenate([-sin, sin, zeros], axis=1))


def _layer(h, ln1_g, w_in, q_a_norm, kv_a_norm, w_uq, w_ukv, out_norm_dil, out_norm_mla, w_o,
           ln2_g, peer_wq, peer_sub_keys, peer_u, peer_v, final_g):
    B, S, D = h.shape
    T = B * S
    H = N_HEADS
    d_dil = H * HEAD_DIM
    q_lora = q_a_norm.shape[0]
    kv_lora = kv_a_norm.shape[0]
    x2 = h.reshape(T, D)

    d_in = w_in.shape[1]
    d_in_pad = -(-d_in // LANES) * LANES
    w_in_p = jnp.pad(w_in, ((0, 0), (0, d_in_pad - d_in))).astype(BF16)
    z = _norm_matmul(x2, ln1_g, w_in_p, tm=ROW_TILE,
                     tn=d_in_pad // 3 if d_in_pad % (3 * LANES) == 0 else LANES)
    z3 = z.reshape(B, S, d_in_pad)

    slopes = 2.0 ** (-8.0 * jnp.arange(1, H + 1, dtype=F32) / H)
    o_a = _dilated_attention(z3, slopes, n_heads=H)

    wq = w_uq.reshape(q_lora, H, QK_NOPE + QK_ROPE)
    wq_pe = jnp.pad(wq[:, :, QK_NOPE:], ((0, 0), (0, 0), (0, LANES - QK_ROPE)))
    wq_p = jnp.concatenate([wq[:, :, :QK_NOPE].reshape(q_lora, H * QK_NOPE),
                            wq_pe.reshape(q_lora, H * LANES)], axis=1).astype(BF16)
    wkv = w_ukv.reshape(kv_lora, H, QK_NOPE + HEAD_DIM)
    wkv_p = jnp.concatenate([wkv[:, :, :QK_NOPE].reshape(kv_lora, H * QK_NOPE),
                             wkv[:, :, QK_NOPE:].reshape(kv_lora, H * HEAD_DIM)], axis=1).astype(BF16)
    assert q_lora == kv_lora and (3 * d_dil) % q_lora == 0
    q_mla = _norm_matmul(z, q_a_norm, wq_p, xcol=3 * d_dil // q_lora, kdim=q_lora, out_dtype=BF16,
                         tm=ROW_TILE, tn=wq_p.shape[1])
    kv_mla = _norm_matmul(z, kv_a_norm, wkv_p, xcol=3 * d_dil // q_lora + 1, kdim=kv_lora, out_dtype=BF16,
                          tm=ROW_TILE, tn=wkv_p.shape[1])
    cos, sin_signed = _rope_tables(S)
    o_b = _mla_attention(q_mla.reshape(B, S, -1), kv_mla.reshape(B, S, -1), z3,
                         (3 * d_dil + q_lora + kv_lora) // LANES, cos, sin_signed, n_heads=H)

    h1 = _outproj(o_a.reshape(T, d_dil), o_b.reshape(T, -1), out_norm_dil, out_norm_mla,
                  w_o.astype(BF16), x2, tm=ROW_TILE)

    q_slabs, xn2 = _norm_matmul(h1, ln2_g, peer_wq.astype(BF16), tm=ROW_TILE, emit_xn=True,
                                slab_out=True, out_dtype=BF16)
    keys = peer_sub_keys.reshape(PEER_HEADS * 2, PEER_NKEYS, -1).astype(BF16)
    idx_t, gate_t = _peer_topk(q_slabs, keys)
    table = _expert_table(peer_u, peer_v)
    out = _peer_ffn(idx_t.T, gate_t.T, xn2, h1, final_g.reshape(D // LANES, LANES), table)
    return out.reshape(B, S, D)


def kernel(x, ln1_g, w_in, q_a_norm, kv_a_norm, w_uq, w_ukv, out_norm_dil, out_norm_mla, w_o,
           ln2_g, peer_wq, peer_sub_keys, peer_u, peer_v, lnf_g):
    assert ln1_g.shape[0] == 1, "single-layer trunk"
    return _layer(x, ln1_g[0], w_in[0], q_a_norm[0], kv_a_norm[0], w_uq[0], w_ukv[0],
                  out_norm_dil[0], out_norm_mla[0], w_o[0], ln2_g[0], peer_wq[0],
                  peer_sub_keys[0], peer_u[0], peer_v[0], lnf_g)
```

```python
import functools

import jax
import jax.numpy as jnp
from jax import lax
from jax.experimental import pallas as pl
from jax.experimental.pallas import tpu as pltpu

F32 = jnp.float32
BF16 = jnp.bfloat16

EPS = 1e-6
NEG = -1e30
HEAD_DIM = 128
BLOCK = 128
DIL_PATTERNS = ((128, 1), (512, 4), (2048, 16))
DIL_UNROLL = 16
N_HEADS = 8
QK_NOPE = 128
QK_ROPE = 64
ROPE_THETA = 10000.0
PEER_HEADS = 8
PEER_NKEYS = 128
PEER_TOPK = 16
PEER_E = PEER_HEADS * PEER_TOPK

LANES = 128
SUBLANES = 8
VMEM_CAP = 60000 * 1024
ROW_TILE = 1024


def _vmem_limit(nbytes):
    return int(min(VMEM_CAP, max(16 * 1024 * 1024, nbytes * 3 // 2)))


def _rms(x, g):
    return x * lax.rsqrt(jnp.mean(x * x, axis=-1, keepdims=True) + EPS) * g


def _norm_matmul_body(x_ref, g_ref, w_ref, *rest, emit_xn, slabs):
    if emit_xn:
        o_ref, xn_out_ref, xn_ref = rest
    else:
        o_ref, xn_ref = rest
    j = pl.program_id(1)

    @pl.when(j == 0)
    def _():
        xn = _rms(x_ref[...].astype(F32), g_ref[...])
        xn_ref[...] = xn.astype(BF16)
        if emit_xn:
            xn_out_ref[...] = xn

    res = jnp.dot(xn_ref[...], w_ref[...], preferred_element_type=F32).astype(o_ref.dtype)
    if slabs:
        for s in range(slabs):
            o_ref[s] = res[:, s * LANES:(s + 1) * LANES]
    else:
        o_ref[...] = res


def _norm_matmul(x, g, w, *, xcol=0, kdim=None, tm=512, tn=512, emit_xn=False, slab_out=False,
                 out_dtype=F32):
    T = x.shape[0]
    kdim = kdim or x.shape[1]
    N = w.shape[1]
    tm = min(tm, T)
    tn = min(tn, N)
    assert T % tm == 0 and N % tn == 0 and w.shape[0] == kdim
    slabs = tn // LANES if slab_out else 0
    if slab_out:
        out_shape = [jax.ShapeDtypeStruct((N // LANES, T, LANES), out_dtype)]
        out_specs = [pl.BlockSpec((slabs, tm, LANES), lambda i, j: (j, i, 0))]
    else:
        out_shape = [jax.ShapeDtypeStruct((T, N), out_dtype)]
        out_specs = [pl.BlockSpec((tm, tn), lambda i, j: (i, j))]
    if emit_xn:
        out_shape.append(jax.ShapeDtypeStruct((T, kdim), F32))
        out_specs.append(pl.BlockSpec((tm, kdim), lambda i, j: (i, 0)))
    est = 2 * (tm * kdim * 4 + kdim * tn * 2 + tm * tn * 4) + tm * kdim * 2
    if emit_xn:
        est += 2 * tm * kdim * 4
    outs = pl.pallas_call(
        functools.partial(_norm_matmul_body, emit_xn=emit_xn, slabs=slabs),
        grid=(T // tm, N // tn),
        in_specs=[
            pl.BlockSpec((tm, kdim), lambda i, j: (i, xcol)),
            pl.BlockSpec((1, kdim), lambda i, j: (0, 0)),
            pl.BlockSpec((kdim, tn), lambda i, j: (0, j)),
        ],
        out_specs=out_specs,
        out_shape=out_shape,
        scratch_shapes=[pltpu.VMEM((tm, kdim), BF16)],
        compiler_params=pltpu.CompilerParams(
            dimension_semantics=("arbitrary", "arbitrary"),
            vmem_limit_bytes=_vmem_limit(est)),
        name="norm_matmul",
    )(x, g.reshape(1, kdim).astype(F32), w)
    return outs if emit_xn else outs[0]


def _dilated_body(slopes_ref, q_ref, k_ref, v_ref, o_ref, o_scr, l_scr, *, seq, patterns, scale):
    h = pl.program_id(1)
    slope = slopes_ref[h]
    qi = lax.broadcasted_iota(jnp.int32, (BLOCK, 2 * BLOCK), 0)
    kj = lax.broadcasted_iota(jnp.int32, (BLOCK, 2 * BLOCK), 1)
    delta = qi + BLOCK - kj
    nt = (((1,), (1,)), ((), ()))

    for p, (window, d) in enumerate(patterns):
        steps = window // d
        nb = seq // d // BLOCK
        in_window = (delta >= 0) & (delta <= steps)
        bias = -slope * (delta * d).astype(F32)

        def block(t, carry, d=d, nb=nb, in_window=in_window, bias=bias, p=p):
            r = t // nb
            n = t % nb
            start = n * (BLOCK * d) + r
            pstart = jnp.maximum(n - 1, 0) * (BLOCK * d) + r
            rows = pl.ds(start, BLOCK, stride=d) if d > 1 else pl.ds(start, BLOCK)
            prow = pl.ds(pstart, BLOCK, stride=d) if d > 1 else pl.ds(pstart, BLOCK)
            q = q_ref[0, rows, :].astype(BF16)
            kk = jnp.concatenate([k_ref[0, prow, :], k_ref[0, rows, :]], axis=0).astype(BF16)
            vv = jnp.concatenate([v_ref[0, prow, :], v_ref[0, rows, :]], axis=0).astype(BF16)
            s = lax.dot_general(q, kk, nt, preferred_element_type=F32) * scale
            valid = in_window & ((kj >= BLOCK) | (n > 0))
            s = jnp.where(valid, s + bias, NEG)
            m = jnp.max(s, axis=-1, keepdims=True)
            e = jnp.exp(s - m)
            l = jnp.sum(e, axis=-1, keepdims=True)
            o = jnp.dot((e / l).astype(BF16), vv, preferred_element_type=F32)
            o_scr[p, rows, :] = o
            l_scr[p, rows, :] = jnp.broadcast_to(m + jnp.log(l), (BLOCK, HEAD_DIM))
            return carry

        lax.fori_loop(0, d * nb, block, 0, unroll=DIL_UNROLL)

    def mix(c, carry):
        rows = pl.ds(pl.multiple_of(c * BLOCK, BLOCK), BLOCK)
        ls = [l_scr[p, rows, :] for p in range(len(patterns))]
        m = functools.reduce(jnp.maximum, ls)
        es = [jnp.exp(l - m) for l in ls]
        den = functools.reduce(jnp.add, es)
        num = functools.reduce(jnp.add, [e * o_scr[p, rows, :] for p, e in enumerate(es)])
        o_ref[0, rows, :] = (num / den).astype(o_ref.dtype)
        return carry

    lax.fori_loop(0, seq // BLOCK, mix, 0)


def _dilated_attention(z3, slopes, *, n_heads, patterns=DIL_PATTERNS):
    B, S, _ = z3.shape
    for _, d in patterns:
        assert S % (d * BLOCK) == 0
    blk = (1, S, HEAD_DIM)
    est = 2 * 4 * S * HEAD_DIM * 4 + 2 * len(patterns) * S * HEAD_DIM * 4
    return pl.pallas_call(
        functools.partial(_dilated_body, seq=S, patterns=patterns, scale=HEAD_DIM ** -0.5),
        grid=(B, n_heads),
        in_specs=[
            pl.BlockSpec(memory_space=pltpu.SMEM),
            pl.BlockSpec(blk, lambda b, h: (b, 0, h)),
            pl.BlockSpec(blk, lambda b, h: (b, 0, n_heads + h)),
            pl.BlockSpec(blk, lambda b, h: (b, 0, 2 * n_heads + h)),
        ],
        out_specs=pl.BlockSpec(blk, lambda b, h: (b, 0, h)),
        out_shape=jax.ShapeDtypeStruct((B, S, n_heads * HEAD_DIM), BF16),
        scratch_shapes=[pltpu.VMEM((len(patterns), S, HEAD_DIM), F32),
                        pltpu.VMEM((len(patterns), S, HEAD_DIM), F32)],
        compiler_params=pltpu.CompilerParams(
            dimension_semantics=("arbitrary", "arbitrary"),
            vmem_limit_bytes=_vmem_limit(est)),
        name="dilated_attention",
    )(slopes, z3, z3, z3)


def _rope(x, cos, sin_signed):
    lane = lax.broadcasted_iota(jnp.int32, x.shape, 1)
    half = QK_ROPE // 2
    swapped = jnp.where(lane < half, pltpu.roll(x, LANES - half, 1), pltpu.roll(x, half, 1))
    return x * cos + swapped * sin_signed


def _mla_body(qn_ref, qp_ref, kn_ref, v_ref, kr_ref, cos_ref, sin_ref,
              o_ref, qcat_scr, kcat_scr, *, seq, tq, scale):
    cos, sin = cos_ref[...], sin_ref[...]
    qcat_scr[:, :LANES] = qn_ref[0]
    qcat_scr[:, LANES:] = _rope(qp_ref[0].astype(F32), cos, sin).astype(BF16)
    kcat_scr[:, :LANES] = kn_ref[0]
    kcat_scr[:, LANES:] = _rope(kr_ref[0], cos, sin).astype(BF16)
    nt = (((1,), (1,)), ((), ()))

    for qi in range(seq // tq):
        ext = (qi + 1) * tq
        rows = slice(qi * tq, ext)
        s = lax.dot_general(qcat_scr[rows, :], kcat_scr[:ext, :], nt, preferred_element_type=F32) * scale
        qpos = qi * tq + lax.broadcasted_iota(jnp.int32, (tq, ext), 0)
        kpos = lax.broadcasted_iota(jnp.int32, (tq, ext), 1)
        s = jnp.where(kpos <= qpos, s, NEG)
        e = jnp.exp(s - jnp.max(s, axis=-1, keepdims=True))
        l = jnp.sum(e, axis=-1, keepdims=True)
        o = jnp.dot(e.astype(BF16), v_ref[0, :ext, :], preferred_element_type=F32) / l
        o_ref[0, rows, :] = o.astype(o_ref.dtype)


def _mla_attention(q3, kv3, z3, kr_col, cos, sin_signed, *, n_heads, tq=512):
    B, S, _ = q3.shape
    tq = min(tq, S)
    assert S % tq == 0 and q3.dtype == BF16 and kv3.dtype == BF16
    blk = (1, S, LANES)
    est = 2 * (5 * 2 + 3 * 4) * S * LANES + 4 * S * LANES * 2 + 3 * tq * S * 4
    return pl.pallas_call(
        functools.partial(_mla_body, seq=S, tq=tq, scale=(QK_NOPE + QK_ROPE) ** -0.5),
        grid=(B, n_heads),
        in_specs=[
            pl.BlockSpec(blk, lambda b, h: (b, 0, h)),
            pl.BlockSpec(blk, lambda b, h: (b, 0, n_heads + h)),
            pl.BlockSpec(blk, lambda b, h: (b, 0, h)),
            pl.BlockSpec(blk, lambda b, h: (b, 0, n_heads + h)),
            pl.BlockSpec(blk, lambda b, h: (b, 0, kr_col)),
            pl.BlockSpec((S, LANES), lambda b, h: (0, 0)),
            pl.BlockSpec((S, LANES), lambda b, h: (0, 0)),
        ],
        out_specs=pl.BlockSpec(blk, lambda b, h: (b, 0, h)),
        out_shape=jax.ShapeDtypeStruct((B, S, n_heads * HEAD_DIM), BF16),
        scratch_shapes=[pltpu.VMEM((S, 2 * LANES), BF16), pltpu.VMEM((S, 2 * LANES), BF16)],
        compiler_params=pltpu.CompilerParams(
            dimension_semantics=("arbitrary", "arbitrary"),
            vmem_limit_bytes=_vmem_limit(est)),
        name="mla_attention",
    )(q3, q3, kv3, kv3, z3, cos, sin_signed)


def _outproj_body(oa_ref, ob_ref, ga_ref, gb_ref, w_ref, x_ref, o_ref, xn_ref, *, da):
    j = pl.program_id(1)

    @pl.when(j == 0)
    def _():
        xn_ref[:, :da] = _rms(oa_ref[...].astype(F32), ga_ref[...]).astype(BF16)
        xn_ref[:, da:] = _rms(ob_ref[...].astype(F32), gb_ref[...]).astype(BF16)

    o_ref[...] = x_ref[...] + jnp.dot(xn_ref[...], w_ref[...], preferred_element_type=F32)


def _outproj(oa, ob, ga, gb, w, x, *, tm=512, tn=512):
    T, da = oa.shape
    db = ob.shape[1]
    N = w.shape[1]
    tm, tn = min(tm, T), min(tn, N)
    assert T % tm == 0 and N % tn == 0 and w.shape[0] == da + db
    est = 2 * (tm * (da + db) * 4 + (da + db) * tn * 2 + 2 * tm * tn * 4) + tm * (da + db) * 2
    return pl.pallas_call(
        functools.partial(_outproj_body, da=da),
        grid=(T // tm, N // tn),
        in_specs=[
            pl.BlockSpec((tm, da), lambda i, j: (i, 0)),
            pl.BlockSpec((tm, db), lambda i, j: (i, 0)),
            pl.BlockSpec((1, da), lambda i, j: (0, 0)),
            pl.BlockSpec((1, db), lambda i, j: (0, 0)),
            pl.BlockSpec((da + db, tn), lambda i, j: (0, j)),
            pl.BlockSpec((tm, tn), lambda i, j: (i, j)),
        ],
        out_specs=pl.BlockSpec((tm, tn), lambda i, j: (i, j)),
        out_shape=jax.ShapeDtypeStruct((T, N), F32),
        scratch_shapes=[pltpu.VMEM((tm, da + db), BF16)],
        compiler_params=pltpu.CompilerParams(
            dimension_semantics=("arbitrary", "arbitrary"),
            vmem_limit_bytes=_vmem_limit(est)),
        name="out_projection",
    )(oa, ob, ga.reshape(1, da), gb.reshape(1, db), w, x)


TOPK_HEADS = 4


def _candidate_blocks(topk):
    blocks = []
    a = 0
    while a < topk:
        nb = topk // (a + 1)
        if nb >= SUBLANES:
            blocks += [(a, 1, b0, min(SUBLANES, nb - b0)) for b0 in range(0, nb, SUBLANES)]
            a += 1
        elif nb > 1:
            blocks.append((a, 1, 0, nb))
            a += 1
        else:
            na = min(SUBLANES, topk - a)
            blocks.append((a, na, 0, 1))
            a += na
    return blocks


def _topk_body(q_ref, keys_ref, idx_ref, gate_ref, s_ref, sv_ref, si_ref, cs_ref, ci_ref, bs_ref, be_ref,
               *, tt, nkeys, topk, blocks):
    lowest = float(jnp.finfo(jnp.float32).min)
    nchain = 2 * TOPK_HEADS
    iota_n = lax.broadcasted_iota(jnp.int32, (nkeys, tt), 0)
    sub = lax.broadcasted_iota(jnp.int32, (SUBLANES, tt), 0)
    nt = (((1,), (1,)), ((), ()))

    for ch in range(nchain):
        s_ref[ch] = lax.dot_general(keys_ref[ch], q_ref[ch], nt, preferred_element_type=F32)

    def pick(k, carry):
        for ch in range(nchain):
            s = s_ref[ch]
            m = jnp.max(s, axis=0, keepdims=True)
            ix = jnp.min(jnp.where(s == m, iota_n, nkeys), axis=0, keepdims=True)
            sv_ref[ch, pl.ds(k, 1), :] = m
            si_ref[ch, pl.ds(k, 1), :] = ix
            s_ref[ch] = jnp.where(iota_n == ix, lowest, s)
        return carry

    lax.fori_loop(0, topk, pick, 0)

    pos_blocks = []
    for r, (a0, na, b0, nb) in enumerate(blocks):
        rows = slice(r * SUBLANES, (r + 1) * SUBLANES)
        if na == 1:
            live = sub < nb
            pos_blocks.append(a0 * topk + b0 + sub)
        else:
            live = sub < na
            pos_blocks.append((a0 + sub) * topk)
        for hd in range(TOPK_HEADS):
            if na == 1:
                cs = sv_ref[2 * hd, a0:a0 + 1, :] + sv_ref[2 * hd + 1, b0:b0 + SUBLANES, :]
                ci = si_ref[2 * hd, a0:a0 + 1, :] * nkeys + si_ref[2 * hd + 1, b0:b0 + SUBLANES, :]
            else:
                cs = sv_ref[2 * hd, a0:a0 + SUBLANES, :] + sv_ref[2 * hd + 1, 0:1, :]
                ci = si_ref[2 * hd, a0:a0 + SUBLANES, :] * nkeys + si_ref[2 * hd + 1, 0:1, :]
            cs_ref[hd, rows, :] = jnp.where(live, cs, lowest)
            ci_ref[hd, rows, :] = ci
    pos = jnp.concatenate(pos_blocks, axis=0)
    big = topk * topk

    def pick2(k, carry):
        for hd in range(TOPK_HEADS):
            c = cs_ref[hd]
            m = jnp.max(c, axis=0, keepdims=True)
            first = jnp.min(jnp.where(c == m, pos, big), axis=0, keepdims=True)
            hit = pos == first
            bs_ref[hd, pl.ds(k, 1), :] = m
            be_ref[hd, pl.ds(k, 1), :] = jnp.max(jnp.where(hit, ci_ref[hd], -1), axis=0, keepdims=True)
            cs_ref[hd] = jnp.where(hit, lowest, c)
        return carry

    lax.fori_loop(0, topk, pick2, 0)

    for hd in range(TOPK_HEADS):
        b = bs_ref[hd]
        e = jnp.exp(b - jnp.max(b, axis=0, keepdims=True))
        gate_ref[hd * topk:(hd + 1) * topk, :] = e / jnp.sum(e, axis=0, keepdims=True)
        idx_ref[hd * topk:(hd + 1) * topk, :] = be_ref[hd]


def _peer_topk(q_slabs, keys, *, tt=128):
    hp, T, c = q_slabs.shape
    heads = hp // 2
    nkeys = keys.shape[1]
    tt = min(tt, T)
    topk = PEER_TOPK
    assert T % tt == 0 and heads % TOPK_HEADS == 0 and topk % SUBLANES == 0
    blocks = _candidate_blocks(topk)
    ncand = len(blocks) * SUBLANES
    nchain = 2 * TOPK_HEADS
    rows = TOPK_HEADS * topk
    return pl.pallas_call(
        functools.partial(_topk_body, tt=tt, nkeys=nkeys, topk=topk, blocks=blocks),
        grid=(T // tt, heads // TOPK_HEADS),
        in_specs=[
            pl.BlockSpec((nchain, tt, c), lambda i, h: (h, i, 0)),
            pl.BlockSpec((nchain, nkeys, c), lambda i, h: (h, 0, 0)),
        ],
        out_specs=[pl.BlockSpec((rows, tt), lambda i, h: (h, i)),
                   pl.BlockSpec((rows, tt), lambda i, h: (h, i))],
        out_shape=[jax.ShapeDtypeStruct((heads * topk, T), jnp.int32),
                   jax.ShapeDtypeStruct((heads * topk, T), F32)],
        scratch_shapes=[pltpu.VMEM((nchain, nkeys, tt), F32),
                        pltpu.VMEM((nchain, topk, tt), F32), pltpu.VMEM((nchain, topk, tt), jnp.int32),
                        pltpu.VMEM((TOPK_HEADS, ncand, tt), F32), pltpu.VMEM((TOPK_HEADS, ncand, tt), jnp.int32),
                        pltpu.VMEM((TOPK_HEADS, topk, tt), F32), pltpu.VMEM((TOPK_HEADS, topk, tt), jnp.int32)],
        compiler_params=pltpu.CompilerParams(dimension_semantics=("arbitrary", "arbitrary")),
        name="peer_topk",
    )(q_slabs, keys)


GSIZE = SUBLANES
NGROUP = PEER_E // GSIZE
DOT_CHAINS = 4
FFN_SLOTS = 4
FFN_PHASES = 8
FFN_AHEAD = 2


def _gelu_exact(x):
    return 0.5 * x * (1.0 + lax.erf(x * (2.0 ** -0.5)))


def _peer_ffn_body(idx_hbm, gate_ref, x_ref, h_ref, gf_ref, tab_ref, o_ref, *scratch,
                   ctok, nrow, ntok):
    bufs = scratch[:FFN_SLOTS]
    idx_a, idx_b, sem, isem = scratch[FFN_SLOTS:]
    i = pl.program_id(0)
    n = pl.num_programs(0)
    sub = lax.broadcasted_iota(jnp.int32, (SUBLANES, LANES), 0)
    lane = lax.broadcasted_iota(jnp.int32, (SUBLANES, LANES), 1)
    step = FFN_PHASES * ctok
    half = step // 2
    tile = 2 * nrow

    def idx_copy(part, base):
        lo, cnt, dst = ((0, half // 2, idx_a), (half // 2, half // 2, idx_a), (half, half, idx_b))[part]
        src = idx_hbm.at[pl.ds(pl.multiple_of(base + lo, SUBLANES), cnt), :]
        return pltpu.make_async_copy(src, dst.at[pl.ds(lo % half, cnt), :], isem.at[part])

    next_base = jnp.minimum((i + 1) * step, ntok - step)

    def idx_of(tok, j):
        return idx_a[tok, j] if tok < half else idx_b[tok - half, j]

    def start_row(tok, dst_slot, c, j):
        pltpu.make_async_copy(tab_ref.at[idx_of(tok, j)], bufs[dst_slot].at[pl.ds((c * PEER_E + j) * tile, tile), :],
                              sem.at[dst_slot]).start(priority=j % 2)

    def wait_slot(s):
        pltpu.make_async_copy(bufs[s], bufs[s], sem.at[s]).wait()

    def expert_tile(slot, c, j, which):
        return bufs[slot][pl.ds((c * PEER_E + j) * tile + which * nrow, nrow), :].astype(F32)

    def token_tile(ref, tok):
        return jnp.concatenate([ref[tok:tok + 1, s * LANES:(s + 1) * LANES] for s in range(nrow)], axis=0)

    @pl.when(i == 0)
    def _():
        first_ids = idx_copy(0, 0)
        first_ids.start()
        first_ids.wait()
        idx_copy(1, 0).start()
        idx_copy(2, 0).start()
        for p in range(FFN_AHEAD):
            def first(c, carry, p=p):
                for j in range(PEER_E):
                    first_row = pl.multiple_of((c * PEER_E + j) * tile, tile)
                    pltpu.make_async_copy(tab_ref.at[idx_a[p * ctok + c, j]],
                                          bufs[p].at[pl.ds(first_row, tile), :], sem.at[p]).start(priority=j % 2)
                return carry
            lax.fori_loop(0, ctok, first, 0)

    idx_copy(0, next_base).start()
    idx_copy(1, 0).wait()
    results = []
    for p in range(FFN_PHASES):
        if p * ctok == half // 2:
            idx_copy(1, next_base).start()
            idx_copy(2, 0).wait()
        if (p + FFN_AHEAD) * ctok == step:
            idx_copy(2, next_base).start()
            idx_copy(0, 0).wait()
        slot = p % FFN_SLOTS
        wait_slot(slot)
        ahead_slot = (p + FFN_AHEAD) % FFN_SLOTS
        ahead_tok0 = ((p + FFN_AHEAD) % FFN_PHASES) * ctok
        starts = iter([(ahead_tok0 + c, ahead_slot, c, j) for c in range(ctok) for j in range(PEER_E)])

        def dot_step(c, e, xt, partial):
            q = expert_tile(slot, c, e, 0) * xt
            col = jnp.sum(q[:SUBLANES] + q[SUBLANES:], axis=1, keepdims=True)
            partial[e % DOT_CHAINS] = jnp.where(lane == e, col, partial[e % DOT_CHAINS])

        prev = None
        for c in range(ctok + 1):
            if c < ctok:
                xt = token_tile(x_ref, p * ctok + c)
                partial = [jnp.zeros((SUBLANES, LANES), F32) for _ in range(DOT_CHAINS)]
            ya = [jnp.zeros((SUBLANES, LANES), F32) for _ in range(4)]
            for g in range(NGROUP):
                if prev is not None:
                    onto = jnp.sum(jnp.where(lane == g * GSIZE + sub, prev[1], 0.0), axis=1, keepdims=True)
                    a_g = jnp.broadcast_to(onto, (SUBLANES, LANES))
                for k in range(GSIZE):
                    e = g * GSIZE + k
                    for args in ([next(starts, None)] if c < ctok else []):
                        if args is not None:
                            start_row(*args)
                    if c < ctok:
                        dot_step(c, e, xt, partial)
                    if prev is not None:
                        a = jnp.broadcast_to(a_g[k:k + 1, :], (SUBLANES, LANES))
                        vt = expert_tile(slot, prev[0], e, 1)
                        ya[2 * (k % 2)] = ya[2 * (k % 2)] + a * vt[:SUBLANES]
                        ya[2 * (k % 2) + 1] = ya[2 * (k % 2) + 1] + a * vt[SUBLANES:]
            if prev is not None:
                tok = p * ctok + prev[0]
                z = token_tile(h_ref, tok) + jnp.concatenate([ya[0] + ya[2], ya[1] + ya[3]], axis=0)
                ms = jnp.sum(jnp.sum(z * z, axis=1, keepdims=True), axis=0, keepdims=True) * (1.0 / (nrow * LANES))
                results.append((tok, z * lax.rsqrt(ms + EPS) * gf_ref[...]))
            if c < ctok:
                tok = p * ctok + c
                dots = jnp.sum(functools.reduce(jnp.add, partial), axis=0, keepdims=True)
                act = _gelu_exact(dots) * gate_ref[tok:tok + 1, :]
                prev = (c, jnp.broadcast_to(act, (SUBLANES, LANES)))
            else:
                prev = None
        if len(results) == SUBLANES:
            for tok, zn in results:
                for s in range(nrow):
                    o_ref[tok:tok + 1, s * LANES:(s + 1) * LANES] = zn[s:s + 1, :]
            results = []

    @pl.when(i == n - 1)
    def _():
        for p in range(FFN_AHEAD):
            wait_slot(p)
        idx_copy(1, 0).wait()
        idx_copy(2, 0).wait()


def _peer_ffn(idx, gate, xn, h, gf, table, *, ctok=4):
    T, nexp = idx.shape
    D = xn.shape[1]
    nrow = D // LANES
    step = FFN_PHASES * ctok
    half = step // 2
    assert nexp == PEER_E and SUBLANES % ctok == 0 and half % (2 * SUBLANES) == 0 and T % step == 0 and T >= 2 * step
    assert table.shape[1:] == (2 * nrow, LANES) and table.dtype == BF16 and nrow % (2 * SUBLANES) == 0
    assert FFN_PHASES % FFN_SLOTS == 0 and 0 < FFN_AHEAD < FFN_SLOTS and FFN_AHEAD * ctok == half // 2
    n = T // step
    slot_rows = ctok * nexp * 2 * nrow
    est = FFN_SLOTS * slot_rows * LANES * 2 + 6 * step * D * 4
    return pl.pallas_call(
        functools.partial(_peer_ffn_body, ctok=ctok, nrow=nrow, ntok=T),
        grid=(n,),
        in_specs=[
            pl.BlockSpec(memory_space=pl.ANY),
            pl.BlockSpec((step, nexp), lambda i: (i, 0)),
            pl.BlockSpec((step, D), lambda i: (i, 0)),
            pl.BlockSpec((step, D), lambda i: (i, 0)),
            pl.BlockSpec((nrow, LANES), lambda i: (0, 0)),
            pl.BlockSpec(memory_space=pl.ANY),
        ],
        out_specs=pl.BlockSpec((step, D), lambda i: (i, 0)),
        out_shape=jax.ShapeDtypeStruct((T, D), F32),
        scratch_shapes=[pltpu.VMEM((slot_rows, LANES), BF16) for _ in range(FFN_SLOTS)]
                       + [pltpu.SMEM((half, nexp), jnp.int32), pltpu.SMEM((half, nexp), jnp.int32),
                          pltpu.SemaphoreType.DMA((FFN_SLOTS,)), pltpu.SemaphoreType.DMA((3,))],
        compiler_params=pltpu.CompilerParams(
            dimension_semantics=("arbitrary",),
            vmem_limit_bytes=_vmem_limit(est)),
        name="peer_ffn",
    )(idx, gate, xn, h, gf, table)


def _expert_table(u, v):
    n, d = u.shape
    both = jnp.concatenate([u.astype(BF16).reshape(n, d // LANES, LANES),
                            v.astype(BF16).reshape(n, d // LANES, LANES)], axis=1)
    return both


def _rope_tables(seq):
    half = QK_ROPE // 2
    freqs = ROPE_THETA ** (-jnp.arange(half, dtype=F32) / half)
    ang = jnp.arange(seq, dtype=F32)[:, None] * freqs[None, :]
    cos, sin = jnp.cos(ang), jnp.sin(ang)
    zeros = jnp.zeros((seq, LANES - QK_ROPE), F32)
    return (jnp.concatenate([cos, cos, zeros], axis=1),
            jnp.concatenate([-sin, sin, zeros], axis=1))


def _layer(h, ln1_g, w_in, q_a_norm, kv_a_norm, w_uq, w_ukv, out_norm_dil, out_norm_mla, w_o,
           ln2_g, peer_wq, peer_sub_keys, peer_u, peer_v, final_g):
    B, S, D = h.shape
    T = B * S
    H = N_HEADS
    d_dil = H * HEAD_DIM
    q_lora = q_a_norm.shape[0]
    kv_lora = kv_a_norm.shape[0]
    x2 = h.reshape(T, D)

    d_in = w_in.shape[1]
    d_in_pad = -(-d_in // LANES) * LANES
    w_in_p = jnp.pad(w_in, ((0, 0), (0, d_in_pad - d_in))).astype(BF16)
    z = _norm_matmul(x2, ln1_g, w_in_p, tm=ROW_TILE,
                     tn=d_in_pad // 3 if d_in_pad % (3 * LANES) == 0 else LANES)
    z3 = z.reshape(B, S, d_in_pad)

    slopes = 2.0 ** (-8.0 * jnp.arange(1, H + 1, dtype=F32) / H)
    o_a = _dilated_attention(z3, slopes, n_heads=H)

    wq = w_uq.reshape(q_lora, H, QK_NOPE + QK_ROPE)
    wq_pe = jnp.pad(wq[:, :, QK_NOPE:], ((0, 0), (0, 0), (0, LANES - QK_ROPE)))
    wq_p = jnp.concatenate([wq[:, :, :QK_NOPE].reshape(q_lora, H * QK_NOPE),
                            wq_pe.reshape(q_lora, H * LANES)], axis=1).astype(BF16)
    wkv = w_ukv.reshape(kv_lora, H, QK_NOPE + HEAD_DIM)
    wkv_p = jnp.concatenate([wkv[:, :, :QK_NOPE].reshape(kv_lora, H * QK_NOPE),
                             wkv[:, :, QK_NOPE:].reshape(kv_lora, H * HEAD_DIM)], axis=1).astype(BF16)
    assert q_lora == kv_lora and (3 * d_dil) % q_lora == 0
    q_mla = _norm_matmul(z, q_a_norm, wq_p, xcol=3 * d_dil // q_lora, kdim=q_lora, out_dtype=BF16,
                         tm=ROW_TILE, tn=wq_p.shape[1])
    kv_mla = _norm_matmul(z, kv_a_norm, wkv_p, xcol=3 * d_dil // q_lora + 1, kdim=kv_lora, out_dtype=BF16,
                          tm=ROW_TILE, tn=wkv_p.shape[1])
    cos, sin_signed = _rope_tables(S)
    o_b = _mla_attention(q_mla.reshape(B, S, -1), kv_mla.reshape(B, S, -1), z3,
                         (3 * d_dil + q_lora + kv_lora) // LANES, cos, sin_signed, n_heads=H)

    h1 = _outproj(o_a.reshape(T, d_dil), o_b.reshape(T, -1), out_norm_dil, out_norm_mla,
                  w_o.astype(BF16), x2, tm=ROW_TILE)

    q_slabs, xn2 = _norm_matmul(h1, ln2_g, peer_wq.astype(BF16), tm=ROW_TILE, emit_xn=True,
                                slab_out=True, out_dtype=BF16)
    keys = peer_sub_keys.reshape(PEER_HEADS * 2, PEER_NKEYS, -1).astype(BF16)
    idx_t, gate_t = _peer_topk(q_slabs, keys)
    table = _expert_table(peer_u, peer_v)
    out = _peer_ffn(idx_t.T, gate_t.T, xn2, h1, final_g.reshape(D // LANES, LANES), table)
    return out.reshape(B, S, D)


def kernel(x, ln1_g, w_in, q_a_norm, kv_a_norm, w_uq, w_ukv, out_norm_dil, out_norm_mla, w_o,
           ln2_g, peer_wq, peer_sub_keys, peer_u, peer_v, lnf_g):
    assert ln1_g.shape[0] == 1, "single-layer trunk"
    return _layer(x, ln1_g[0], w_in[0], q_a_norm[0], kv_a_norm[0], w_uq[0], w_ukv[0],
                  out_norm_dil[0], out_norm_mla[0], w_o[0], ln2_g[0], peer_wq[0],
                  peer_sub_keys[0], peer_u[0], peer_v[0], lnf_g)
```

```python
import functools

import jax
import jax.numpy as jnp
from jax import lax
from jax.experimental import pallas as pl
from jax.experimental.pallas import tpu as pltpu

F32 = jnp.float32
BF16 = jnp.bfloat16

EPS = 1e-6
NEG = -1e30
HEAD_DIM = 128
BLOCK = 128
DIL_PATTERNS = ((128, 1), (512, 4), (2048, 16))
DIL_UNROLL = 16
N_HEADS = 8
QK_NOPE = 128
QK_ROPE = 64
ROPE_THETA = 10000.0
PEER_HEADS = 8
PEER_NKEYS = 128
PEER_TOPK = 16
PEER_E = PEER_HEADS * PEER_TOPK

LANES = 128
SUBLANES = 8
VMEM_CAP = 60000 * 1024
ROW_TILE = 1024


def _vmem_limit(nbytes):
    return int(min(VMEM_CAP, max(16 * 1024 * 1024, nbytes * 3 // 2)))


def _rms(x, g):
    return x * lax.rsqrt(jnp.mean(x * x, axis=-1, keepdims=True) + EPS) * g


def _norm_matmul_body(x_ref, g_ref, w_ref, *rest, emit_xn, slabs):
    if emit_xn:
        o_ref, xn_out_ref, xn_ref = rest
    else:
        o_ref, xn_ref = rest
    j = pl.program_id(1)

    @pl.when(j == 0)
    def _():
        xn = _rms(x_ref[...].astype(F32), g_ref[...])
        xn_ref[...] = xn.astype(BF16)
        if emit_xn:
            xn_out_ref[...] = xn

    res = jnp.dot(xn_ref[...], w_ref[...], preferred_element_type=F32).astype(o_ref.dtype)
    if slabs:
        for s in range(slabs):
            o_ref[s] = res[:, s * LANES:(s + 1) * LANES]
    else:
        o_ref[...] = res


def _norm_matmul(x, g, w, *, xcol=0, kdim=None, tm=512, tn=512, emit_xn=False, slab_out=False,
                 out_dtype=F32):
    T = x.shape[0]
    kdim = kdim or x.shape[1]
    N = w.shape[1]
    tm = min(tm, T)
    tn = min(tn, N)
    assert T % tm == 0 and N % tn == 0 and w.shape[0] == kdim
    slabs = tn // LANES if slab_out else 0
    if slab_out:
        out_shape = [jax.ShapeDtypeStruct((N // LANES, T, LANES), out_dtype)]
        out_specs = [pl.BlockSpec((slabs, tm, LANES), lambda i, j: (j, i, 0))]
    else:
        out_shape = [jax.ShapeDtypeStruct((T, N), out_dtype)]
        out_specs = [pl.BlockSpec((tm, tn), lambda i, j: (i, j))]
    if emit_xn:
        out_shape.append(jax.ShapeDtypeStruct((T, kdim), F32))
        out_specs.append(pl.BlockSpec((tm, kdim), lambda i, j: (i, 0)))
    est = 2 * (tm * kdim * 4 + kdim * tn * 2 + tm * tn * 4) + tm * kdim * 2
    if emit_xn:
        est += 2 * tm * kdim * 4
    outs = pl.pallas_call(
        functools.partial(_norm_matmul_body, emit_xn=emit_xn, slabs=slabs),
        grid=(T // tm, N // tn),
        in_specs=[
            pl.BlockSpec((tm, kdim), lambda i, j: (i, xcol)),
            pl.BlockSpec((1, kdim), lambda i, j: (0, 0)),
            pl.BlockSpec((kdim, tn), lambda i, j: (0, j)),
        ],
        out_specs=out_specs,
        out_shape=out_shape,
        scratch_shapes=[pltpu.VMEM((tm, kdim), BF16)],
        compiler_params=pltpu.CompilerParams(
            dimension_semantics=("arbitrary", "arbitrary"),
            vmem_limit_bytes=_vmem_limit(est)),
        name="norm_matmul",
    )(x, g.reshape(1, kdim).astype(F32), w)
    return outs if emit_xn else outs[0]


def _dilated_body(slopes_ref, q_ref, k_ref, v_ref, o_ref, o_scr, l_scr, *, seq, patterns, scale):
    h = pl.program_id(1)
    slope = slopes_ref[h]
    qi = lax.broadcasted_iota(jnp.int32, (BLOCK, 2 * BLOCK), 0)
    kj = lax.broadcasted_iota(jnp.int32, (BLOCK, 2 * BLOCK), 1)
    delta = qi + BLOCK - kj
    nt = (((1,), (1,)), ((), ()))

    for p, (window, d) in enumerate(patterns):
        steps = window // d
        nb = seq // d // BLOCK
        in_window = (delta >= 0) & (delta <= steps)
        bias = -slope * (delta * d).astype(F32)

        def block(t, carry, d=d, nb=nb, in_window=in_window, bias=bias, p=p):
            r = t // nb
            n = t % nb
            start = n * (BLOCK * d) + r
            pstart = jnp.maximum(n - 1, 0) * (BLOCK * d) + r
            rows = pl.ds(start, BLOCK, stride=d) if d > 1 else pl.ds(start, BLOCK)
            prow = pl.ds(pstart, BLOCK, stride=d) if d > 1 else pl.ds(pstart, BLOCK)
            q = q_ref[0, rows, :].astype(BF16)
            kk = jnp.concatenate([k_ref[0, prow, :], k_ref[0, rows, :]], axis=0).astype(BF16)
            vv = jnp.concatenate([v_ref[0, prow, :], v_ref[0, rows, :]], axis=0).astype(BF16)
            s = lax.dot_general(q, kk, nt, preferred_element_type=F32) * scale
            valid = in_window & ((kj >= BLOCK) | (n > 0))
            s = jnp.where(valid, s + bias, NEG)
            m = jnp.max(s, axis=-1, keepdims=True)
            e = jnp.exp(s - m)
            l = jnp.sum(e, axis=-1, keepdims=True)
            o = jnp.dot((e / l).astype(BF16), vv, preferred_element_type=F32)
            o_scr[p, rows, :] = o
            l_scr[p, rows, :] = jnp.broadcast_to(m + jnp.log(l), (BLOCK, HEAD_DIM))
            return carry

        lax.fori_loop(0, d * nb, block, 0, unroll=DIL_UNROLL)

    def mix(c, carry):
        rows = pl.ds(pl.multiple_of(c * BLOCK, BLOCK), BLOCK)
        ls = [l_scr[p, rows, :] for p in range(len(patterns))]
        m = functools.reduce(jnp.maximum, ls)
        es = [jnp.exp(l - m) for l in ls]
        den = functools.reduce(jnp.add, es)
        num = functools.reduce(jnp.add, [e * o_scr[p, rows, :] for p, e in enumerate(es)])
        o_ref[0, rows, :] = (num / den).astype(o_ref.dtype)
        return carry

    lax.fori_loop(0, seq // BLOCK, mix, 0)


def _dilated_attention(z3, slopes, *, n_heads, patterns=DIL_PATTERNS):
    B, S, _ = z3.shape
    for _, d in patterns:
        assert S % (d * BLOCK) == 0
    blk = (1, S, HEAD_DIM)
    est = 2 * 4 * S * HEAD_DIM * 4 + 2 * len(patterns) * S * HEAD_DIM * 4
    return pl.pallas_call(
        functools.partial(_dilated_body, seq=S, patterns=patterns, scale=HEAD_DIM ** -0.5),
        grid=(B, n_heads),
        in_specs=[
            pl.BlockSpec(memory_space=pltpu.SMEM),
            pl.BlockSpec(blk, lambda b, h: (b, 0, h)),
            pl.BlockSpec(blk, lambda b, h: (b, 0, n_heads + h)),
            pl.BlockSpec(blk, lambda b, h: (b, 0, 2 * n_heads + h)),
        ],
        out_specs=pl.BlockSpec(blk, lambda b, h: (b, 0, h)),
        out_shape=jax.ShapeDtypeStruct((B, S, n_heads * HEAD_DIM), BF16),
        scratch_shapes=[pltpu.VMEM((len(patterns), S, HEAD_DIM), F32),
                        pltpu.VMEM((len(patterns), S, HEAD_DIM), F32)],
        compiler_params=pltpu.CompilerParams(
            dimension_semantics=("arbitrary", "arbitrary"),
            vmem_limit_bytes=_vmem_limit(est)),
        name="dilated_attention",
    )(slopes, z3, z3, z3)


def _rope(x, cos, sin_signed):
    lane = lax.broadcasted_iota(jnp.int32, x.shape, 1)
    half = QK_ROPE // 2
    swapped = jnp.where(lane < half, pltpu.roll(x, LANES - half, 1), pltpu.roll(x, half, 1))
    return x * cos + swapped * sin_signed


def _mla_body(qn_ref, qp_ref, kn_ref, v_ref, kr_ref, cos_ref, sin_ref,
              o_ref, qcat_scr, kcat_scr, *, seq, tq, scale):
    cos, sin = cos_ref[...], sin_ref[...]
    qcat_scr[:, :LANES] = qn_ref[0]
    qcat_scr[:, LANES:] = _rope(qp_ref[0].astype(F32), cos, sin).astype(BF16)
    kcat_scr[:, :LANES] = kn_ref[0]
    kcat_scr[:, LANES:] = _rope(kr_ref[0], cos, sin).astype(BF16)
    nt = (((1,), (1,)), ((), ()))

    for qi in range(seq // tq):
        ext = (qi + 1) * tq
        rows = slice(qi * tq, ext)
        s = lax.dot_general(qcat_scr[rows, :], kcat_scr[:ext, :], nt, preferred_element_type=F32) * scale
        qpos = qi * tq + lax.broadcasted_iota(jnp.int32, (tq, ext), 0)
        kpos = lax.broadcasted_iota(jnp.int32, (tq, ext), 1)
        s = jnp.where(kpos <= qpos, s, NEG)
        e = jnp.exp(s - jnp.max(s, axis=-1, keepdims=True))
        l = jnp.sum(e, axis=-1, keepdims=True)
        o = jnp.dot(e.astype(BF16), v_ref[0, :ext, :], preferred_element_type=F32) / l
        o_ref[0, rows, :] = o.astype(o_ref.dtype)


def _mla_attention(q3, kv3, z3, kr_col, cos, sin_signed, *, n_heads, tq=512):
    B, S, _ = q3.shape
    tq = min(tq, S)
    assert S % tq == 0 and q3.dtype == BF16 and kv3.dtype == BF16
    blk = (1, S, LANES)
    est = 2 * (5 * 2 + 3 * 4) * S * LANES + 4 * S * LANES * 2 + 3 * tq * S * 4
    return pl.pallas_call(
        functools.partial(_mla_body, seq=S, tq=tq, scale=(QK_NOPE + QK_ROPE) ** -0.5),
        grid=(B, n_heads),
        in_specs=[
            pl.BlockSpec(blk, lambda b, h: (b, 0, h)),
            pl.BlockSpec(blk, lambda b, h: (b, 0, n_heads + h)),
            pl.BlockSpec(blk, lambda b, h: (b, 0, h)),
            pl.BlockSpec(blk, lambda b, h: (b, 0, n_heads + h)),
            pl.BlockSpec(blk, lambda b, h: (b, 0, kr_col)),
            pl.BlockSpec((S, LANES), lambda b, h: (0, 0)),
            pl.BlockSpec((S, LANES), lambda b, h: (0, 0)),
        ],
        out_specs=pl.BlockSpec(blk, lambda b, h: (b, 0, h)),
        out_shape=jax.ShapeDtypeStruct((B, S, n_heads * HEAD_DIM), BF16),
        scratch_shapes=[pltpu.VMEM((S, 2 * LANES), BF16), pltpu.VMEM((S, 2 * LANES), BF16)],
        compiler_params=pltpu.CompilerParams(
            dimension_semantics=("arbitrary", "arbitrary"),
            vmem_limit_bytes=_vmem_limit(est)),
        name="mla_attention",
    )(q3, q3, kv3, kv3, z3, cos, sin_signed)


def _outproj_body(oa_ref, ob_ref, ga_ref, gb_ref, w_ref, x_ref, o_ref, xn_ref, *, da):
    j = pl.program_id(1)

    @pl.when(j == 0)
    def _():
        xn_ref[:, :da] = _rms(oa_ref[...].astype(F32), ga_ref[...]).astype(BF16)
        xn_ref[:, da:] = _rms(ob_ref[...].astype(F32), gb_ref[...]).astype(BF16)

    o_ref[...] = x_ref[...] + jnp.dot(xn_ref[...], w_ref[...], preferred_element_type=F32)


def _outproj(oa, ob, ga, gb, w, x, *, tm=512, tn=512):
    T, da = oa.shape
    db = ob.shape[1]
    N = w.shape[1]
    tm, tn = min(tm, T), min(tn, N)
    assert T % tm == 0 and N % tn == 0 and w.shape[0] == da + db
    est = 2 * (tm * (da + db) * 4 + (da + db) * tn * 2 + 2 * tm * tn * 4) + tm * (da + db) * 2
    return pl.pallas_call(
        functools.partial(_outproj_body, da=da),
        grid=(T // tm, N // tn),
        in_specs=[
            pl.BlockSpec((tm, da), lambda i, j: (i, 0)),
            pl.BlockSpec((tm, db), lambda i, j: (i, 0)),
            pl.BlockSpec((1, da), lambda i, j: (0, 0)),
            pl.BlockSpec((1, db), lambda i, j: (0, 0)),
            pl.BlockSpec((da + db, tn), lambda i, j: (0, j)),
            pl.BlockSpec((tm, tn), lambda i, j: (i, j)),
        ],
        out_specs=pl.BlockSpec((tm, tn), lambda i, j: (i, j)),
        out_shape=jax.ShapeDtypeStruct((T, N), F32),
        scratch_shapes=[pltpu.VMEM((tm, da + db), BF16)],
        compiler_params=pltpu.CompilerParams(
            dimension_semantics=("arbitrary", "arbitrary"),
            vmem_limit_bytes=_vmem_limit(est)),
        name="out_projection",
    )(oa, ob, ga.reshape(1, da), gb.reshape(1, db), w, x)


TOPK_HEADS = 4


def _candidate_blocks(topk):
    blocks = []
    a = 0
    while a < topk:
        nb = topk // (a + 1)
        if nb >= SUBLANES:
            blocks += [(a, 1, b0, min(SUBLANES, nb - b0)) for b0 in range(0, nb, SUBLANES)]
            a += 1
        elif nb > 1:
            blocks.append((a, 1, 0, nb))
            a += 1
        else:
            na = min(SUBLANES, topk - a)
            blocks.append((a, na, 0, 1))
            a += na
    return blocks


def _topk_body(q_ref, keys_ref, idx_ref, gate_ref, s_ref, sv_ref, si_ref, cs_ref, ci_ref, bs_ref, be_ref,
               *, tt, nkeys, topk, blocks):
    lowest = float(jnp.finfo(jnp.float32).min)
    nchain = 2 * TOPK_HEADS
    iota_n = lax.broadcasted_iota(jnp.int32, (nkeys, tt), 0)
    sub = lax.broadcasted_iota(jnp.int32, (SUBLANES, tt), 0)
    nt = (((1,), (1,)), ((), ()))

    for ch in range(nchain):
        s_ref[ch] = lax.dot_general(keys_ref[ch], q_ref[ch], nt, preferred_element_type=F32)

    def pick(k, carry):
        for ch in range(nchain):
            s = s_ref[ch]
            m = jnp.max(s, axis=0, keepdims=True)
            ix = jnp.min(jnp.where(s == m, iota_n, nkeys), axis=0, keepdims=True)
            sv_ref[ch, pl.ds(k, 1), :] = m
            si_ref[ch, pl.ds(k, 1), :] = ix
            s_ref[ch] = jnp.where(iota_n == ix, lowest, s)
        return carry

    lax.fori_loop(0, topk, pick, 0)

    pos_blocks = []
    for r, (a0, na, b0, nb) in enumerate(blocks):
        rows = slice(r * SUBLANES, (r + 1) * SUBLANES)
        if na == 1:
            live = sub < nb
            pos_blocks.append(a0 * topk + b0 + sub)
        else:
            live = sub < na
            pos_blocks.append((a0 + sub) * topk)
        for hd in range(TOPK_HEADS):
            if na == 1:
                cs = sv_ref[2 * hd, a0:a0 + 1, :] + sv_ref[2 * hd + 1, b0:b0 + SUBLANES, :]
                ci = si_ref[2 * hd, a0:a0 + 1, :] * nkeys + si_ref[2 * hd + 1, b0:b0 + SUBLANES, :]
            else:
                cs = sv_ref[2 * hd, a0:a0 + SUBLANES, :] + sv_ref[2 * hd + 1, 0:1, :]
                ci = si_ref[2 * hd, a0:a0 + SUBLANES, :] * nkeys + si_ref[2 * hd + 1, 0:1, :]
            cs_ref[hd, rows, :] = jnp.where(live, cs, lowest)
            ci_ref[hd, rows, :] = ci
    pos = jnp.concatenate(pos_blocks, axis=0)
    big = topk * topk

    def pick2(k, carry):
        for hd in range(TOPK_HEADS):
            c = cs_ref[hd]
            m = jnp.max(c, axis=0, keepdims=True)
            first = jnp.min(jnp.where(c == m, pos, big), axis=0, keepdims=True)
            hit = pos == first
            bs_ref[hd, pl.ds(k, 1), :] = m
            be_ref[hd, pl.ds(k, 1), :] = jnp.max(jnp.where(hit, ci_ref[hd], -1), axis=0, keepdims=True)
            cs_ref[hd] = jnp.where(hit, lowest, c)
        return carry

    lax.fori_loop(0, topk, pick2, 0)

    for hd in range(TOPK_HEADS):
        b = bs_ref[hd]
        e = jnp.exp(b - jnp.max(b, axis=0, keepdims=True))
        gate_ref[hd * topk:(hd + 1) * topk, :] = e / jnp.sum(e, axis=0, keepdims=True)
        idx_ref[hd * topk:(hd + 1) * topk, :] = be_ref[hd]


def _peer_topk(q_slabs, keys, *, tt=128):
    hp, T, c = q_slabs.shape
    heads = hp // 2
    nkeys = keys.shape[1]
    tt = min(tt, T)
    topk = PEER_TOPK
    assert T % tt == 0 and heads % TOPK_HEADS == 0 and topk % SUBLANES == 0
    blocks = _candidate_blocks(topk)
    ncand = len(blocks) * SUBLANES
    nchain = 2 * TOPK_HEADS
    rows = TOPK_HEADS * topk
    return pl.pallas_call(
        functools.partial(_topk_body, tt=tt, nkeys=nkeys, topk=topk, blocks=blocks),
        grid=(T // tt, heads // TOPK_HEADS),
        in_specs=[
            pl.BlockSpec((nchain, tt, c), lambda i, h: (h, i, 0)),
            pl.BlockSpec((nchain, nkeys, c), lambda i, h: (h, 0, 0)),
        ],
        out_specs=[pl.BlockSpec((rows, tt), lambda i, h: (h, i)),
                   pl.BlockSpec((rows, tt), lambda i, h: (h, i))],
        out_shape=[jax.ShapeDtypeStruct((heads * topk, T), jnp.int32),
                   jax.ShapeDtypeStruct((heads * topk, T), F32)],
        scratch_shapes=[pltpu.VMEM((nchain, nkeys, tt), F32),
                        pltpu.VMEM((nchain, topk, tt), F32), pltpu.VMEM((nchain, topk, tt), jnp.int32),
                        pltpu.VMEM((TOPK_HEADS, ncand, tt), F32), pltpu.VMEM((TOPK_HEADS, ncand, tt), jnp.int32),
                        pltpu.VMEM((TOPK_HEADS, topk, tt), F32), pltpu.VMEM((TOPK_HEADS, topk, tt), jnp.int32)],
        compiler_params=pltpu.CompilerParams(dimension_semantics=("arbitrary", "arbitrary")),
        name="peer_topk",
    )(q_slabs, keys)


GSIZE = SUBLANES
NGROUP = PEER_E // GSIZE
DOT_CHAINS = 4
FFN_SLOTS = 4
FFN_PHASES = 8
FFN_AHEAD = 2


def _gelu_exact(x):
    return 0.5 * x * (1.0 + lax.erf(x * (2.0 ** -0.5)))


def _peer_ffn_body(idx_hbm, gate_ref, x_ref, h_ref, gf_ref, tab_ref, o_ref, *scratch,
                   ctok, nrow, ntok):
    bufs = scratch[:FFN_SLOTS]
    idx_a, idx_b, sem, isem = scratch[FFN_SLOTS:]
    i = pl.program_id(0)
    n = pl.num_programs(0)
    sub = lax.broadcasted_iota(jnp.int32, (SUBLANES, LANES), 0)
    lane = lax.broadcasted_iota(jnp.int32, (SUBLANES, LANES), 1)
    step = FFN_PHASES * ctok
    half = step // 2
    tile = 2 * nrow

    def idx_copy(part, base):
        lo, cnt, dst = ((0, half // 2, idx_a), (half // 2, half // 2, idx_a), (half, half, idx_b))[part]
        src = idx_hbm.at[pl.ds(pl.multiple_of(base + lo, SUBLANES), cnt), :]
        return pltpu.make_async_copy(src, dst.at[pl.ds(lo % half, cnt), :], isem.at[part])

    next_base = jnp.minimum((i + 1) * step, ntok - step)

    def idx_of(tok, j):
        return idx_a[tok, j] if tok < half else idx_b[tok - half, j]

    def start_row(tok, dst_slot, c, j):
        pltpu.make_async_copy(tab_ref.at[idx_of(tok, j)], bufs[dst_slot].at[pl.ds((c * PEER_E + j) * tile, tile), :],
                              sem.at[dst_slot]).start(priority=j % 2)

    def wait_slot(s):
        pltpu.make_async_copy(bufs[s], bufs[s], sem.at[s]).wait()

    def expert_tile(slot, c, j, which):
        return bufs[slot][pl.ds((c * PEER_E + j) * tile + which * nrow, nrow), :].astype(F32)

    def token_tile(ref, tok):
        return jnp.concatenate([ref[tok:tok + 1, s * LANES:(s + 1) * LANES] for s in range(nrow)], axis=0)

    @pl.when(i == 0)
    def _():
        first_ids = idx_copy(0, 0)
        first_ids.start()
        first_ids.wait()
        idx_copy(1, 0).start()
        idx_copy(2, 0).start()
        for p in range(FFN_AHEAD):
            def first(c, carry, p=p):
                for j in range(PEER_E):
                    first_row = pl.multiple_of((c * PEER_E + j) * tile, tile)
                    pltpu.make_async_copy(tab_ref.at[idx_a[p * ctok + c, j]],
                                          bufs[p].at[pl.ds(first_row, tile), :], sem.at[p]).start(priority=j % 2)
                return carry
            lax.fori_loop(0, ctok, first, 0)

    idx_copy(0, next_base).start()
    idx_copy(1, 0).wait()
    results = []
    prev = None
    starts = iter(())
    for t in range(step + 1):
        cur = t < step
        if cur:
            p, c = divmod(t, ctok)
            if c == 0:
                if p * ctok == half // 2:
                    idx_copy(1, next_base).start()
                    idx_copy(2, 0).wait()
                if (p + FFN_AHEAD) * ctok == step:
                    idx_copy(2, next_base).start()
                    idx_copy(0, 0).wait()
                slot = p % FFN_SLOTS
                wait_slot(slot)
                ahead_slot = (p + FFN_AHEAD) % FFN_SLOTS
                ahead_tok0 = ((p + FFN_AHEAD) % FFN_PHASES) * ctok
                starts = iter([(ahead_tok0 + cc, ahead_slot, cc, j) for cc in range(ctok) for j in range(PEER_E)])
            xt = token_tile(x_ref, t)
            partial = [jnp.zeros((SUBLANES, LANES), F32) for _ in range(DOT_CHAINS)]
        ya = [jnp.zeros((SUBLANES, LANES), F32) for _ in range(4)]
        for g in range(NGROUP):
            if prev is not None:
                onto = jnp.sum(jnp.where(lane == g * GSIZE + sub, prev[3], 0.0), axis=1, keepdims=True)
                a_g = jnp.broadcast_to(onto, (SUBLANES, LANES))
            for k in range(GSIZE):
                e = g * GSIZE + k
                if cur:
                    args = next(starts, None)
                    if args is not None:
                        start_row(*args)
                    q = expert_tile(slot, c, e, 0) * xt
                    col = jnp.sum(q[:SUBLANES] + q[SUBLANES:], axis=1, keepdims=True)
                    partial[e % DOT_CHAINS] = jnp.where(lane == e, col, partial[e % DOT_CHAINS])
                if prev is not None:
                    a = jnp.broadcast_to(a_g[k:k + 1, :], (SUBLANES, LANES))
                    vt = expert_tile(prev[0], prev[1], e, 1)
                    ya[2 * (k % 2)] = ya[2 * (k % 2)] + a * vt[:SUBLANES]
                    ya[2 * (k % 2) + 1] = ya[2 * (k % 2) + 1] + a * vt[SUBLANES:]
        if prev is not None:
            z = token_tile(h_ref, prev[2]) + jnp.concatenate([ya[0] + ya[2], ya[1] + ya[3]], axis=0)
            ms = jnp.sum(jnp.sum(z * z, axis=1, keepdims=True), axis=0, keepdims=True) * (1.0 / (nrow * LANES))
            results.append((prev[2], z * lax.rsqrt(ms + EPS) * gf_ref[...]))
            if len(results) == SUBLANES:
                for tok, zn in results:
                    for s in range(nrow):
                        o_ref[tok:tok + 1, s * LANES:(s + 1) * LANES] = zn[s:s + 1, :]
                results = []
        if cur:
            dots = jnp.sum(functools.reduce(jnp.add, partial), axis=0, keepdims=True)
            act = _gelu_exact(dots) * gate_ref[t:t + 1, :]
            prev = (slot, c, t, jnp.broadcast_to(act, (SUBLANES, LANES)))
        else:
            prev = None

    @pl.when(i == n - 1)
    def _():
        for p in range(FFN_AHEAD):
            wait_slot(p)
        idx_copy(1, 0).wait()
        idx_copy(2, 0).wait()


def _peer_ffn(idx, gate, xn, h, gf, table, *, ctok=4):
    T, nexp = idx.shape
    D = xn.shape[1]
    nrow = D // LANES
    step = FFN_PHASES * ctok
    half = step // 2
    assert nexp == PEER_E and SUBLANES % ctok == 0 and half % (2 * SUBLANES) == 0 and T % step == 0 and T >= 2 * step
    assert table.shape[1:] == (2 * nrow, LANES) and table.dtype == BF16 and nrow % (2 * SUBLANES) == 0
    assert FFN_PHASES % FFN_SLOTS == 0 and 0 < FFN_AHEAD < FFN_SLOTS and FFN_AHEAD * ctok == half // 2
    n = T // step
    slot_rows = ctok * nexp * 2 * nrow
    est = FFN_SLOTS * slot_rows * LANES * 2 + 6 * step * D * 4
    return pl.pallas_call(
        functools.partial(_peer_ffn_body, ctok=ctok, nrow=nrow, ntok=T),
        grid=(n,),
        in_specs=[
            pl.BlockSpec(memory_space=pl.ANY),
            pl.BlockSpec((step, nexp), lambda i: (i, 0)),
            pl.BlockSpec((step, D), lambda i: (i, 0)),
            pl.BlockSpec((step, D), lambda i: (i, 0)),
            pl.BlockSpec((nrow, LANES), lambda i: (0, 0)),
            pl.BlockSpec(memory_space=pl.ANY),
        ],
        out_specs=pl.BlockSpec((step, D), lambda i: (i, 0)),
        out_shape=jax.ShapeDtypeStruct((T, D), F32),
        scratch_shapes=[pltpu.VMEM((slot_rows, LANES), BF16) for _ in range(FFN_SLOTS)]
                       + [pltpu.SMEM((half, nexp), jnp.int32), pltpu.SMEM((half, nexp), jnp.int32),
                          pltpu.SemaphoreType.DMA((FFN_SLOTS,)), pltpu.SemaphoreType.DMA((3,))],
        compiler_params=pltpu.CompilerParams(
            dimension_semantics=("arbitrary",),
            vmem_limit_bytes=_vmem_limit(est)),
        name="peer_ffn",
    )(idx, gate, xn, h, gf, table)


def _expert_table(u, v):
    n, d = u.shape
    both = jnp.concatenate([u.astype(BF16).reshape(n, d // LANES, LANES),
                            v.astype(BF16).reshape(n, d // LANES, LANES)], axis=1)
    return both


def _rope_tables(seq):
    half = QK_ROPE // 2
    freqs = ROPE_THETA ** (-jnp.arange(half, dtype=F32) / half)
    ang = jnp.arange(seq, dtype=F32)[:, None] * freqs[None, :]
    cos, sin = jnp.cos(ang), jnp.sin(ang)
    zeros = jnp.zeros((seq, LANES - QK_ROPE), F32)
    return (jnp.concatenate([cos, cos, zeros], axis=1),
            jnp.concatenate([-sin, sin, zeros], axis=1))


def _layer(h, ln1_g, w_in, q_a_norm, kv_a_norm, w_uq, w_ukv, out_norm_dil, out_norm_mla, w_o,
           ln2_g, peer_wq, peer_sub_keys, peer_u, peer_v, final_g):
    B, S, D = h.shape
    T = B * S
    H = N_HEADS
    d_dil = H * HEAD_DIM
    q_lora = q_a_norm.shape[0]
    kv_lora = kv_a_norm.shape[0]
    x2 = h.reshape(T, D)

    d_in = w_in.shape[1]
    d_in_pad = -(-d_in // LANES) * LANES
    w_in_p = jnp.pad(w_in, ((0, 0), (0, d_in_pad - d_in))).astype(BF16)
    z = _norm_matmul(x2, ln1_g, w_in_p, tm=ROW_TILE,
                     tn=d_in_pad // 3 if d_in_pad % (3 * LANES) == 0 else LANES)
    z3 = z.reshape(B, S, d_in_pad)

    slopes = 2.0 ** (-8.0 * jnp.arange(1, H + 1, dtype=F32) / H)
    o_a = _dilated_attention(z3, slopes, n_heads=H)

    wq = w_uq.reshape(q_lora, H, QK_NOPE + QK_ROPE)
    wq_pe = jnp.pad(wq[:, :, QK_NOPE:], ((0, 0), (0, 0), (0, LANES - QK_ROPE)))
    wq_p = jnp.concatenate([wq[:, :, :QK_NOPE].reshape(q_lora, H * QK_NOPE),
                            wq_pe.reshape(q_lora, H * LANES)], axis=1).astype(BF16)
    wkv = w_ukv.reshape(kv_lora, H, QK_NOPE + HEAD_DIM)
    wkv_p = jnp.concatenate([wkv[:, :, :QK_NOPE].reshape(kv_lora, H * QK_NOPE),
                             wkv[:, :, QK_NOPE:].reshape(kv_lora, H * HEAD_DIM)], axis=1).astype(BF16)
    assert q_lora == kv_lora and (3 * d_dil) % q_lora == 0
    q_mla = _norm_matmul(z, q_a_norm, wq_p, xcol=3 * d_dil // q_lora, kdim=q_lora, out_dtype=BF16,
                         tm=ROW_TILE, tn=wq_p.shape[1])
    kv_mla = _norm_matmul(z, kv_a_norm, wkv_p, xcol=3 * d_dil // q_lora + 1, kdim=kv_lora, out_dtype=BF16,
                          tm=ROW_TILE, tn=wkv_p.shape[1])
    cos, sin_signed = _rope_tables(S)
    o_b = _mla_attention(q_mla.reshape(B, S, -1), kv_mla.reshape(B, S, -1), z3,
                         (3 * d_dil + q_lora + kv_lora) // LANES, cos, sin_signed, n_heads=H)

    h1 = _outproj(o_a.reshape(T, d_dil), o_b.reshape(T, -1), out_norm_dil, out_norm_mla,
                  w_o.astype(BF16), x2, tm=ROW_TILE)

    q_slabs, xn2 = _norm_matmul(h1, ln2_g, peer_wq.astype(BF16), tm=ROW_TILE, emit_xn=True,
                                slab_out=True, out_dtype=BF16)
    keys = peer_sub_keys.reshape(PEER_HEADS * 2, PEER_NKEYS, -1).astype(BF16)
    idx_t, gate_t = _peer_topk(q_slabs, keys)
    table = _expert_table(peer_u, peer_v)
    out = _peer_ffn(idx_t.T, gate_t.T, xn2, h1, final_g.reshape(D // LANES, LANES), table)
    return out.reshape(B, S, D)


def kernel(x, ln1_g, w_in, q_a_norm, kv_a_norm, w_uq, w_ukv, out_norm_dil, out_norm_mla, w_o,
           ln2_g, peer_wq, peer_sub_keys, peer_u, peer_v, lnf_g):
    assert ln1_g.shape[0] == 1, "single-layer trunk"
    return _layer(x, ln1_g[0], w_in[0], q_a_norm[0], kv_a_norm[0], w_uq[0], w_ukv[0],
                  out_norm_dil[0], out_norm_mla[0], w_o[0], ln2_g[0], peer_wq[0],
                  peer_sub_keys[0], peer_u[0], peer_v[0], lnf_g)
```

```python
import functools

import jax
import jax.numpy as jnp
from jax import lax
from jax.experimental import pallas as pl
from jax.experimental.pallas import tpu as pltpu

F32 = jnp.float32
BF16 = jnp.bfloat16

EPS = 1e-6
NEG = -1e30
HEAD_DIM = 128
BLOCK = 128
DIL_PATTERNS = ((128, 1), (512, 4), (2048, 16))
DIL_UNROLL = 16
N_HEADS = 8
QK_NOPE = 128
QK_ROPE = 64
ROPE_THETA = 10000.0
PEER_HEADS = 8
PEER_NKEYS = 128
PEER_TOPK = 16
PEER_E = PEER_HEADS * PEER_TOPK

LANES = 128
SUBLANES = 8
VMEM_CAP = 60000 * 1024
ROW_TILE = 1024


def _vmem_limit(nbytes):
    return int(min(VMEM_CAP, max(16 * 1024 * 1024, nbytes * 3 // 2)))


def _rms(x, g):
    return x * lax.rsqrt(jnp.mean(x * x, axis=-1, keepdims=True) + EPS) * g


def _norm_matmul_body(x_ref, g_ref, w_ref, *rest, emit_xn, slabs):
    if emit_xn:
        o_ref, xn_out_ref, xn_ref = rest
    else:
        o_ref, xn_ref = rest
    j = pl.program_id(1)

    @pl.when(j == 0)
    def _():
        xn = _rms(x_ref[...].astype(F32), g_ref[...])
        xn_ref[...] = xn.astype(BF16)
        if emit_xn:
            xn_out_ref[...] = xn

    res = jnp.dot(xn_ref[...], w_ref[...], preferred_element_type=F32).astype(o_ref.dtype)
    if slabs:
        for s in range(slabs):
            o_ref[s] = res[:, s * LANES:(s + 1) * LANES]
    else:
        o_ref[...] = res


def _norm_matmul(x, g, w, *, xcol=0, kdim=None, tm=512, tn=512, emit_xn=False, slab_out=False,
                 out_dtype=F32):
    T = x.shape[0]
    kdim = kdim or x.shape[1]
    N = w.shape[1]
    tm = min(tm, T)
    tn = min(tn, N)
    assert T % tm == 0 and N % tn == 0 and w.shape[0] == kdim
    slabs = tn // LANES if slab_out else 0
    if slab_out:
        out_shape = [jax.ShapeDtypeStruct((N // LANES, T, LANES), out_dtype)]
        out_specs = [pl.BlockSpec((slabs, tm, LANES), lambda i, j: (j, i, 0))]
    else:
        out_shape = [jax.ShapeDtypeStruct((T, N), out_dtype)]
        out_specs = [pl.BlockSpec((tm, tn), lambda i, j: (i, j))]
    if emit_xn:
        out_shape.append(jax.ShapeDtypeStruct((T, kdim), F32))
        out_specs.append(pl.BlockSpec((tm, kdim), lambda i, j: (i, 0)))
    est = 2 * (tm * kdim * 4 + kdim * tn * 2 + tm * tn * 4) + tm * kdim * 2
    if emit_xn:
        est += 2 * tm * kdim * 4
    outs = pl.pallas_call(
        functools.partial(_norm_matmul_body, emit_xn=emit_xn, slabs=slabs),
        grid=(T // tm, N // tn),
        in_specs=[
            pl.BlockSpec((tm, kdim), lambda i, j: (i, xcol)),
            pl.BlockSpec((1, kdim), lambda i, j: (0, 0)),
            pl.BlockSpec((kdim, tn), lambda i, j: (0, j)),
        ],
        out_specs=out_specs,
        out_shape=out_shape,
        scratch_shapes=[pltpu.VMEM((tm, kdim), BF16)],
        compiler_params=pltpu.CompilerParams(
            dimension_semantics=("arbitrary", "arbitrary"),
            vmem_limit_bytes=_vmem_limit(est)),
        name="norm_matmul",
    )(x, g.reshape(1, kdim).astype(F32), w)
    return outs if emit_xn else outs[0]


def _dilated_body(slopes_ref, q_ref, k_ref, v_ref, o_ref, o_scr, l_scr, *, seq, patterns, scale):
    h = pl.program_id(1)
    slope = slopes_ref[h]
    qi = lax.broadcasted_iota(jnp.int32, (BLOCK, 2 * BLOCK), 0)
    kj = lax.broadcasted_iota(jnp.int32, (BLOCK, 2 * BLOCK), 1)
    delta = qi + BLOCK - kj
    nt = (((1,), (1,)), ((), ()))

    for p, (window, d) in enumerate(patterns):
        steps = window // d
        nb = seq // d // BLOCK
        in_window = (delta >= 0) & (delta <= steps)
        bias = -slope * (delta * d).astype(F32)

        def block(t, carry, d=d, nb=nb, in_window=in_window, bias=bias, p=p):
            r = t // nb
            n = t % nb
            start = n * (BLOCK * d) + r
            pstart = jnp.maximum(n - 1, 0) * (BLOCK * d) + r
            rows = pl.ds(start, BLOCK, stride=d) if d > 1 else pl.ds(start, BLOCK)
            prow = pl.ds(pstart, BLOCK, stride=d) if d > 1 else pl.ds(pstart, BLOCK)
            q = q_ref[0, rows, :].astype(BF16)
            kk = jnp.concatenate([k_ref[0, prow, :], k_ref[0, rows, :]], axis=0).astype(BF16)
            vv = jnp.concatenate([v_ref[0, prow, :], v_ref[0, rows, :]], axis=0).astype(BF16)
            s = lax.dot_general(q, kk, nt, preferred_element_type=F32) * scale
            valid = in_window & ((kj >= BLOCK) | (n > 0))
            s = jnp.where(valid, s + bias, NEG)
            m = jnp.max(s, axis=-1, keepdims=True)
            e = jnp.exp(s - m)
            l = jnp.sum(e, axis=-1, keepdims=True)
            o = jnp.dot((e / l).astype(BF16), vv, preferred_element_type=F32)
            o_scr[p, rows, :] = o
            l_scr[p, rows, :] = jnp.broadcast_to(m + jnp.log(l), (BLOCK, HEAD_DIM))
            return carry

        lax.fori_loop(0, d * nb, block, 0, unroll=DIL_UNROLL)

    def mix(c, carry):
        rows = pl.ds(pl.multiple_of(c * BLOCK, BLOCK), BLOCK)
        ls = [l_scr[p, rows, :] for p in range(len(patterns))]
        m = functools.reduce(jnp.maximum, ls)
        es = [jnp.exp(l - m) for l in ls]
        den = functools.reduce(jnp.add, es)
        num = functools.reduce(jnp.add, [e * o_scr[p, rows, :] for p, e in enumerate(es)])
        o_ref[0, rows, :] = (num / den).astype(o_ref.dtype)
        return carry

    lax.fori_loop(0, seq // BLOCK, mix, 0)


def _dilated_attention(z3, slopes, *, n_heads, patterns=DIL_PATTERNS):
    B, S, _ = z3.shape
    for _, d in patterns:
        assert S % (d * BLOCK) == 0
    blk = (1, S, HEAD_DIM)
    est = 2 * 4 * S * HEAD_DIM * 4 + 2 * len(patterns) * S * HEAD_DIM * 4
    return pl.pallas_call(
        functools.partial(_dilated_body, seq=S, patterns=patterns, scale=HEAD_DIM ** -0.5),
        grid=(B, n_heads),
        in_specs=[
            pl.BlockSpec(memory_space=pltpu.SMEM),
            pl.BlockSpec(blk, lambda b, h: (b, 0, h)),
            pl.BlockSpec(blk, lambda b, h: (b, 0, n_heads + h)),
            pl.BlockSpec(blk, lambda b, h: (b, 0, 2 * n_heads + h)),
        ],
        out_specs=pl.BlockSpec(blk, lambda b, h: (b, 0, h)),
        out_shape=jax.ShapeDtypeStruct((B, S, n_heads * HEAD_DIM), BF16),
        scratch_shapes=[pltpu.VMEM((len(patterns), S, HEAD_DIM), F32),
                        pltpu.VMEM((len(patterns), S, HEAD_DIM), F32)],
        compiler_params=pltpu.CompilerParams(
            dimension_semantics=("arbitrary", "arbitrary"),
            vmem_limit_bytes=_vmem_limit(est)),
        name="dilated_attention",
    )(slopes, z3, z3, z3)


def _rope(x, cos, sin_signed):
    lane = lax.broadcasted_iota(jnp.int32, x.shape, 1)
    half = QK_ROPE // 2
    swapped = jnp.where(lane < half, pltpu.roll(x, LANES - half, 1), pltpu.roll(x, half, 1))
    return x * cos + swapped * sin_signed


def _mla_body(qn_ref, qp_ref, kn_ref, v_ref, kr_ref, cos_ref, sin_ref,
              o_ref, qcat_scr, kcat_scr, *, seq, tq, scale):
    cos, sin = cos_ref[...], sin_ref[...]
    qcat_scr[:, :LANES] = qn_ref[0]
    qcat_scr[:, LANES:] = _rope(qp_ref[0].astype(F32), cos, sin).astype(BF16)
    kcat_scr[:, :LANES] = kn_ref[0]
    kcat_scr[:, LANES:] = _rope(kr_ref[0], cos, sin).astype(BF16)
    nt = (((1,), (1,)), ((), ()))

    for qi in range(seq // tq):
        ext = (qi + 1) * tq
        rows = slice(qi * tq, ext)
        s = lax.dot_general(qcat_scr[rows, :], kcat_scr[:ext, :], nt, preferred_element_type=F32) * scale
        qpos = qi * tq + lax.broadcasted_iota(jnp.int32, (tq, ext), 0)
        kpos = lax.broadcasted_iota(jnp.int32, (tq, ext), 1)
        s = jnp.where(kpos <= qpos, s, NEG)
        e = jnp.exp(s - jnp.max(s, axis=-1, keepdims=True))
        l = jnp.sum(e, axis=-1, keepdims=True)
        o = jnp.dot(e.astype(BF16), v_ref[0, :ext, :], preferred_element_type=F32) / l
        o_ref[0, rows, :] = o.astype(o_ref.dtype)


def _mla_attention(q3, kv3, z3, kr_col, cos, sin_signed, *, n_heads, tq=512):
    B, S, _ = q3.shape
    tq = min(tq, S)
    assert S % tq == 0 and q3.dtype == BF16 and kv3.dtype == BF16
    blk = (1, S, LANES)
    est = 2 * (5 * 2 + 3 * 4) * S * LANES + 4 * S * LANES * 2 + 3 * tq * S * 4
    return pl.pallas_call(
        functools.partial(_mla_body, seq=S, tq=tq, scale=(QK_NOPE + QK_ROPE) ** -0.5),
        grid=(B, n_heads),
        in_specs=[
            pl.BlockSpec(blk, lambda b, h: (b, 0, h)),
            pl.BlockSpec(blk, lambda b, h: (b, 0, n_heads + h)),
            pl.BlockSpec(blk, lambda b, h: (b, 0, h)),
            pl.BlockSpec(blk, lambda b, h: (b, 0, n_heads + h)),
            pl.BlockSpec(blk, lambda b, h: (b, 0, kr_col)),
            pl.BlockSpec((S, LANES), lambda b, h: (0, 0)),
            pl.BlockSpec((S, LANES), lambda b, h: (0, 0)),
        ],
        out_specs=pl.BlockSpec(blk, lambda b, h: (b, 0, h)),
        out_shape=jax.ShapeDtypeStruct((B, S, n_heads * HEAD_DIM), BF16),
        scratch_shapes=[pltpu.VMEM((S, 2 * LANES), BF16), pltpu.VMEM((S, 2 * LANES), BF16)],
        compiler_params=pltpu.CompilerParams(
            dimension_semantics=("arbitrary", "arbitrary"),
            vmem_limit_bytes=_vmem_limit(est)),
        name="mla_attention",
    )(q3, q3, kv3, kv3, z3, cos, sin_signed)


def _outproj_body(oa_ref, ob_ref, ga_ref, gb_ref, w_ref, x_ref, o_ref, xn_ref, *, da):
    j = pl.program_id(1)

    @pl.when(j == 0)
    def _():
        xn_ref[:, :da] = _rms(oa_ref[...].astype(F32), ga_ref[...]).astype(BF16)
        xn_ref[:, da:] = _rms(ob_ref[...].astype(F32), gb_ref[...]).astype(BF16)

    o_ref[...] = x_ref[...] + jnp.dot(xn_ref[...], w_ref[...], preferred_element_type=F32)


def _outproj(oa, ob, ga, gb, w, x, *, tm=512, tn=512):
    T, da = oa.shape
    db = ob.shape[1]
    N = w.shape[1]
    tm, tn = min(tm, T), min(tn, N)
    assert T % tm == 0 and N % tn == 0 and w.shape[0] == da + db
    est = 2 * (tm * (da + db) * 4 + (da + db) * tn * 2 + 2 * tm * tn * 4) + tm * (da + db) * 2
    return pl.pallas_call(
        functools.partial(_outproj_body, da=da),
        grid=(T // tm, N // tn),
        in_specs=[
            pl.BlockSpec((tm, da), lambda i, j: (i, 0)),
            pl.BlockSpec((tm, db), lambda i, j: (i, 0)),
            pl.BlockSpec((1, da), lambda i, j: (0, 0)),
            pl.BlockSpec((1, db), lambda i, j: (0, 0)),
            pl.BlockSpec((da + db, tn), lambda i, j: (0, j)),
            pl.BlockSpec((tm, tn), lambda i, j: (i, j)),
        ],
        out_specs=pl.BlockSpec((tm, tn), lambda i, j: (i, j)),
        out_shape=jax.ShapeDtypeStruct((T, N), F32),
        scratch_shapes=[pltpu.VMEM((tm, da + db), BF16)],
        compiler_params=pltpu.CompilerParams(
            dimension_semantics=("arbitrary", "arbitrary"),
            vmem_limit_bytes=_vmem_limit(est)),
        name="out_projection",
    )(oa, ob, ga.reshape(1, da), gb.reshape(1, db), w, x)


def _outproj_query_body(oa_ref, ob_ref, ga_ref, gb_ref, wo_ref, x_ref, g2_ref, wq_ref,
                        h_ref, q_ref, xn_ref, *, slabs):
    na = _rms(oa_ref[...].astype(F32), ga_ref[...]).astype(BF16)
    nb = _rms(ob_ref[...].astype(F32), gb_ref[...]).astype(BF16)
    h = x_ref[...] + jnp.dot(jnp.concatenate([na, nb], axis=1), wo_ref[...], preferred_element_type=F32)
    h_ref[...] = h
    xn = _rms(h, g2_ref[...])
    xn_ref[...] = xn
    q = jnp.dot(xn.astype(BF16), wq_ref[...], preferred_element_type=F32).astype(q_ref.dtype)
    for s in range(slabs):
        q_ref[s] = q[:, s * LANES:(s + 1) * LANES]


def _outproj_query(oa, ob, ga, gb, wo, x, g2, wq, *, tm=256):
    T, da = oa.shape
    db = ob.shape[1]
    D = wo.shape[1]
    nq = wq.shape[1]
    tm = min(tm, T)
    assert T % tm == 0 and wo.shape[0] == da + db and wq.shape[0] == D and nq % LANES == 0
    slabs = nq // LANES
    est = 2 * ((da + db) * D * 2 + D * nq * 2) + 2 * tm * ((da + db) * 2 + 3 * D * 4 + nq * 2) + 4 * tm * D * 4
    row = lambda i: (i, 0)
    fixed = lambda i: (0, 0)
    return pl.pallas_call(
        functools.partial(_outproj_query_body, slabs=slabs),
        grid=(T // tm,),
        in_specs=[
            pl.BlockSpec((tm, da), row), pl.BlockSpec((tm, db), row),
            pl.BlockSpec((1, da), fixed), pl.BlockSpec((1, db), fixed),
            pl.BlockSpec((da + db, D), fixed), pl.BlockSpec((tm, D), row),
            pl.BlockSpec((1, D), fixed), pl.BlockSpec((D, nq), fixed),
        ],
        out_specs=[pl.BlockSpec((tm, D), row),
                   pl.BlockSpec((slabs, tm, LANES), lambda i: (0, i, 0)),
                   pl.BlockSpec((tm, D), row)],
        out_shape=[jax.ShapeDtypeStruct((T, D), F32),
                   jax.ShapeDtypeStruct((slabs, T, LANES), BF16),
                   jax.ShapeDtypeStruct((T, D), F32)],
        compiler_params=pltpu.CompilerParams(
            dimension_semantics=("arbitrary",),
            vmem_limit_bytes=_vmem_limit(est)),
        name="out_projection_peer_query",
    )(oa, ob, ga.reshape(1, da), gb.reshape(1, db), wo, x, g2.reshape(1, D), wq)


TOPK_HEADS = 4


def _candidate_blocks(topk):
    blocks = []
    a = 0
    while a < topk:
        nb = topk // (a + 1)
        if nb >= SUBLANES:
            blocks += [(a, 1, b0, min(SUBLANES, nb - b0)) for b0 in range(0, nb, SUBLANES)]
            a += 1
        elif nb > 1:
            blocks.append((a, 1, 0, nb))
            a += 1
        else:
            na = min(SUBLANES, topk - a)
            blocks.append((a, na, 0, 1))
            a += na
    return blocks


def _topk_body(q_ref, keys_ref, idx_ref, gate_ref, s_ref, sv_ref, si_ref, cs_ref, ci_ref, bs_ref, be_ref,
               *, tt, nkeys, topk, blocks):
    lowest = float(jnp.finfo(jnp.float32).min)
    nchain = 2 * TOPK_HEADS
    iota_n = lax.broadcasted_iota(jnp.int32, (nkeys, tt), 0)
    sub = lax.broadcasted_iota(jnp.int32, (SUBLANES, tt), 0)
    nt = (((1,), (1,)), ((), ()))

    for ch in range(nchain):
        s_ref[ch] = lax.dot_general(keys_ref[ch], q_ref[ch], nt, preferred_element_type=F32)

    def pick(k, carry):
        for ch in range(nchain):
            s = s_ref[ch]
            m = jnp.max(s, axis=0, keepdims=True)
            ix = jnp.min(jnp.where(s == m, iota_n, nkeys), axis=0, keepdims=True)
            sv_ref[ch, pl.ds(k, 1), :] = m
            si_ref[ch, pl.ds(k, 1), :] = ix
            s_ref[ch] = jnp.where(iota_n == ix, lowest, s)
        return carry

    lax.fori_loop(0, topk, pick, 0)

    pos_blocks = []
    for r, (a0, na, b0, nb) in enumerate(blocks):
        rows = slice(r * SUBLANES, (r + 1) * SUBLANES)
        if na == 1:
            live = sub < nb
            pos_blocks.append(a0 * topk + b0 + sub)
        else:
            live = sub < na
            pos_blocks.append((a0 + sub) * topk)
        for hd in range(TOPK_HEADS):
            if na == 1:
                cs = sv_ref[2 * hd, a0:a0 + 1, :] + sv_ref[2 * hd + 1, b0:b0 + SUBLANES, :]
                ci = si_ref[2 * hd, a0:a0 + 1, :] * nkeys + si_ref[2 * hd + 1, b0:b0 + SUBLANES, :]
            else:
                cs = sv_ref[2 * hd, a0:a0 + SUBLANES, :] + sv_ref[2 * hd + 1, 0:1, :]
                ci = si_ref[2 * hd, a0:a0 + SUBLANES, :] * nkeys + si_ref[2 * hd + 1, 0:1, :]
            cs_ref[hd, rows, :] = jnp.where(live, cs, lowest)
            ci_ref[hd, rows, :] = ci
    pos = jnp.concatenate(pos_blocks, axis=0)
    big = topk * topk

    def pick2(k, carry):
        for hd in range(TOPK_HEADS):
            c = cs_ref[hd]
            m = jnp.max(c, axis=0, keepdims=True)
            first = jnp.min(jnp.where(c == m, pos, big), axis=0, keepdims=True)
            hit = pos == first
            bs_ref[hd, pl.ds(k, 1), :] = m
            be_ref[hd, pl.ds(k, 1), :] = jnp.max(jnp.where(hit, ci_ref[hd], -1), axis=0, keepdims=True)
            cs_ref[hd] = jnp.where(hit, lowest, c)
        return carry

    lax.fori_loop(0, topk, pick2, 0)

    for hd in range(TOPK_HEADS):
        b = bs_ref[hd]
        e = jnp.exp(b - jnp.max(b, axis=0, keepdims=True))
        gate_ref[hd * topk:(hd + 1) * topk, :] = e / jnp.sum(e, axis=0, keepdims=True)
        idx_ref[hd * topk:(hd + 1) * topk, :] = be_ref[hd]


def _peer_topk(q_slabs, keys, *, tt=128):
    hp, T, c = q_slabs.shape
    heads = hp // 2
    nkeys = keys.shape[1]
    tt = min(tt, T)
    topk = PEER_TOPK
    assert T % tt == 0 and heads % TOPK_HEADS == 0 and topk % SUBLANES == 0
    blocks = _candidate_blocks(topk)
    ncand = len(blocks) * SUBLANES
    nchain = 2 * TOPK_HEADS
    rows = TOPK_HEADS * topk
    return pl.pallas_call(
        functools.partial(_topk_body, tt=tt, nkeys=nkeys, topk=topk, blocks=blocks),
        grid=(T // tt, heads // TOPK_HEADS),
        in_specs=[
            pl.BlockSpec((nchain, tt, c), lambda i, h: (h, i, 0)),
            pl.BlockSpec((nchain, nkeys, c), lambda i, h: (h, 0, 0)),
        ],
        out_specs=[pl.BlockSpec((rows, tt), lambda i, h: (h, i)),
                   pl.BlockSpec((rows, tt), lambda i, h: (h, i))],
        out_shape=[jax.ShapeDtypeStruct((heads * topk, T), jnp.int32),
                   jax.ShapeDtypeStruct((heads * topk, T), F32)],
        scratch_shapes=[pltpu.VMEM((nchain, nkeys, tt), F32),
                        pltpu.VMEM((nchain, topk, tt), F32), pltpu.VMEM((nchain, topk, tt), jnp.int32),
                        pltpu.VMEM((TOPK_HEADS, ncand, tt), F32), pltpu.VMEM((TOPK_HEADS, ncand, tt), jnp.int32),
                        pltpu.VMEM((TOPK_HEADS, topk, tt), F32), pltpu.VMEM((TOPK_HEADS, topk, tt), jnp.int32)],
        compiler_params=pltpu.CompilerParams(dimension_semantics=("arbitrary", "arbitrary")),
        name="peer_topk",
    )(q_slabs, keys)


GSIZE = SUBLANES
NGROUP = PEER_E // GSIZE
DOT_CHAINS = 4
FFN_SLOTS = 4
FFN_PHASES = 8
FFN_AHEAD = 2


def _gelu_exact(x):
    return 0.5 * x * (1.0 + lax.erf(x * (2.0 ** -0.5)))


def _peer_ffn_body(idx_hbm, gate_ref, x_ref, h_ref, gf_ref, tab_ref, o_ref, *scratch,
                   ctok, nrow, ntok):
    bufs = scratch[:FFN_SLOTS]
    idx_a, idx_b, sem, isem = scratch[FFN_SLOTS:]
    i = pl.program_id(0)
    n = pl.num_programs(0)
    sub = lax.broadcasted_iota(jnp.int32, (SUBLANES, LANES), 0)
    lane = lax.broadcasted_iota(jnp.int32, (SUBLANES, LANES), 1)
    step = FFN_PHASES * ctok
    half = step // 2
    tile = 2 * nrow

    def idx_copy(part, base):
        lo, cnt, dst = ((0, half // 2, idx_a), (half // 2, half // 2, idx_a), (half, half, idx_b))[part]
        src = idx_hbm.at[pl.ds(pl.multiple_of(base + lo, SUBLANES), cnt), :]
        return pltpu.make_async_copy(src, dst.at[pl.ds(lo % half, cnt), :], isem.at[part])

    next_base = jnp.minimum((i + 1) * step, ntok - step)

    def idx_of(tok, j):
        return idx_a[tok, j] if tok < half else idx_b[tok - half, j]

    def start_row(tok, dst_slot, c, j):
        pltpu.make_async_copy(tab_ref.at[idx_of(tok, j)], bufs[dst_slot].at[pl.ds((c * PEER_E + j) * tile, tile), :],
                              sem.at[dst_slot]).start(priority=j % 2)

    def wait_slot(s):
        pltpu.make_async_copy(bufs[s], bufs[s], sem.at[s]).wait()

    def expert_tile(slot, c, j, which):
        return bufs[slot][pl.ds((c * PEER_E + j) * tile + which * nrow, nrow), :].astype(F32)

    def token_tile(ref, tok):
        return jnp.concatenate([ref[tok:tok + 1, s * LANES:(s + 1) * LANES] for s in range(nrow)], axis=0)

    @pl.when(i == 0)
    def _():
        first_ids = idx_copy(0, 0)
        first_ids.start()
        first_ids.wait()
        idx_copy(1, 0).start()
        idx_copy(2, 0).start()
        for p in range(FFN_AHEAD):
            def first(c, carry, p=p):
                for j in range(PEER_E):
                    first_row = pl.multiple_of((c * PEER_E + j) * tile, tile)
                    pltpu.make_async_copy(tab_ref.at[idx_a[p * ctok + c, j]],
                                          bufs[p].at[pl.ds(first_row, tile), :], sem.at[p]).start(priority=j % 2)
                return carry
            lax.fori_loop(0, ctok, first, 0)

    idx_copy(0, next_base).start()
    idx_copy(1, 0).wait()
    results = []
    prev = None
    starts = iter(())
    for t in range(step + 1):
        cur = t < step
        if cur:
            p, c = divmod(t, ctok)
            if c == 0:
                if p * ctok == half // 2:
                    idx_copy(1, next_base).start()
                    idx_copy(2, 0).wait()
                if (p + FFN_AHEAD) * ctok == step:
                    idx_copy(2, next_base).start()
                    idx_copy(0, 0).wait()
                slot = p % FFN_SLOTS
                wait_slot(slot)
                ahead_slot = (p + FFN_AHEAD) % FFN_SLOTS
                ahead_tok0 = ((p + FFN_AHEAD) % FFN_PHASES) * ctok
                starts = iter([(ahead_tok0 + cc, ahead_slot, cc, j) for cc in range(ctok) for j in range(PEER_E)])
            xt = token_tile(x_ref, t)
            partial = [jnp.zeros((SUBLANES, LANES), F32) for _ in range(DOT_CHAINS)]
        ya = [jnp.zeros((SUBLANES, LANES), F32) for _ in range(4)]
        for g in range(NGROUP):
            if prev is not None:
                onto = jnp.sum(jnp.where(lane == g * GSIZE + sub, prev[3], 0.0), axis=1, keepdims=True)
                a_g = jnp.broadcast_to(onto, (SUBLANES, LANES))
            for k in range(GSIZE):
                e = g * GSIZE + k
                if cur:
                    args = next(starts, None)
                    if args is not None:
                        start_row(*args)
                    q = expert_tile(slot, c, e, 0) * xt
                    col = jnp.sum(q[:SUBLANES] + q[SUBLANES:], axis=1, keepdims=True)
                    partial[e % DOT_CHAINS] = jnp.where(lane == e, col, partial[e % DOT_CHAINS])
                if prev is not None:
                    a = jnp.broadcast_to(a_g[k:k + 1, :], (SUBLANES, LANES))
                    vt = expert_tile(prev[0], prev[1], e, 1)
                    ya[2 * (k % 2)] = ya[2 * (k % 2)] + a * vt[:SUBLANES]
                    ya[2 * (k % 2) + 1] = ya[2 * (k % 2) + 1] + a * vt[SUBLANES:]
        if prev is not None:
            z = token_tile(h_ref, prev[2]) + jnp.concatenate([ya[0] + ya[2], ya[1] + ya[3]], axis=0)
            ms = jnp.sum(jnp.sum(z * z, axis=1, keepdims=True), axis=0, keepdims=True) * (1.0 / (nrow * LANES))
            results.append((prev[2], z * lax.rsqrt(ms + EPS) * gf_ref[...]))
            if len(results) == SUBLANES:
                for tok, zn in results:
                    for s in range(nrow):
                        o_ref[tok:tok + 1, s * LANES:(s + 1) * LANES] = zn[s:s + 1, :]
                results = []
        if cur:
            dots = jnp.sum(functools.reduce(jnp.add, partial), axis=0, keepdims=True)
            act = _gelu_exact(dots) * gate_ref[t:t + 1, :]
            prev = (slot, c, t, jnp.broadcast_to(act, (SUBLANES, LANES)))
        else:
            prev = None

    @pl.when(i == n - 1)
    def _():
        for p in range(FFN_AHEAD):
            wait_slot(p)
        idx_copy(1, 0).wait()
        idx_copy(2, 0).wait()


def _peer_ffn(idx, gate, xn, h, gf, table, *, ctok=4):
    T, nexp = idx.shape
    D = xn.shape[1]
    nrow = D // LANES
    step = FFN_PHASES * ctok
    half = step // 2
    assert nexp == PEER_E and SUBLANES % ctok == 0 and half % (2 * SUBLANES) == 0 and T % step == 0 and T >= 2 * step
    assert table.shape[1:] == (2 * nrow, LANES) and table.dtype == BF16 and nrow % (2 * SUBLANES) == 0
    assert FFN_PHASES % FFN_SLOTS == 0 and 0 < FFN_AHEAD < FFN_SLOTS and FFN_AHEAD * ctok == half // 2
    n = T // step
    slot_rows = ctok * nexp * 2 * nrow
    est = FFN_SLOTS * slot_rows * LANES * 2 + 6 * step * D * 4
    return pl.pallas_call(
        functools.partial(_peer_ffn_body, ctok=ctok, nrow=nrow, ntok=T),
        grid=(n,),
        in_specs=[
            pl.BlockSpec(memory_space=pl.ANY),
            pl.BlockSpec((step, nexp), lambda i: (i, 0)),
            pl.BlockSpec((step, D), lambda i: (i, 0)),
            pl.BlockSpec((step, D), lambda i: (i, 0)),
            pl.BlockSpec((nrow, LANES), lambda i: (0, 0)),
            pl.BlockSpec(memory_space=pl.ANY),
        ],
        out_specs=pl.BlockSpec((step, D), lambda i: (i, 0)),
        out_shape=jax.ShapeDtypeStruct((T, D), F32),
        scratch_shapes=[pltpu.VMEM((slot_rows, LANES), BF16) for _ in range(FFN_SLOTS)]
                       + [pltpu.SMEM((half, nexp), jnp.int32), pltpu.SMEM((half, nexp), jnp.int32),
                          pltpu.SemaphoreType.DMA((FFN_SLOTS,)), pltpu.SemaphoreType.DMA((3,))],
        compiler_params=pltpu.CompilerParams(
            dimension_semantics=("arbitrary",),
            vmem_limit_bytes=_vmem_limit(est)),
        name="peer_ffn",
    )(idx, gate, xn, h, gf, table)


def _expert_table(u, v):
    n, d = u.shape
    both = jnp.concatenate([u.astype(BF16).reshape(n, d // LANES, LANES),
                            v.astype(BF16).reshape(n, d // LANES, LANES)], axis=1)
    return both


def _rope_tables(seq):
    half = QK_ROPE // 2
    freqs = ROPE_THETA ** (-jnp.arange(half, dtype=F32) / half)
    ang = jnp.arange(seq, dtype=F32)[:, None] * freqs[None, :]
    cos, sin = jnp.cos(ang), jnp.sin(ang)
    zeros = jnp.zeros((seq, LANES - QK_ROPE), F32)
    return (jnp.concatenate([cos, cos, zeros], axis=1),
            jnp.concatenate([-sin, sin, zeros], axis=1))


def _layer(h, ln1_g, w_in, q_a_norm, kv_a_norm, w_uq, w_ukv, out_norm_dil, out_norm_mla, w_o,
           ln2_g, peer_wq, peer_sub_keys, peer_u, peer_v, final_g):
    B, S, D = h.shape
    T = B * S
    H = N_HEADS
    d_dil = H * HEAD_DIM
    q_lora = q_a_norm.shape[0]
    kv_lora = kv_a_norm.shape[0]
    x2 = h.reshape(T, D)

    d_in = w_in.shape[1]
    d_in_pad = -(-d_in // LANES) * LANES
    w_in_p = jnp.pad(w_in, ((0, 0), (0, d_in_pad - d_in))).astype(BF16)
    z = _norm_matmul(x2, ln1_g, w_in_p, tm=ROW_TILE,
                     tn=d_in_pad // 3 if d_in_pad % (3 * LANES) == 0 else LANES)
    z3 = z.reshape(B, S, d_in_pad)

    slopes = 2.0 ** (-8.0 * jnp.arange(1, H + 1, dtype=F32) / H)
    o_a = _dilated_attention(z3, slopes, n_heads=H)

    wq = w_uq.reshape(q_lora, H, QK_NOPE + QK_ROPE)
    wq_pe = jnp.pad(wq[:, :, QK_NOPE:], ((0, 0), (0, 0), (0, LANES - QK_ROPE)))
    wq_p = jnp.concatenate([wq[:, :, :QK_NOPE].reshape(q_lora, H * QK_NOPE),
                            wq_pe.reshape(q_lora, H * LANES)], axis=1).astype(BF16)
    wkv = w_ukv.reshape(kv_lora, H, QK_NOPE + HEAD_DIM)
    wkv_p = jnp.concatenate([wkv[:, :, :QK_NOPE].reshape(kv_lora, H * QK_NOPE),
                             wkv[:, :, QK_NOPE:].reshape(kv_lora, H * HEAD_DIM)], axis=1).astype(BF16)
    assert q_lora == kv_lora and (3 * d_dil) % q_lora == 0
    q_mla = _norm_matmul(z, q_a_norm, wq_p, xcol=3 * d_dil // q_lora, kdim=q_lora, out_dtype=BF16,
                         tm=ROW_TILE, tn=wq_p.shape[1])
    kv_mla = _norm_matmul(z, kv_a_norm, wkv_p, xcol=3 * d_dil // q_lora + 1, kdim=kv_lora, out_dtype=BF16,
                          tm=ROW_TILE, tn=wkv_p.shape[1])
    cos, sin_signed = _rope_tables(S)
    o_b = _mla_attention(q_mla.reshape(B, S, -1), kv_mla.reshape(B, S, -1), z3,
                         (3 * d_dil + q_lora + kv_lora) // LANES, cos, sin_signed, n_heads=H)

    h1, q_slabs, xn2 = _outproj_query(o_a.reshape(T, d_dil), o_b.reshape(T, -1), out_norm_dil, out_norm_mla,
                                      w_o.astype(BF16), x2, ln2_g, peer_wq.astype(BF16))
    keys = peer_sub_keys.reshape(PEER_HEADS * 2, PEER_NKEYS, -1).astype(BF16)
    idx_t, gate_t = _peer_topk(q_slabs, keys)
    table = _expert_table(peer_u, peer_v)
    out = _peer_ffn(idx_t.T, gate_t.T, xn2, h1, final_g.reshape(D // LANES, LANES), table)
    return out.reshape(B, S, D)


def kernel(x, ln1_g, w_in, q_a_norm, kv_a_norm, w_uq, w_ukv, out_norm_dil, out_norm_mla, w_o,
           ln2_g, peer_wq, peer_sub_keys, peer_u, peer_v, lnf_g):
    assert ln1_g.shape[0] == 1, "single-layer trunk"
    return _layer(x, ln1_g[0], w_in[0], q_a_norm[0], kv_a_norm[0], w_uq[0], w_ukv[0],
                  out_norm_dil[0], out_norm_mla[0], w_o[0], ln2_g[0], peer_wq[0],
                  peer_sub_keys[0], peer_u[0], peer_v[0], lnf_g)
```
